```python
import math
import jax, jax.numpy as jnp
from jax import lax
import numpy as np

D_MODEL = 1024
BATCH = 16
SEQ = 4096
DEPTH = 4

GRID_W = 64
CTX_LEN = 256
HEAD_DIM = 64
ATTN_WIDTH = D_MODEL // 2
ATTN_HEADS = ATTN_WIDTH // HEAD_DIM
KV_HEADS = ATTN_HEADS // 4
GQA_GROUP = ATTN_HEADS // KV_HEADS
Q_BLOCK = 128
ATTN_SCALE = HEAD_DIM ** -0.5
ROPE_THETA = 10000.0
ROPE_NF = HEAD_DIM // 4
DN_WIDTH = D_MODEL - ATTN_WIDTH
DN_DIM = 64
DN_HEADS = DN_WIDTH // DN_DIM
DN_SCALE = DN_DIM ** -0.5
CONV_K = 5
CHUNK = 64
N_GROUPS = 4
EXPERTS_PER_GROUP = 8
N_EXPERTS = N_GROUPS * EXPERTS_PER_GROUP
TOP_K = 2
EXPERT_FF = D_MODEL // 4
MOE_BLOCK = 128
EPS = 1e-6
PROJ_SIZES = (ATTN_WIDTH, KV_HEADS * HEAD_DIM, KV_HEADS * HEAD_DIM, 3 * DN_WIDTH, DN_WIDTH, 2 * DN_HEADS, 2 * DN_HEADS)
PROJ_DIM = sum(PROJ_SIZES)
SPLIT_IDX = tuple(int(v) for v in np.cumsum(PROJ_SIZES)[:-1])

kernel_name = "hybrid_gqa_gdn_hmoe_diffusion_trunk"


def rms_norm(x, w):
    xf = x.astype(jnp.float32)
    y = xf * lax.rsqrt(jnp.mean(xf * xf, axis=-1, keepdims=True) + EPS)
    return (y * w.astype(jnp.float32)).astype(x.dtype)


def l2_normalize(x):
    xf = x.astype(jnp.float32)
    return (xf * lax.rsqrt(jnp.sum(xf * xf, axis=-1, keepdims=True) + EPS)).astype(x.dtype)


def modulate(h, shift, scale):
    return h * (1.0 + scale.astype(h.dtype)) + shift.astype(h.dtype)


def axial_rope_tables(n):
    rows = n // GRID_W
    pos_row = jnp.repeat(jnp.arange(rows, dtype=jnp.float32), GRID_W)
    pos_col = jnp.tile(jnp.arange(GRID_W, dtype=jnp.float32), rows)
    inv = ROPE_THETA ** (-jnp.arange(ROPE_NF, dtype=jnp.float32) / ROPE_NF)
    ang = jnp.stack([pos_row, pos_col], axis=-1)[..., None] * inv
    return jnp.cos(ang), jnp.sin(ang)


def apply_rope(x, rope):
    cos, sin = rope
    B, S, H, _ = x.shape
    xf = x.astype(jnp.float32).reshape(B, S, H, 2, 2, ROPE_NF)
    c, s = cos[None, :, None], sin[None, :, None]
    x1, x2 = xf[..., 0, :], xf[..., 1, :]
    out = jnp.stack([x1 * c - x2 * s, x1 * s + x2 * c], axis=-2)
    return out.reshape(B, S, H, HEAD_DIM).astype(x.dtype)


def short_conv(x, w):
    return lax.conv_general_dilated(
        x, w[:, None, :].astype(x.dtype), window_strides=(1,),
        padding=((CONV_K // 2, CONV_K // 2),),
        dimension_numbers=("NWC", "WIO", "NWC"), feature_group_count=x.shape[-1])


def gqa_attend(q, k, v):
    s = jnp.einsum("bqkgd,bskd->bkgqs", q, k).astype(jnp.float32) * ATTN_SCALE
    p = jax.nn.softmax(s, axis=-1).astype(v.dtype)
    return jnp.einsum("bkgqs,bskd->bqkgd", p, v)


def gated_delta_chunked(q, k, v, g, beta, s0):
    out_dtype = v.dtype
    B, T, H, Dk = q.shape
    n = T // CHUNK
    f32 = jnp.float32
    to_chunks = lambda a: jnp.moveaxis(a.astype(f32).reshape(B, n, CHUNK, H, -1), 3, 1)
    qc, kc, vc = to_chunks(q), to_chunks(k), to_chunks(v)
    gcum = jnp.cumsum(to_chunks(g[..., None])[..., 0], axis=-1)
    bc = to_chunks(beta[..., None])
    incl = jnp.tril(jnp.ones((CHUNK, CHUNK), dtype=bool))
    strict = jnp.tril(jnp.ones((CHUNK, CHUNK), dtype=bool), -1)
    diff = gcum[..., :, None] - gcum[..., None, :]
    decay = jnp.where(incl, jnp.exp(jnp.where(incl, diff, 0.0)), 0.0)
    kb = kc * bc
    lower = jnp.where(strict, jnp.einsum("bhnid,bhnjd->bhnij", kb, kc) * decay, 0.0)
    eye = jnp.eye(CHUNK, dtype=f32)
    tmat = lax.linalg.triangular_solve(lower + eye, jnp.broadcast_to(eye, lower.shape),
                                       left_side=True, lower=True, unit_diagonal=True)
    u = jnp.einsum("bhnij,bhnje->bhnie", tmat, vc * bc)
    w = jnp.einsum("bhnij,bhnjd->bhnid", tmat, kb * jnp.exp(gcum)[..., None])
    intra = jnp.where(incl, jnp.einsum("bhnid,bhnjd->bhnij", qc, kc) * decay, 0.0)
    q_dec = qc * jnp.exp(gcum)[..., None]
    g_last = gcum[..., -1]
    k_dec = kc * jnp.exp(g_last[..., None] - gcum)[..., None]

    def step(state, xs):
        u_i, w_i, qd_i, kd_i, a_i, gl_i = xs
        v_new = u_i - jnp.einsum("bhid,bhde->bhie", w_i, state)
        o_i = jnp.einsum("bhid,bhde->bhie", qd_i, state) + jnp.einsum("bhij,bhje->bhie", a_i, v_new)
        state = state * jnp.exp(gl_i)[..., None, None] + jnp.einsum("bhid,bhie->bhde", kd_i, v_new)
        return state, o_i

    xs = tuple(jnp.moveaxis(t, 2, 0) for t in (u, w, q_dec, k_dec, intra, g_last))
    s_final, o = lax.scan(step, s0, xs)
    o = jnp.transpose(o, (1, 0, 3, 2, 4)).reshape(B, T, H, v.shape[-1])
    return o.astype(out_dtype), s_final


def mixer_inputs(h, lp, rope):
    B, S, _ = h.shape
    aq, ak, av, dqkv, dgate, dbeta, dalpha = jnp.split(h @ lp["w_in"], SPLIT_IDX, axis=-1)
    q = rms_norm(aq.reshape(B, S, ATTN_HEADS, HEAD_DIM), lp["q_norm_w"])
    k = rms_norm(ak.reshape(B, S, KV_HEADS, HEAD_DIM), lp["k_norm_w"])
    if rope is not None:
        q, k = apply_rope(q, rope), apply_rope(k, rope)
    dq, dk, dv = jnp.split(jax.nn.silu(short_conv(dqkv, lp["conv_w"])), 3, axis=-1)
    shp = (B, S, DN_HEADS, DN_DIM)
    beta = jax.nn.sigmoid(dbeta.astype(jnp.float32)).reshape(B, S, 2, DN_HEADS)
    g = -jnp.exp(lp["dn_A_log"].astype(jnp.float32)) * jax.nn.softplus(
        dalpha.astype(jnp.float32).reshape(B, S, 2, DN_HEADS) + lp["dn_dt_bias"].astype(jnp.float32))
    return dict(q=q, k=k, v=av.reshape(B, S, KV_HEADS, HEAD_DIM),
                dq=l2_normalize(dq.reshape(shp)) * DN_SCALE, dk=l2_normalize(dk.reshape(shp)),
                dv=dv.reshape(shp), gate=dgate.reshape(shp), beta=beta, g=g)


def merge_groups(a, o_delta, gate, lp):
    B, S = a.shape[:2]
    dn = rms_norm(o_delta, lp["dn_norm_w"]) * jax.nn.silu(gate)
    mixed = jnp.concatenate([a, dn.reshape(B, S, DN_WIDTH).astype(a.dtype)], axis=-1)
    return mixed @ lp["w_out"]


def hybrid_mixer(h_lat, h_ctx, lp, rope, ctx_out):
    B, S, _ = h_lat.shape
    L = h_ctx.shape[1]
    pl = mixer_inputs(h_lat, lp, rope)
    pc = mixer_inputs(h_ctx, lp, None)
    k_all = jnp.concatenate([pc["k"], pl["k"]], axis=1)
    v_all = jnp.concatenate([pc["v"], pl["v"]], axis=1)
    nb = S // Q_BLOCK
    qb = jnp.moveaxis(pl["q"].reshape(B, nb, Q_BLOCK, KV_HEADS, GQA_GROUP, HEAD_DIM), 1, 0)
    a_lat = lax.map(lambda qi: gqa_attend(qi, k_all, v_all), qb)
    a_lat = jnp.moveaxis(a_lat, 0, 1).reshape(B, S, ATTN_WIDTH)
    zero = jnp.zeros((B, DN_HEADS, DN_DIM, DN_DIM), jnp.float32)
    d_lat, d_ctx = 0.0, 0.0
    for d in range(2):
        rev = (lambda a: jnp.flip(a, axis=1)) if d == 1 else (lambda a: a)
        o_c, s_c = gated_delta_chunked(*(rev(t) for t in (pc["dq"], pc["dk"], pc["dv"], pc["g"][:, :, d], pc["beta"][:, :, d])), zero)
        o_l, _ = gated_delta_chunked(*(rev(t) for t in (pl["dq"], pl["dk"], pl["dv"], pl["g"][:, :, d], pl["beta"][:, :, d])), s_c)
        d_lat = d_lat + rev(o_l)
        d_ctx = d_ctx + rev(o_c)
    y_lat = merge_groups(a_lat, d_lat, pl["gate"], lp)
    if not ctx_out:
        return y_lat, None
    a_ctx = gqa_attend(pc["q"].reshape(B, L, KV_HEADS, GQA_GROUP, HEAD_DIM), pc["k"], pc["v"]).reshape(B, L, ATTN_WIDTH)
    y_ctx = merge_groups(a_ctx, d_ctx, pc["gate"], lp)
    return y_lat, y_ctx


def hier_moe(h, lp):
    N, D = h.shape
    g_logit = (h @ lp["rg_w"]).astype(jnp.float32) + lp["rg_b"].astype(jnp.float32)
    g_prob = jax.nn.softmax(g_logit, axis=-1)
    g_sel = jnp.argmax(g_logit, axis=-1)
    p_g = jnp.take_along_axis(g_prob, g_sel[:, None], axis=-1)
    e_logit = ((h @ lp["re_w"]).astype(jnp.float32) + lp["re_b"].astype(jnp.float32)).reshape(N, N_GROUPS, EXPERTS_PER_GROUP)
    e_prob = jax.nn.softmax(e_logit[jnp.arange(N), g_sel], axis=-1)
    top_p, top_i = lax.top_k(e_prob, TOP_K)
    wts = top_p / jnp.sum(top_p, axis=-1, keepdims=True) * p_g
    eid = (g_sel[:, None] * EXPERTS_PER_GROUP + top_i).reshape(-1).astype(jnp.int32)
    A = N * TOP_K
    tok = jnp.repeat(jnp.arange(N, dtype=jnp.int32), TOP_K)
    order = jnp.argsort(eid)
    e_s, tok_s, w_s = eid[order], tok[order], wts.reshape(-1)[order]
    counts = jnp.zeros((N_EXPERTS,), jnp.int32).at[eid].add(1)
    padded = (counts + MOE_BLOCK - 1) // MOE_BLOCK * MOE_BLOCK
    pad_end = jnp.cumsum(padded)
    pad_start = pad_end - padded
    seg_start = jnp.cumsum(counts) - counts
    dest = pad_start[e_s] + jnp.arange(A, dtype=jnp.int32) - seg_start[e_s]
    n_blocks = -(-A // MOE_BLOCK) + N_EXPERTS
    P = n_blocks * MOE_BLOCK
    slot_tok = jnp.zeros((P,), jnp.int32).at[dest].set(tok_s)
    xb = h[slot_tok].reshape(n_blocks, MOE_BLOCK, D)
    blk_e = jnp.minimum(jnp.searchsorted(pad_end, jnp.arange(n_blocks, dtype=jnp.int32) * MOE_BLOCK, side="right"), N_EXPERTS - 1)
    w1, w3, w2 = lp["w1"], lp["w3"], lp["w2"]

    def expert_block(args):
        xi, e = args
        return (jax.nn.silu(xi @ w1[e]) * (xi @ w3[e])) @ w2[e]

    y = lax.map(expert_block, (xb, blk_e)).reshape(P, D)
    return jax.ops.segment_sum(y[dest] * w_s[:, None].astype(y.dtype), tok_s, num_segments=N)


def setup_inputs(seed: int = 0) -> dict:
    key = jax.random.key(seed)
    ks = jax.random.split(key, 24)
    f32 = jnp.float32
    nrm = lambda k, shape, s: jax.random.normal(k, shape, f32) * s
    gain = lambda k, shape: 1.0 + 0.05 * jax.random.normal(k, shape, f32)
    dt = jnp.exp(jax.random.uniform(ks[10], (DEPTH, 2, DN_HEADS), f32, math.log(1e-3), math.log(1e-1)))
    return {
        "x": nrm(ks[0], (BATCH, SEQ, D_MODEL), 1.0),
        "c": nrm(ks[1], (BATCH, D_MODEL), 1.0),
        "ctx": nrm(ks[2], (BATCH, CTX_LEN, D_MODEL), 1.0),
        "c_ctx": nrm(ks[3], (D_MODEL,), 1.0),
        "ada_w": nrm(ks[4], (DEPTH, D_MODEL, 6 * D_MODEL), 0.5 * D_MODEL ** -0.5),
        "ada_b": nrm(ks[5], (DEPTH, 6 * D_MODEL), 0.01),
        "norm_mix_w": gain(ks[6], (DEPTH, D_MODEL)),
        "norm_ffn_w": gain(ks[7], (DEPTH, D_MODEL)),
        "w_in": nrm(ks[8], (DEPTH, D_MODEL, PROJ_DIM), D_MODEL ** -0.5),
        "q_norm_w": gain(ks[9], (DEPTH, HEAD_DIM)),
        "k_norm_w": gain(ks[11], (DEPTH, HEAD_DIM)),
        "conv_w": nrm(ks[12], (DEPTH, CONV_K, 3 * DN_WIDTH), CONV_K ** -0.5),
        "dn_A_log": jnp.log(jax.random.uniform(ks[13], (DEPTH, 2, DN_HEADS), f32, 1.0, 16.0)),
        "dn_dt_bias": dt + jnp.log(-jnp.expm1(-dt)),
        "dn_norm_w": gain(ks[14], (DEPTH, DN_DIM)),
        "w_out": nrm(ks[15], (DEPTH, D_MODEL, D_MODEL), D_MODEL ** -0.5),
        "rg_w": nrm(ks[16], (DEPTH, D_MODEL, N_GROUPS), D_MODEL ** -0.5),
        "rg_b": nrm(ks[17], (DEPTH, N_GROUPS), 0.01),
        "re_w": nrm(ks[18], (DEPTH, D_MODEL, N_EXPERTS), D_MODEL ** -0.5),
        "re_b": nrm(ks[19], (DEPTH, N_EXPERTS), 0.01),
        "w1": nrm(ks[20], (DEPTH, N_EXPERTS, D_MODEL, EXPERT_FF), D_MODEL ** -0.5),
        "w3": nrm(ks[21], (DEPTH, N_EXPERTS, D_MODEL, EXPERT_FF), D_MODEL ** -0.5),
        "w2": nrm(ks[22], (DEPTH, N_EXPERTS, EXPERT_FF, D_MODEL), EXPERT_FF ** -0.5),
        "final_norm_w": gain(ks[23], (D_MODEL,)),
    }


def reference(x, c, ctx, c_ctx, ada_w, ada_b, norm_mix_w, norm_ffn_w, w_in, q_norm_w, k_norm_w,
              conv_w, dn_A_log, dn_dt_bias, dn_norm_w, w_out, rg_w, rg_b, re_w, re_b, w1, w3, w2,
              final_norm_w):
    B, S, D = x.shape
    L = ctx.shape[1]
    rope = axial_rope_tables(S)
    sc, scc = jax.nn.silu(c), jax.nn.silu(c_ctx)
    for i in range(DEPTH):
        lp = dict(w_in=w_in[i], q_norm_w=q_norm_w[i], k_norm_w=k_norm_w[i], conv_w=conv_w[i],
                  dn_A_log=dn_A_log[i], dn_dt_bias=dn_dt_bias[i], dn_norm_w=dn_norm_w[i], w_out=w_out[i],
                  rg_w=rg_w[i], rg_b=rg_b[i], re_w=re_w[i], re_b=re_b[i], w1=w1[i], w3=w3[i], w2=w2[i])
        last = i == DEPTH - 1
        m_lat = jnp.split((sc @ ada_w[i] + ada_b[i])[:, None, :], 6, axis=-1)
        m_ctx = jnp.split(scc @ ada_w[i] + ada_b[i], 6, axis=-1)
        h_lat = modulate(rms_norm(x, norm_mix_w[i]), m_lat[0], m_lat[1])
        h_ctx = modulate(rms_norm(ctx, norm_mix_w[i]), m_ctx[0], m_ctx[1])
        y_lat, y_ctx = hybrid_mixer(h_lat, h_ctx, lp, rope, not last)
        x = x + m_lat[2].astype(x.dtype) * y_lat
        h_lat = modulate(rms_norm(x, norm_ffn_w[i]), m_lat[3], m_lat[4])
        if last:
            x = x + m_lat[5].astype(x.dtype) * hier_moe(h_lat.reshape(-1, D), lp).reshape(B, S, D)
        else:
            ctx = ctx + m_ctx[2].astype(ctx.dtype) * y_ctx
            h_ctx = modulate(rms_norm(ctx, norm_ffn_w[i]), m_ctx[3], m_ctx[4])
            f = hier_moe(jnp.concatenate([h_ctx.reshape(-1, D), h_lat.reshape(-1, D)], axis=0), lp)
            ctx = ctx + m_ctx[5].astype(ctx.dtype) * f[:B * L].reshape(B, L, D)
            x = x + m_lat[5].astype(x.dtype) * f[B * L:].reshape(B, S, D)
    return rms_norm(x, final_norm_w)
```

```python
import functools
import math

import jax
import jax.numpy as jnp
from jax import lax
from jax.experimental import pallas as pl
from jax.experimental.pallas import tpu as pltpu

F32 = jnp.float32
BF16 = jnp.bfloat16
HIGHEST = lax.Precision.HIGHEST

GRID_W = 64
HEAD_DIM = 64
ATTN_HEADS = 8
KV_HEADS = 2
GQA_GROUP = ATTN_HEADS // KV_HEADS
ATTN_WIDTH = ATTN_HEADS * HEAD_DIM
KV_WIDTH = KV_HEADS * HEAD_DIM
ATTN_SCALE = HEAD_DIM ** -0.5
ROPE_THETA = 10000.0
ROPE_NF = HEAD_DIM // 4
DN_HEADS = 8
DN_DIM = 64
DN_WIDTH = DN_HEADS * DN_DIM
DN_SCALE = DN_DIM ** -0.5
DN_GROUP = 4
CONV_K = 5
CHUNK = 64
N_GROUPS = 4
EXPERTS_PER_GROUP = 8
N_EXPERTS = N_GROUPS * EXPERTS_PER_GROUP
EPS = 1e-6
QK_WIDTH = ATTN_WIDTH + KV_WIDTH
TOKEN_TILE = 256
HALO = 16
ROUTE_LANES = 128
VMEM_LIMIT = 56 * 1024 * 1024


def _cparams(*sem):
    return pltpu.CompilerParams(dimension_semantics=sem, vmem_limit_bytes=VMEM_LIMIT)


def _silu(x):
    return x * jax.nn.sigmoid(x)


def _block_ones(n, blk):
    i = lax.broadcasted_iota(jnp.int32, (n, n), 0) // blk
    j = lax.broadcasted_iota(jnp.int32, (n, n), 1) // blk
    return (i == j).astype(BF16)


def _group_mean_sq(x, ones_bd, width):
    return jnp.dot((x * x).astype(BF16), ones_bd, preferred_element_type=F32) * (1.0 / width)


def _ada_kernel(cs_ref, w_ref, b_ref, o_ref):
    o_ref[0] = jnp.dot(_silu(cs_ref[...]), w_ref[0], preferred_element_type=F32,
                       precision=HIGHEST) + b_ref[0]


def _ada_mods(cs, ada_w, ada_b):
    depth, d, n6 = ada_w.shape
    rows = cs.shape[0]
    tn = 1536
    return pl.pallas_call(
        _ada_kernel,
        grid=(depth, n6 // tn),
        in_specs=[pl.BlockSpec((rows, d), lambda l, j: (0, 0)),
                  pl.BlockSpec((1, d, tn), lambda l, j: (l, 0, j)),
                  pl.BlockSpec((1, 1, tn), lambda l, j: (l, 0, j))],
        out_specs=pl.BlockSpec((1, rows, tn), lambda l, j: (l, 0, j)),
        out_shape=jax.ShapeDtypeStruct((depth, rows, n6), F32),
        compiler_params=_cparams("parallel", "parallel"),
        name="ada_mods",
    )(cs, ada_w, ada_b.reshape(depth, 1, n6))


def _inproj_kernel(*refs, has_prev):
    if has_prev:
        (x_ref, f_ref, modp_ref, mod_ref, nw_ref, win_ref, qkw_ref, cos_ref, sin_ref, bd_ref, dnp_ref,
         q_ref, k_ref, v_ref, dqkv_ref, gate_ref, bg_ref, xo_ref) = refs
        x = x_ref[0] + modp_ref[0, 5:6, :] * f_ref[0].astype(F32)
        xo_ref[0] = x
    else:
        (x_ref, mod_ref, nw_ref, win_ref, qkw_ref, cos_ref, sin_ref, bd_ref, dnp_ref,
         q_ref, k_ref, v_ref, dqkv_ref, gate_ref, bg_ref) = refs
        x = x_ref[0]
    h = x * lax.rsqrt(jnp.mean(x * x, axis=-1, keepdims=True) + EPS) * nw_ref[...]
    h = h * (1.0 + mod_ref[0, 1:2, :]) + mod_ref[0, 0:1, :]
    acc = jnp.dot(h.astype(BF16), win_ref[...], preferred_element_type=F32)
    qk = acc[:, :QK_WIDTH]
    qn = qk * lax.rsqrt(_group_mean_sq(qk, bd_ref[...], HEAD_DIM) + EPS) * qkw_ref[...]
    lane = lax.broadcasted_iota(jnp.int32, qn.shape, 1)
    partner = jnp.where(lane % (2 * ROPE_NF) < ROPE_NF,
                        pltpu.roll(qn, QK_WIDTH - ROPE_NF, 1), pltpu.roll(qn, ROPE_NF, 1))
    qr = qn * cos_ref[...] + partner * sin_ref[...]
    q_ref[0] = (qr[:, :ATTN_WIDTH] * ATTN_SCALE).astype(BF16)
    k_ref[0] = qr[:, ATTN_WIDTH:].astype(BF16)
    c0 = QK_WIDTH
    v_ref[0] = acc[:, c0:c0 + KV_WIDTH].astype(BF16)
    c0 += KV_WIDTH
    dqkv_ref[0] = acc[:, c0:c0 + 3 * DN_WIDTH].astype(BF16)
    c0 += 3 * DN_WIDTH
    gate_ref[0] = acc[:, c0:c0 + DN_WIDTH].astype(BF16)
    c0 += DN_WIDTH
    z = acc[:, c0:c0 + 4 * DN_HEADS]
    zb = z + dnp_ref[1:2, :]
    softplus = jnp.maximum(zb, 0.0) + jnp.log1p(jnp.exp(-jnp.abs(zb)))
    lane_z = lax.broadcasted_iota(jnp.int32, z.shape, 1)
    bg_ref[0] = jnp.where(lane_z < 2 * DN_HEADS, jax.nn.sigmoid(z), -jnp.exp(dnp_ref[0:1, :]) * softplus)


def _inproj(xs, f_prev, mods_prev, mods, nw, win, qkw, cos_t, sin_t, bd, dnp, *, n_ctx_tiles):
    b, t, d = xs.shape
    tm = TOKEN_TILE
    nt = t // tm
    proj = win.shape[1]
    has_prev = f_prev is not None
    tok = lambda i, j: (j, i, 0)
    modi = lambda i, j: (jnp.where(i < n_ctx_tiles, b, j), 0, 0)
    const = lambda i, j: (0, 0)
    in_specs = [pl.BlockSpec((1, tm, d), tok)]
    args = [xs]
    if has_prev:
        in_specs += [pl.BlockSpec((1, tm, d), tok), pl.BlockSpec((1, 6, d), modi)]
        args += [f_prev, mods_prev]
    in_specs += [pl.BlockSpec((1, 6, d), modi), pl.BlockSpec((1, d), const),
                 pl.BlockSpec((d, proj), const), pl.BlockSpec((1, QK_WIDTH), const),
                 pl.BlockSpec((tm, QK_WIDTH), lambda i, j: (i, 0)),
                 pl.BlockSpec((tm, QK_WIDTH), lambda i, j: (i, 0)),
                 pl.BlockSpec((QK_WIDTH, QK_WIDTH), const), pl.BlockSpec((2, 4 * DN_HEADS), const)]
    args += [mods, nw, win, qkw, cos_t, sin_t, bd, dnp]
    widths = [ATTN_WIDTH, KV_WIDTH, KV_WIDTH, 3 * DN_WIDTH, DN_WIDTH]
    out_specs = [pl.BlockSpec((1, tm, w), tok) for w in widths] + [pl.BlockSpec((1, tm, 4 * DN_HEADS), tok)]
    out_shape = [jax.ShapeDtypeStruct((b, t, w), BF16) for w in widths] + [
        jax.ShapeDtypeStruct((b, t, 4 * DN_HEADS), F32)]
    aliases = {}
    if has_prev:
        out_specs.append(pl.BlockSpec((1, tm, d), tok))
        out_shape.append(jax.ShapeDtypeStruct((b, t, d), F32))
        aliases = {0: len(out_shape) - 1}
    return pl.pallas_call(
        functools.partial(_inproj_kernel, has_prev=has_prev),
        grid=(nt, b), in_specs=in_specs, out_specs=out_specs, out_shape=out_shape,
        input_output_aliases=aliases,
        compiler_params=_cparams("parallel", "parallel"),
        name="inproj",
    )(*args)


def _conv_kernel(x_ref, xb_ref, xa_ref, w_ref, bd_ref, dq_ref, dk_ref, dv_ref, *, n_ctx_tiles, n_tiles):
    i = pl.program_id(1)
    tm = x_ref.shape[1]
    first = jnp.logical_or(i == 0, i == n_ctx_tiles)
    last = jnp.logical_or(i == n_ctx_tiles - 1, i == n_tiles - 1)
    before = xb_ref[0].astype(F32)[HALO - 8:] * jnp.where(first, 0.0, 1.0)
    after = xa_ref[0].astype(F32)[:8] * jnp.where(last, 0.0, 1.0)
    ext = jnp.concatenate([before, x_ref[0].astype(F32), after], axis=0)
    n_ext = tm + 16
    y = None
    for j in range(CONV_K):
        shift = (CONV_K // 2 - j) % n_ext
        tap = ext if shift == 0 else pltpu.roll(ext, shift, 0)
        term = tap[8:8 + tm] * w_ref[j:j + 1, :]
        y = term if y is None else y + term
    y = _silu(y)
    q, k, v = y[:, :DN_WIDTH], y[:, DN_WIDTH:2 * DN_WIDTH], y[:, 2 * DN_WIDTH:]
    bd = bd_ref[...]
    dq_ref[0] = (q * lax.rsqrt(_group_mean_sq(q, bd, 1.0) + EPS) * DN_SCALE).astype(BF16)
    dk_ref[0] = (k * lax.rsqrt(_group_mean_sq(k, bd, 1.0) + EPS)).astype(BF16)
    dv_ref[0] = v.astype(BF16)


def _dn_conv(dqkv, conv_w, bd, *, n_ctx_tiles):
    b, t, c = dqkv.shape
    tm = TOKEN_TILE
    nt = t // tm
    r = tm // HALO
    n_halo = t // HALO
    kern = functools.partial(_conv_kernel, n_ctx_tiles=n_ctx_tiles, n_tiles=nt)
    out = jax.ShapeDtypeStruct((b, t, DN_WIDTH), BF16)
    return pl.pallas_call(
        kern, grid=(b, nt),
        in_specs=[pl.BlockSpec((1, tm, c), lambda bi, i: (bi, i, 0)),
                  pl.BlockSpec((1, HALO, c), lambda bi, i: (bi, jnp.maximum(i * r - 1, 0), 0)),
                  pl.BlockSpec((1, HALO, c), lambda bi, i: (bi, jnp.minimum((i + 1) * r, n_halo - 1), 0)),
                  pl.BlockSpec((CONV_K, c), lambda bi, i: (0, 0)),
                  pl.BlockSpec((DN_WIDTH, DN_WIDTH), lambda bi, i: (0, 0))],
        out_specs=[pl.BlockSpec((1, tm, DN_WIDTH), lambda bi, i: (bi, i, 0))] * 3,
        out_shape=[out, out, out],
        compiler_params=_cparams("parallel", "parallel"),
        name="dn_conv",
    )(dqkv, dqkv, dqkv, conv_w, bd)


def _attn_kernel(q_ref, k_ref, v_ref, o_ref, m_sc, l_sc, acc_sc, *, tk, n_ctx_q, n_ctx_k, n_all_k):
    tq = q_ref.shape[1]
    qt = pl.program_id(1)
    n_kv = jnp.where(qt < n_ctx_q, n_ctx_k, n_all_k)
    q = q_ref[0]
    for g in range(KV_HEADS):
        lo = g * HEAD_DIM
        qg = jnp.concatenate(
            [q[:, (g * GQA_GROUP + j) * HEAD_DIM:(g * GQA_GROUP + j + 1) * HEAD_DIM] for j in range(GQA_GROUP)],
            axis=0)
        m_sc[...] = jnp.full(m_sc.shape, -jnp.inf, F32)
        l_sc[...] = jnp.zeros(l_sc.shape, F32)
        acc_sc[...] = jnp.zeros(acc_sc.shape, F32)

        def body(i, carry):
            rows = pl.ds(pl.multiple_of(i * tk, tk), tk)
            kt = k_ref[0, rows, lo:lo + HEAD_DIM]
            vt = v_ref[0, rows, lo:lo + HEAD_DIM]
            s = lax.dot_general(qg, kt, (((1,), (1,)), ((), ())), preferred_element_type=F32)
            m_prev = m_sc[...]
            m_new = jnp.maximum(m_prev, jnp.max(s, axis=-1, keepdims=True))
            p = jnp.exp(s - m_new)
            alpha = jnp.exp(m_prev - m_new)
            l_sc[...] = alpha * l_sc[...] + jnp.sum(p, axis=-1, keepdims=True)
            acc_sc[...] = alpha * acc_sc[...] + jnp.dot(p.astype(BF16), vt, preferred_element_type=F32)
            m_sc[...] = m_new
            return carry

        lax.fori_loop(0, n_kv, body, 0)
        og = acc_sc[...] / l_sc[...]
        for j in range(GQA_GROUP):
            h0 = (g * GQA_GROUP + j) * HEAD_DIM
            o_ref[0, :, h0:h0 + HEAD_DIM] = og[j * tq:(j + 1) * tq].astype(BF16)


def _attention(q, k, v, *, n_ctx):
    b, t, _ = q.shape
    tq = tk = TOKEN_TILE
    kern = functools.partial(_attn_kernel, tk=tk, n_ctx_q=n_ctx // tq, n_ctx_k=n_ctx // tk, n_all_k=t // tk)
    rows = GQA_GROUP * tq
    return pl.pallas_call(
        kern, grid=(b, t // tq),
        in_specs=[pl.BlockSpec((1, tq, ATTN_WIDTH), lambda bi, i: (bi, i, 0)),
                  pl.BlockSpec((1, t, KV_WIDTH), lambda bi, i: (bi, 0, 0)),
                  pl.BlockSpec((1, t, KV_WIDTH), lambda bi, i: (bi, 0, 0))],
        out_specs=pl.BlockSpec((1, tq, ATTN_WIDTH), lambda bi, i: (bi, i, 0)),
        out_shape=jax.ShapeDtypeStruct((b, t, ATTN_WIDTH), BF16),
        scratch_shapes=[pltpu.VMEM((rows, 1), F32), pltpu.VMEM((rows, 1), F32),
                        pltpu.VMEM((rows, HEAD_DIM), F32)],
        compiler_params=_cparams("parallel", "parallel"),
        name="gqa_attention",
    )(q, k, v)


def _dn_kernel(dq_ref, dk_ref, dv_ref, bg_ref, bgt_ref, o_ref, *, n_chunks, n_ctx_chunks):
    c = CHUNK
    ii = lax.broadcasted_iota(jnp.int32, (c, c), 0)
    jj = lax.broadcasted_iota(jnp.int32, (c, c), 1)
    eye = (ii == jj).astype(F32)
    nh = DN_GROUP
    for d in range(2):
        incl = (ii >= jj) if d == 0 else (ii <= jj)
        strict = (ii > jj) if d == 0 else (ii < jj)
        tri = incl.astype(F32)
        tri_t = ((ii <= jj) if d == 0 else (ii >= jj)).astype(F32)
        last_row = c - 1 if d == 0 else 0

        def step(s, state, d=d, incl=incl, strict=strict, tri=tri, tri_t=tri_t, last_row=last_row):
            if d == 0:
                ci = s
            else:
                ci = jnp.where(s < n_ctx_chunks, n_ctx_chunks - 1 - s, n_chunks - 1 + n_ctx_chunks - s)
            rows = pl.ds(pl.multiple_of(ci * c, c), c)
            bgc = bg_ref[0, 0, rows, :]
            gcum = jnp.dot(tri, bgc, preferred_element_type=F32, precision=HIGHEST)
            gcum_t = jnp.dot(bgt_ref[0, 0, ci], tri_t, preferred_element_type=F32, precision=HIGHEST)
            new_state = []
            for h in range(nh):
                cb = d * nh + h
                cg = 2 * nh + d * nh + h
                hs = slice(h * DN_DIM, (h + 1) * DN_DIM)
                g_col = gcum[:, cg:cg + 1]
                g_row = gcum_t[cg:cg + 1, :]
                g_last = gcum[last_row:last_row + 1, cg:cg + 1]
                beta = bgc[:, cb:cb + 1]
                q = dq_ref[0, rows, hs].astype(F32)
                k = dk_ref[0, rows, hs].astype(F32)
                v = dv_ref[0, rows, hs].astype(F32)
                decay = jnp.where(incl, jnp.exp(jnp.where(incl, g_col - g_row, 0.0)), 0.0)
                kb = k * beta
                nt_dims = (((1,), (1,)), ((), ()))
                kk = lax.dot_general(kb.astype(BF16), k.astype(BF16), nt_dims, preferred_element_type=F32)
                lower = jnp.where(strict, kk * decay, 0.0)
                p = -lower
                tmat = eye + p
                for _ in range(int(math.log2(c)) - 1):
                    p = jnp.dot(p, p, preferred_element_type=F32, precision=HIGHEST)
                    tmat = tmat + jnp.dot(tmat, p, preferred_element_type=F32, precision=HIGHEST)
                e_g = jnp.exp(g_col)
                tb = tmat.astype(BF16)
                u = jnp.dot(tb, (v * beta).astype(BF16), preferred_element_type=F32)
                w = jnp.dot(tb, (kb * e_g).astype(BF16), preferred_element_type=F32)
                qk = lax.dot_general(q.astype(BF16), k.astype(BF16), nt_dims, preferred_element_type=F32)
                intra = jnp.where(incl, qk * decay, 0.0)
                q_dec = q * e_g
                k_dec = k * jnp.exp(g_last - g_col)
                st = state[h]
                st_b = st.astype(BF16)
                v_new = u - jnp.dot(w.astype(BF16), st_b, preferred_element_type=F32)
                o = (jnp.dot(q_dec.astype(BF16), st_b, preferred_element_type=F32)
                     + jnp.dot(intra.astype(BF16), v_new.astype(BF16), preferred_element_type=F32))
                upd = lax.dot_general(k_dec.astype(BF16), v_new.astype(BF16), (((0,), (0,)), ((), ())),
                                      preferred_element_type=F32)
                new_state.append(st * jnp.exp(g_last) + upd)
                if d == 0:
                    o_ref[0, rows, hs] = o
                else:
                    o_ref[0, rows, hs] += o
            return tuple(new_state)

        zero = tuple(jnp.zeros((DN_DIM, DN_DIM), F32) for _ in range(nh))
        lax.fori_loop(0, n_chunks, step, zero)


def _deltanet(dq, dk, dv, bg, *, n_ctx):
    b, t, _ = dq.shape
    n_chunks = t // CHUNK
    ng = DN_HEADS // DN_GROUP
    bgh = bg.reshape(b, t, 2, 2, ng, DN_GROUP).transpose(0, 4, 1, 2, 3, 5).reshape(b, ng, t, 4 * DN_GROUP)
    bgt = bgh.reshape(b, ng, n_chunks, CHUNK, 4 * DN_GROUP).transpose(0, 1, 2, 4, 3)
    lanes = DN_GROUP * DN_DIM
    seq = pl.BlockSpec((1, t, lanes), lambda bi, gi: (bi, 0, gi))
    kern = functools.partial(_dn_kernel, n_chunks=n_chunks, n_ctx_chunks=n_ctx // CHUNK)
    return pl.pallas_call(
        kern, grid=(b, ng),
        in_specs=[seq, seq, seq,
                  pl.BlockSpec((1, 1, t, 4 * DN_GROUP), lambda bi, gi: (bi, gi, 0, 0)),
                  pl.BlockSpec((1, 1, n_chunks, 4 * DN_GROUP, CHUNK), lambda bi, gi: (bi, gi, 0, 0, 0))],
        out_specs=seq,
        out_shape=jax.ShapeDtypeStruct((b, t, DN_WIDTH), F32),
        compiler_params=_cparams("parallel", "parallel"),
        name="gated_deltanet",
    )(dq, dk, dv, bgh, bgt)


def _outproj_kernel(a_ref, d_ref, gate_ref, x_ref, mod_ref, dnw_ref, bd_ref, woa_ref, wod_ref, nfw_ref,
                    wr_ref, br_ref, xo_ref, hf_ref, route_ref):
    dd = d_ref[0]
    gate = gate_ref[0].astype(F32)
    dn = dd * lax.rsqrt(_group_mean_sq(dd, bd_ref[...], DN_DIM) + EPS) * dnw_ref[...] * _silu(gate)
    y = (jnp.dot(a_ref[0], woa_ref[...], preferred_element_type=F32)
         + jnp.dot(dn.astype(BF16), wod_ref[...], preferred_element_type=F32))
    x = x_ref[0] + mod_ref[0, 2:3, :] * y
    xo_ref[0] = x
    h = x * lax.rsqrt(jnp.mean(x * x, axis=-1, keepdims=True) + EPS) * nfw_ref[...]
    h = h * (1.0 + mod_ref[0, 4:5, :]) + mod_ref[0, 3:4, :]
    hb = h.astype(BF16)
    hf_ref[0] = hb
    logits = jnp.dot(hb, wr_ref[...], preferred_element_type=F32) + br_ref[...]
    lane = lax.broadcasted_iota(jnp.int32, logits.shape, 1)
    neg = jnp.float32(-jnp.inf)
    big = jnp.int32(ROUTE_LANES)

    def first_argmax(vals, vmax):
        return jnp.min(jnp.where(vals == vmax, lane, big), axis=-1, keepdims=True)

    is_g = jnp.logical_and(lane >= N_EXPERTS, lane < N_EXPERTS + N_GROUPS)
    gl = jnp.where(is_g, logits, neg)
    g_max = jnp.max(gl, axis=-1, keepdims=True)
    g_sel = first_argmax(gl, g_max) - N_EXPERTS
    p_g = 1.0 / jnp.sum(jnp.exp(gl - g_max), axis=-1, keepdims=True)
    e_lo = g_sel * EXPERTS_PER_GROUP
    in_grp = jnp.logical_and(lane >= e_lo, lane < e_lo + EXPERTS_PER_GROUP)
    el = jnp.where(in_grp, logits, neg)
    e_max = jnp.max(el, axis=-1, keepdims=True)
    i1 = first_argmax(el, e_max)
    el2 = jnp.where(lane == i1, neg, el)
    e_max2 = jnp.max(el2, axis=-1, keepdims=True)
    i2 = first_argmax(el2, e_max2)
    p2 = jnp.exp(e_max2 - e_max)
    w1 = p_g / (1.0 + p2)
    w2 = p_g * p2 / (1.0 + p2)
    route_ref[0] = jnp.where(lane == i1, w1, jnp.where(lane == i2, w2, 0.0))


def _outproj(a, dsum, gate, xs, mods, dnw, bd, woa, wod, nfw, wr, br, *, n_ctx_tiles):
    b, t, d = xs.shape
    tm = TOKEN_TILE
    tok = lambda i, j: (j, i, 0)
    const = lambda i, j: (0, 0)
    modi = lambda i, j: (jnp.where(i < n_ctx_tiles, b, j), 0, 0)
    return pl.pallas_call(
        _outproj_kernel, grid=(t // tm, b),
        in_specs=[pl.BlockSpec((1, tm, ATTN_WIDTH), tok), pl.BlockSpec((1, tm, DN_WIDTH), tok),
                  pl.BlockSpec((1, tm, DN_WIDTH), tok), pl.BlockSpec((1, tm, d), tok),
                  pl.BlockSpec((1, 6, d), modi), pl.BlockSpec((1, DN_WIDTH), const),
                  pl.BlockSpec((DN_WIDTH, DN_WIDTH), const), pl.BlockSpec((ATTN_WIDTH, d), const),
                  pl.BlockSpec((DN_WIDTH, d), const), pl.BlockSpec((1, d), const),
                  pl.BlockSpec((d, ROUTE_LANES), const), pl.BlockSpec((1, ROUTE_LANES), const)],
        out_specs=[pl.BlockSpec((1, tm, d), tok), pl.BlockSpec((1, tm, d), tok),
                   pl.BlockSpec((1, tm, ROUTE_LANES), tok)],
        out_shape=[jax.ShapeDtypeStruct((b, t, d), F32), jax.ShapeDtypeStruct((b, t, d), BF16),
                   jax.ShapeDtypeStruct((b, t, ROUTE_LANES), F32)],
        input_output_aliases={3: 0},
        compiler_params=_cparams("parallel", "parallel"),
        name="outproj_router",
    )(a, dsum, gate, xs, mods, dnw, bd, woa, wod, nfw, wr, br)


def _moe_kernel(h_ref, r_ref, w1_ref, w3_ref, w2_ref, f_ref, acc_sc):
    e = pl.program_id(1)

    @pl.when(e == 0)
    def _():
        acc_sc[...] = jnp.zeros(acc_sc.shape, F32)

    r = r_ref[...]
    lane = lax.broadcasted_iota(jnp.int32, r.shape, 1)
    wt = jnp.sum(jnp.where(lane == e, r, 0.0), axis=-1, keepdims=True)
    hb = h_ref[...]
    a = jnp.dot(hb, w1_ref[0], preferred_element_type=F32)
    g = jnp.dot(hb, w3_ref[0], preferred_element_type=F32)
    mid = (_silu(a) * g * wt).astype(BF16)
    acc_sc[...] += jnp.dot(mid, w2_ref[0], preferred_element_type=F32)

    @pl.when(e == pl.num_programs(1) - 1)
    def _():
        f_ref[...] = acc_sc[...].astype(f_ref.dtype)


def _moe(hf, route, w1, w3, w2):
    n, d = hf.shape
    ne, _, ff = w1.shape
    tm = next(c for c in (2048, 1024, 512, 256) if n % c == 0)
    return pl.pallas_call(
        _moe_kernel, grid=(n // tm, ne),
        in_specs=[pl.BlockSpec((tm, d), lambda i, e: (i, 0)),
                  pl.BlockSpec((tm, ROUTE_LANES), lambda i, e: (i, 0)),
                  pl.BlockSpec((1, d, ff), lambda i, e: (e, 0, 0)),
                  pl.BlockSpec((1, d, ff), lambda i, e: (e, 0, 0)),
                  pl.BlockSpec((1, ff, d), lambda i, e: (e, 0, 0))],
        out_specs=pl.BlockSpec((tm, d), lambda i, e: (i, 0)),
        out_shape=jax.ShapeDtypeStruct((n, d), BF16),
        scratch_shapes=[pltpu.VMEM((tm, d), F32)],
        compiler_params=_cparams("parallel", "arbitrary"),
        name="moe_experts",
    )(hf, route, w1, w3, w2)


def _final_kernel(x_ref, f_ref, mod_ref, w_ref, o_ref):
    x = x_ref[0] + mod_ref[0, 5:6, :] * f_ref[0].astype(F32)
    o_ref[0] = x * lax.rsqrt(jnp.mean(x * x, axis=-1, keepdims=True) + EPS) * w_ref[...]


def _final_norm(xs, f_prev, mods, w, *, n_ctx_tiles):
    b, t, d = xs.shape
    tm = TOKEN_TILE
    n_lat = t // tm - n_ctx_tiles
    lat = lambda i, j: (j, i + n_ctx_tiles, 0)
    return pl.pallas_call(
        _final_kernel, grid=(n_lat, b),
        in_specs=[pl.BlockSpec((1, tm, d), lat), pl.BlockSpec((1, tm, d), lat),
                  pl.BlockSpec((1, 6, d), lambda i, j: (j, 0, 0)), pl.BlockSpec((1, d), lambda i, j: (0, 0))],
        out_specs=pl.BlockSpec((1, tm, d), lambda i, j: (j, i, 0)),
        out_shape=jax.ShapeDtypeStruct((b, n_lat * tm, d), F32),
        compiler_params=_cparams("parallel", "parallel"),
        name="final_norm",
    )(xs, f_prev, mods, w)


def _rope_tables(n_ctx, n_lat):
    pos = jnp.arange(n_lat, dtype=jnp.int32)
    inv = ROPE_THETA ** (-jnp.arange(ROPE_NF, dtype=F32) / ROPE_NF)
    ang_r = (pos // GRID_W).astype(F32)[:, None] * inv
    ang_c = (pos % GRID_W).astype(F32)[:, None] * inv
    cos = jnp.concatenate([jnp.cos(ang_r)] * 2 + [jnp.cos(ang_c)] * 2, axis=-1)
    sin = jnp.concatenate([-jnp.sin(ang_r), jnp.sin(ang_r), -jnp.sin(ang_c), jnp.sin(ang_c)], axis=-1)
    cos = jnp.concatenate([jnp.ones((n_ctx, HEAD_DIM), F32), cos], axis=0)
    sin = jnp.concatenate([jnp.zeros((n_ctx, HEAD_DIM), F32), sin], axis=0)
    reps = QK_WIDTH // HEAD_DIM
    return jnp.tile(cos, (1, reps)), jnp.tile(sin, (1, reps))


def kernel(x, c, ctx, c_ctx, ada_w, ada_b, norm_mix_w, norm_ffn_w, w_in, q_norm_w, k_norm_w, conv_w,
           dn_A_log, dn_dt_bias, dn_norm_w, w_out, rg_w, rg_b, re_w, re_b, w1, w3, w2, final_norm_w):
    b, s, d = x.shape
    n_ctx = ctx.shape[1]
    depth = w_in.shape[0]
    t = n_ctx + s
    assert n_ctx % TOKEN_TILE == 0 and s % TOKEN_TILE == 0 and s % GRID_W == 0
    n_ctx_tiles = n_ctx // TOKEN_TILE

    xs = jnp.concatenate([ctx, x], axis=1)
    mod_rows = -(-(b + 1) // 8) * 8
    cs = jnp.zeros((mod_rows, d), F32).at[:b].set(c).at[b].set(c_ctx)
    mods = _ada_mods(cs, ada_w, ada_b).reshape(depth, mod_rows, 6, d)

    cos_t, sin_t = _rope_tables(n_ctx, s)
    bd_qk = _block_ones(QK_WIDTH, HEAD_DIM)
    bd_dn = _block_ones(DN_WIDTH, DN_DIM)
    zeros16 = jnp.zeros((2 * DN_HEADS,), F32)

    f_prev = None
    for l in range(depth):
        qkw = jnp.concatenate([jnp.tile(q_norm_w[l], ATTN_HEADS), jnp.tile(k_norm_w[l], KV_HEADS)])[None]
        dnp = jnp.stack([jnp.concatenate([zeros16, dn_A_log[l].reshape(-1)]),
                         jnp.concatenate([zeros16, dn_dt_bias[l].reshape(-1)])])
        outs = _inproj(xs, f_prev, mods[l - 1] if l else None, mods[l], norm_mix_w[l][None],
                       w_in[l].astype(BF16), qkw, cos_t, sin_t, bd_qk, dnp, n_ctx_tiles=n_ctx_tiles)
        q, k, v, dqkv, gate, bg = outs[:6]
        if l:
            xs = outs[6]
        dq, dk, dv = _dn_conv(dqkv, conv_w[l], bd_dn, n_ctx_tiles=n_ctx_tiles)
        a = _attention(q, k, v, n_ctx=n_ctx)
        dsum = _deltanet(dq, dk, dv, bg, n_ctx=n_ctx)
        wo = w_out[l].astype(BF16)
        wr = jnp.zeros((d, ROUTE_LANES), F32).at[:, :N_EXPERTS].set(re_w[l]).at[
            :, N_EXPERTS:N_EXPERTS + N_GROUPS].set(rg_w[l]).astype(BF16)
        br = jnp.zeros((1, ROUTE_LANES), F32).at[0, :N_EXPERTS].set(re_b[l]).at[
            0, N_EXPERTS:N_EXPERTS + N_GROUPS].set(rg_b[l])
        xs, hf, route = _outproj(a, dsum, gate, xs, mods[l], jnp.tile(dn_norm_w[l], DN_HEADS)[None], bd_dn,
                                 wo[:ATTN_WIDTH], wo[ATTN_WIDTH:], norm_ffn_w[l][None], wr, br,
                                 n_ctx_tiles=n_ctx_tiles)
        f_prev = _moe(hf.reshape(b * t, d), route.reshape(b * t, ROUTE_LANES),
                      w1[l].astype(BF16), w3[l].astype(BF16), w2[l].astype(BF16)).reshape(b, t, d)
    return _final_norm(xs, f_prev, mods[depth - 1], final_norm_w[None], n_ctx_tiles=n_ctx_tiles)
```

```python
import functools
import math

import jax
import jax.numpy as jnp
from jax import lax
from jax.experimental import pallas as pl
from jax.experimental.pallas import tpu as pltpu

F32 = jnp.float32
BF16 = jnp.bfloat16
HIGHEST = lax.Precision.HIGHEST

GRID_W = 64
HEAD_DIM = 64
ATTN_HEADS = 8
KV_HEADS = 2
GQA_GROUP = ATTN_HEADS // KV_HEADS
ATTN_WIDTH = ATTN_HEADS * HEAD_DIM
KV_WIDTH = KV_HEADS * HEAD_DIM
ATTN_SCALE = HEAD_DIM ** -0.5
LOG2_E = math.log2(math.e)
ROPE_THETA = 10000.0
ROPE_NF = HEAD_DIM // 4
DN_HEADS = 8
DN_DIM = 64
DN_WIDTH = DN_HEADS * DN_DIM
DN_SCALE = DN_DIM ** -0.5
DN_GROUP = 4
CONV_K = 5
CHUNK = 64
N_GROUPS = 4
EXPERTS_PER_GROUP = 8
N_EXPERTS = N_GROUPS * EXPERTS_PER_GROUP
EPS = 1e-6
QK_WIDTH = ATTN_WIDTH + KV_WIDTH
TOKEN_TILE = 256
HALO = 16
ROUTE_LANES = 128
VMEM_LIMIT = 56 * 1024 * 1024


def _cparams(*sem):
    return pltpu.CompilerParams(dimension_semantics=sem, vmem_limit_bytes=VMEM_LIMIT)


def _silu(x):
    return x * jax.nn.sigmoid(x)


def _block_ones(n, blk):
    i = lax.broadcasted_iota(jnp.int32, (n, n), 0) // blk
    j = lax.broadcasted_iota(jnp.int32, (n, n), 1) // blk
    return (i == j).astype(BF16)


def _group_mean_sq(x, ones_bd, width):
    return jnp.dot((x * x).astype(BF16), ones_bd, preferred_element_type=F32) * (1.0 / width)


def _ada_kernel(cs_ref, w_ref, b_ref, o_ref):
    o_ref[0] = jnp.dot(_silu(cs_ref[...]), w_ref[0], preferred_element_type=F32,
                       precision=HIGHEST) + b_ref[0]


def _ada_mods(cs, ada_w, ada_b):
    depth, d, n6 = ada_w.shape
    rows = cs.shape[0]
    tn = 1536
    return pl.pallas_call(
        _ada_kernel,
        grid=(depth, n6 // tn),
        in_specs=[pl.BlockSpec((rows, d), lambda l, j: (0, 0)),
                  pl.BlockSpec((1, d, tn), lambda l, j: (l, 0, j)),
                  pl.BlockSpec((1, 1, tn), lambda l, j: (l, 0, j))],
        out_specs=pl.BlockSpec((1, rows, tn), lambda l, j: (l, 0, j)),
        out_shape=jax.ShapeDtypeStruct((depth, rows, n6), F32),
        compiler_params=_cparams("parallel", "parallel"),
        name="ada_mods",
    )(cs, ada_w, ada_b.reshape(depth, 1, n6))


def _inproj_kernel(*refs, has_prev):
    if has_prev:
        (x_ref, f_ref, modp_ref, mod_ref, nw_ref, win_ref, qkw_ref, cos_ref, sin_ref, bd_ref, dnp_ref,
         q_ref, k_ref, vt_ref, dqkv_ref, gate_ref, bg_ref, xo_ref) = refs
        x = x_ref[0] + modp_ref[0, 5:6, :] * f_ref[0].astype(F32)
        xo_ref[0] = x
    else:
        (x_ref, mod_ref, nw_ref, win_ref, qkw_ref, cos_ref, sin_ref, bd_ref, dnp_ref,
         q_ref, k_ref, vt_ref, dqkv_ref, gate_ref, bg_ref) = refs
        x = x_ref[0]
    h = x * lax.rsqrt(jnp.mean(x * x, axis=-1, keepdims=True) + EPS) * nw_ref[...]
    h = h * (1.0 + mod_ref[0, 1:2, :]) + mod_ref[0, 0:1, :]
    acc = jnp.dot(h.astype(BF16), win_ref[...], preferred_element_type=F32)
    qk = acc[:, :QK_WIDTH]
    qn = qk * lax.rsqrt(_group_mean_sq(qk, bd_ref[...], HEAD_DIM) + EPS) * qkw_ref[...]
    lane = lax.broadcasted_iota(jnp.int32, qn.shape, 1)
    partner = jnp.where(lane % (2 * ROPE_NF) < ROPE_NF,
                        pltpu.roll(qn, QK_WIDTH - ROPE_NF, 1), pltpu.roll(qn, ROPE_NF, 1))
    qr = qn * cos_ref[...] + partner * sin_ref[...]
    q_ref[0] = (qr[:, :ATTN_WIDTH] * (ATTN_SCALE * LOG2_E)).astype(BF16)
    k_ref[0] = qr[:, ATTN_WIDTH:].astype(BF16)
    c0 = QK_WIDTH
    vt_ref[0] = acc[:, c0:c0 + KV_WIDTH].T.astype(BF16)
    c0 += KV_WIDTH
    dqkv_ref[0] = acc[:, c0:c0 + 3 * DN_WIDTH].astype(BF16)
    c0 += 3 * DN_WIDTH
    gate_ref[0] = acc[:, c0:c0 + DN_WIDTH].astype(BF16)
    c0 += DN_WIDTH
    z = acc[:, c0:c0 + 4 * DN_HEADS]
    zb = z + dnp_ref[1:2, :]
    softplus = jnp.maximum(zb, 0.0) + jnp.log1p(jnp.exp(-jnp.abs(zb)))
    lane_z = lax.broadcasted_iota(jnp.int32, z.shape, 1)
    bg_ref[0] = jnp.where(lane_z < 2 * DN_HEADS, jax.nn.sigmoid(z), -jnp.exp(dnp_ref[0:1, :]) * softplus)


def _inproj(xs, f_prev, mods_prev, mods, nw, win, qkw, cos_t, sin_t, bd, dnp, *, n_ctx_tiles):
    b, t, d = xs.shape
    tm = TOKEN_TILE
    nt = t // tm
    proj = win.shape[1]
    has_prev = f_prev is not None
    tok = lambda i, j: (j, i, 0)
    modi = lambda i, j: (jnp.where(i < n_ctx_tiles, b, j), 0, 0)
    const = lambda i, j: (0, 0)
    in_specs = [pl.BlockSpec((1, tm, d), tok)]
    args = [xs]
    if has_prev:
        in_specs += [pl.BlockSpec((1, tm, d), tok), pl.BlockSpec((1, 6, d), modi)]
        args += [f_prev, mods_prev]
    in_specs += [pl.BlockSpec((1, 6, d), modi), pl.BlockSpec((1, d), const),
                 pl.BlockSpec((d, proj), const), pl.BlockSpec((1, QK_WIDTH), const),
                 pl.BlockSpec((tm, QK_WIDTH), lambda i, j: (i, 0)),
                 pl.BlockSpec((tm, QK_WIDTH), lambda i, j: (i, 0)),
                 pl.BlockSpec((QK_WIDTH, QK_WIDTH), const), pl.BlockSpec((2, 4 * DN_HEADS), const)]
    args += [mods, nw, win, qkw, cos_t, sin_t, bd, dnp]
    seq_out = lambda w, dt: (pl.BlockSpec((1, tm, w), tok), jax.ShapeDtypeStruct((b, t, w), dt))
    outs = [seq_out(ATTN_WIDTH, BF16), seq_out(KV_WIDTH, BF16),
            (pl.BlockSpec((1, KV_WIDTH, tm), lambda i, j: (j, 0, i)), jax.ShapeDtypeStruct((b, KV_WIDTH, t), BF16)),
            seq_out(3 * DN_WIDTH, BF16), seq_out(DN_WIDTH, BF16), seq_out(4 * DN_HEADS, F32)]
    out_specs = [o[0] for o in outs]
    out_shape = [o[1] for o in outs]
    aliases = {}
    if has_prev:
        out_specs.append(pl.BlockSpec((1, tm, d), tok))
        out_shape.append(jax.ShapeDtypeStruct((b, t, d), F32))
        aliases = {0: len(out_shape) - 1}
    return pl.pallas_call(
        functools.partial(_inproj_kernel, has_prev=has_prev),
        grid=(nt, b), in_specs=in_specs, out_specs=out_specs, out_shape=out_shape,
        input_output_aliases=aliases,
        compiler_params=_cparams("parallel", "parallel"),
        name="inproj",
    )(*args)


def _conv_kernel(x_ref, xb_ref, xa_ref, w_ref, bd_ref, dq_ref, dk_ref, dv_ref, *, n_ctx_tiles, n_tiles):
    i = pl.program_id(1)
    tm = x_ref.shape[1]
    first = jnp.logical_or(i == 0, i == n_ctx_tiles)
    last = jnp.logical_or(i == n_ctx_tiles - 1, i == n_tiles - 1)
    before = xb_ref[0].astype(F32)[HALO - 8:] * jnp.where(first, 0.0, 1.0)
    after = xa_ref[0].astype(F32)[:8] * jnp.where(last, 0.0, 1.0)
    ext = jnp.concatenate([before, x_ref[0].astype(F32), after], axis=0)
    n_ext = tm + 16
    y = None
    for j in range(CONV_K):
        shift = (CONV_K // 2 - j) % n_ext
        tap = ext if shift == 0 else pltpu.roll(ext, shift, 0)
        term = tap[8:8 + tm] * w_ref[j:j + 1, :]
        y = term if y is None else y + term
    y = _silu(y)
    q, k, v = y[:, :DN_WIDTH], y[:, DN_WIDTH:2 * DN_WIDTH], y[:, 2 * DN_WIDTH:]
    bd = bd_ref[...]
    dq_ref[0] = (q * lax.rsqrt(_group_mean_sq(q, bd, 1.0) + EPS) * DN_SCALE).astype(BF16)
    dk_ref[0] = (k * lax.rsqrt(_group_mean_sq(k, bd, 1.0) + EPS)).astype(BF16)
    dv_ref[0] = v.astype(BF16)


def _dn_conv(dqkv, conv_w, bd, *, n_ctx_tiles):
    b, t, c = dqkv.shape
    tm = TOKEN_TILE
    nt = t // tm
    r = tm // HALO
    n_halo = t // HALO
    kern = functools.partial(_conv_kernel, n_ctx_tiles=n_ctx_tiles, n_tiles=nt)
    out = jax.ShapeDtypeStruct((b, t, DN_WIDTH), BF16)
    return pl.pallas_call(
        kern, grid=(b, nt),
        in_specs=[pl.BlockSpec((1, tm, c), lambda bi, i: (bi, i, 0)),
                  pl.BlockSpec((1, HALO, c), lambda bi, i: (bi, jnp.maximum(i * r - 1, 0), 0)),
                  pl.BlockSpec((1, HALO, c), lambda bi, i: (bi, jnp.minimum((i + 1) * r, n_halo - 1), 0)),
                  pl.BlockSpec((CONV_K, c), lambda bi, i: (0, 0)),
                  pl.BlockSpec((DN_WIDTH, DN_WIDTH), lambda bi, i: (0, 0))],
        out_specs=[pl.BlockSpec((1, tm, DN_WIDTH), lambda bi, i: (bi, i, 0))] * 3,
        out_shape=[out, out, out],
        compiler_params=_cparams("parallel", "parallel"),
        name="dn_conv",
    )(dqkv, dqkv, dqkv, conv_w, bd)


def _attn_kernel(q_ref, k_ref, vt_ref, o_ref, m_sc, l_sc, acc_sc, *, tk, n_ctx_q, n_ctx_k, n_all_k):
    tq = q_ref.shape[1]
    qt = pl.program_id(1)
    n_kv = jnp.where(qt < n_ctx_q, n_ctx_k, n_all_k)
    q = q_ref[0]
    for g in range(KV_HEADS):
        lo = g * HEAD_DIM
        qg = jnp.concatenate(
            [q[:, (g * GQA_GROUP + j) * HEAD_DIM:(g * GQA_GROUP + j + 1) * HEAD_DIM] for j in range(GQA_GROUP)],
            axis=0)
        m_sc[...] = jnp.full(m_sc.shape, -jnp.inf, F32)
        l_sc[...] = jnp.zeros(l_sc.shape, F32)
        acc_sc[...] = jnp.zeros(acc_sc.shape, F32)

        def body(i, carry):
            rows = pl.ds(pl.multiple_of(i * tk, tk), tk)
            kt = k_ref[0, rows, lo:lo + HEAD_DIM]
            vt = vt_ref[0, lo:lo + HEAD_DIM, rows]
            s = lax.dot_general(kt, qg, (((1,), (1,)), ((), ())), preferred_element_type=F32)
            m_prev = m_sc[...]
            m_new = jnp.maximum(m_prev, jnp.max(s, axis=0, keepdims=True))
            p = jnp.exp2(s - m_new)
            alpha = jnp.exp2(m_prev - m_new)
            l_sc[...] = alpha * l_sc[...] + jnp.sum(p, axis=0, keepdims=True)
            acc_sc[...] = alpha * acc_sc[...] + jnp.dot(vt, p.astype(BF16), preferred_element_type=F32)
            m_sc[...] = m_new
            return carry

        lax.fori_loop(0, n_kv, body, 0)
        og = acc_sc[...] / l_sc[...]
        og = jnp.concatenate([og[:, j * tq:(j + 1) * tq] for j in range(GQA_GROUP)], axis=0)
        w0 = g * GQA_GROUP * HEAD_DIM
        o_ref[0, :, w0:w0 + GQA_GROUP * HEAD_DIM] = og.T.astype(BF16)


def _attention(q, k, vt, *, n_ctx):
    b, t, _ = q.shape
    tq = tk = TOKEN_TILE
    kern = functools.partial(_attn_kernel, tk=tk, n_ctx_q=n_ctx // tq, n_ctx_k=n_ctx // tk, n_all_k=t // tk)
    cols = GQA_GROUP * tq
    return pl.pallas_call(
        kern, grid=(b, t // tq),
        in_specs=[pl.BlockSpec((1, tq, ATTN_WIDTH), lambda bi, i: (bi, i, 0)),
                  pl.BlockSpec((1, t, KV_WIDTH), lambda bi, i: (bi, 0, 0)),
                  pl.BlockSpec((1, KV_WIDTH, t), lambda bi, i: (bi, 0, 0))],
        out_specs=pl.BlockSpec((1, tq, ATTN_WIDTH), lambda bi, i: (bi, i, 0)),
        out_shape=jax.ShapeDtypeStruct((b, t, ATTN_WIDTH), BF16),
        scratch_shapes=[pltpu.VMEM((1, cols), F32), pltpu.VMEM((1, cols), F32),
                        pltpu.VMEM((HEAD_DIM, cols), F32)],
        compiler_params=_cparams("parallel", "parallel"),
        name="gqa_attention",
    )(q, k, vt)


def _dn_kernel_old(dq_ref, dk_ref, dv_ref, bg_ref, bgt_ref, o_ref, *, n_chunks, n_ctx_chunks):
    c = CHUNK
    ii = lax.broadcasted_iota(jnp.int32, (c, c), 0)
    jj = lax.broadcasted_iota(jnp.int32, (c, c), 1)
    eye = (ii == jj).astype(F32)
    nh = DN_GROUP
    for d in range(2):
        incl = (ii >= jj) if d == 0 else (ii <= jj)
        strict = (ii > jj) if d == 0 else (ii < jj)
        tri = incl.astype(F32)
        tri_t = ((ii <= jj) if d == 0 else (ii >= jj)).astype(F32)
        last_row = c - 1 if d == 0 else 0

        def step(s, state, d=d, incl=incl, strict=strict, tri=tri, tri_t=tri_t, last_row=last_row):
            if d == 0:
                ci = s
            else:
                ci = jnp.where(s < n_ctx_chunks, n_ctx_chunks - 1 - s, n_chunks - 1 + n_ctx_chunks - s)
            rows = pl.ds(pl.multiple_of(ci * c, c), c)
            bgc = bg_ref[0, 0, rows, :]
            gcum = jnp.dot(tri, bgc, preferred_element_type=F32, precision=HIGHEST)
            gcum_t = jnp.dot(bgt_ref[0, 0, ci], tri_t, preferred_element_type=F32, precision=HIGHEST)
            new_state = []
            for h in range(nh):
                cb = d * nh + h
                cg = 2 * nh + d * nh + h
                hs = slice(h * DN_DIM, (h + 1) * DN_DIM)
                g_col = gcum[:, cg:cg + 1]
                g_row = gcum_t[cg:cg + 1, :]
                g_last = gcum[last_row:last_row + 1, cg:cg + 1]
                beta = bgc[:, cb:cb + 1]
                q = dq_ref[0, rows, hs].astype(F32)
                k = dk_ref[0, rows, hs].astype(F32)
                v = dv_ref[0, rows, hs].astype(F32)
                decay = jnp.where(incl, jnp.exp(jnp.where(incl, g_col - g_row, 0.0)), 0.0)
                kb = k * beta
                nt_dims = (((1,), (1,)), ((), ()))
                kk = lax.dot_general(kb.astype(BF16), k.astype(BF16), nt_dims, preferred_element_type=F32)
                lower = jnp.where(strict, kk * decay, 0.0)
                p = -lower
                tmat = eye + p
                for _ in range(int(math.log2(c)) - 1):
                    p = jnp.dot(p, p, preferred_element_type=F32, precision=HIGHEST)
                    tmat = tmat + jnp.dot(tmat, p, preferred_element_type=F32, precision=HIGHEST)
                e_g = jnp.exp(g_col)
                tb = tmat.astype(BF16)
                u = jnp.dot(tb, (v * beta).astype(BF16), preferred_element_type=F32)
                w = jnp.dot(tb, (kb * e_g).astype(BF16), preferred_element_type=F32)
                qk = lax.dot_general(q.astype(BF16), k.astype(BF16), nt_dims, preferred_element_type=F32)
                intra = jnp.where(incl, qk * decay, 0.0)
                q_dec = q * e_g
                k_dec = k * jnp.exp(g_last - g_col)
                st = state[h]
                st_b = st.astype(BF16)
                v_new = u - jnp.dot(w.astype(BF16), st_b, preferred_element_type=F32)
                o = (jnp.dot(q_dec.astype(BF16), st_b, preferred_element_type=F32)
                     + jnp.dot(intra.astype(BF16), v_new.astype(BF16), preferred_element_type=F32))
                upd = lax.dot_general(k_dec.astype(BF16), v_new.astype(BF16), (((0,), (0,)), ((), ())),
                                      preferred_element_type=F32)
                new_state.append(st * jnp.exp(g_last) + upd)
                if d == 0:
                    o_ref[0, rows, hs] = o
                else:
                    o_ref[0, rows, hs] += o
            return tuple(new_state)

        zero = tuple(jnp.zeros((DN_DIM, DN_DIM), F32) for _ in range(nh))
        lax.fori_loop(0, n_chunks, step, zero)


def _deltanet_old(dq, dk, dv, bg, *, n_ctx):
    b, t, _ = dq.shape
    n_chunks = t // CHUNK
    ng = DN_HEADS // DN_GROUP
    bgh = bg.reshape(b, t, 2, 2, ng, DN_GROUP).transpose(0, 4, 1, 2, 3, 5).reshape(b, ng, t, 4 * DN_GROUP)
    bgt = bgh.reshape(b, ng, n_chunks, CHUNK, 4 * DN_GROUP).transpose(0, 1, 2, 4, 3)
    lanes = DN_GROUP * DN_DIM
    seq = pl.BlockSpec((1, t, lanes), lambda bi, gi: (bi, 0, gi))
    kern = functools.partial(_dn_kernel_old, n_chunks=n_chunks, n_ctx_chunks=n_ctx // CHUNK)
    return pl.pallas_call(
        kern, grid=(b, ng),
        in_specs=[seq, seq, seq,
                  pl.BlockSpec((1, 1, t, 4 * DN_GROUP), lambda bi, gi: (bi, gi, 0, 0)),
                  pl.BlockSpec((1, 1, n_chunks, 4 * DN_GROUP, CHUNK), lambda bi, gi: (bi, gi, 0, 0, 0))],
        out_specs=seq,
        out_shape=jax.ShapeDtypeStruct((b, t, DN_WIDTH), F32),
        compiler_params=_cparams("parallel", "parallel"),
        name="gated_deltanet",
    )(dq, dk, dv, bgh, bgt)


DN_LANES = DN_GROUP * DN_DIM
DN_BLOCK = 4
SCALAR_LANES = 128


def _split3(x):
    x1 = x.astype(BF16)
    r1 = x - x1.astype(F32)
    x2 = r1.astype(BF16)
    x3 = (r1 - x2.astype(F32)).astype(BF16)
    return jnp.concatenate([x1, x2, x3], axis=1)


def _sum3(y):
    w = y.shape[1] // 3
    return y[:, :w] + y[:, w:2 * w] + y[:, 2 * w:]


def _dn_kernel(dq_ref, dk_ref, dv_ref, bg_ref, o_ref, mp_sc, n_sc, r_sc, gam_sc, *, n_chunks, n_ctx_chunks):
    c = CHUNK
    rows_b = DN_BLOCK * c
    ln = DN_LANES
    ri = lax.broadcasted_iota(jnp.int32, (rows_b, ln), 0)
    li = lax.broadcasted_iota(jnp.int32, (rows_b, ln), 1)
    same = (ri // c) == (li // c)
    i_in, j_in = ri % c, li % c
    eye_rc = i_in == j_in
    e_row = lax.broadcasted_iota(jnp.int32, (3 * SCALAR_LANES, 2 * ln), 0) % SCALAR_LANES
    e_blk = lax.broadcasted_iota(jnp.int32, (3 * SCALAR_LANES, 2 * ln), 1) // c
    zl = lax.broadcasted_iota(jnp.int32, (rows_b, SCALAR_LANES), 1)
    nt_dims = (((1,), (1,)), ((), ()))
    tn_dims = (((0,), (0,)), ((), ()))

    def bdiag(x):
        return jnp.where(same, jnp.concatenate([x] * DN_GROUP, axis=0), jnp.zeros((), x.dtype))

    def fold(x):
        x = jnp.where(same, x, 0.0)
        return x[0:c] + x[c:2 * c] + x[2 * c:3 * c] + x[3 * c:4 * c]

    for d in range(2):
        incl = (i_in >= j_in) if d == 0 else (i_in <= j_in)
        strict = (i_in > j_in) if d == 0 else (i_in < j_in)
        last = c - 1 if d == 0 else 0
        tri_bd = jnp.logical_and(same, incl).astype(BF16)
        sel_last = jnp.logical_and(same, j_in == last).astype(BF16)
        same_b = same.astype(BF16)
        col0 = d * 2 * DN_GROUP
        expand = jnp.logical_and(e_row == e_blk + col0, e_blk < 2 * DN_GROUP).astype(BF16)
        is_beta = jnp.logical_and(zl >= col0, zl < col0 + DN_GROUP)

        def prepare(blk, carry, d=d, incl=incl, strict=strict, tri_bd=tri_bd, sel_last=sel_last,
                    same_b=same_b, expand=expand, is_beta=is_beta):
            rs = pl.ds(pl.multiple_of(blk * rows_b, rows_b), rows_b)
            z = bg_ref[0, 0, rs, :]
            cum = _sum3(jnp.dot(tri_bd, _split3(z), preferred_element_type=F32))
            zz = jnp.where(is_beta, z, cum)
            ex = jnp.dot(_split3(zz), expand, preferred_element_type=F32)
            beta, gce = ex[:, :ln], ex[:, ln:]
            g_row = _sum3(jnp.dot(same_b, _split3(jnp.where(eye_rc, gce, 0.0)), preferred_element_type=F32))
            g_last = _sum3(jnp.dot(sel_last, _split3(gce), preferred_element_type=F32))
            decay = jnp.where(incl, jnp.exp(jnp.where(incl, gce - g_row, 0.0)), 0.0)
            e_g = jnp.exp(gce)
            kbf = dk_ref[0, rs, :]
            qbf = dq_ref[0, rs, :]
            k = kbf.astype(F32)
            kb = k * beta
            kbe = (kb * e_g).astype(BF16)
            kbb = kb.astype(BF16)
            vb = (dv_ref[0, rs, :].astype(F32) * beta).astype(BF16)
            q_dec = qbf.astype(F32) * e_g
            k_dec = (k * jnp.exp(g_last - gce)).astype(BF16)
            gamma = jnp.exp(g_last)
            eye_f = eye_rc.astype(F32)
            chunks = range(DN_BLOCK)
            sls = [slice(ch * c, (ch + 1) * c) for ch in chunks]
            mm = lambda a, bmat: jnp.dot(a.astype(BF16), bmat, preferred_element_type=F32)
            sc = [lax.dot_general(jnp.concatenate([kbb[sl], qbf[sl]], axis=0), bdiag(kbf[sl]), nt_dims,
                                  preferred_element_type=F32) for sl in sls]
            intra = [jnp.where(incl[sl], s[c:] * decay[sl], 0.0).astype(BF16) for s, sl in zip(sc, sls)]
            p = [jnp.where(strict[sl], -s[:c] * decay[sl], 0.0) for s, sl in zip(sc, sls)]
            tmat = [eye_f[sl] + pi for pi, sl in zip(p, sls)]
            p = [mm(pi, bdiag(pi.astype(BF16))) for pi in p]
            for _ in range(int(math.log2(c)) - 2):
                res = [mm(jnp.concatenate([ti, pi], axis=0), bdiag(pi.astype(BF16))) for ti, pi in zip(tmat, p)]
                tmat = [ti + ri[:c] for ti, ri in zip(tmat, res)]
                p = [ri[c:] for ri in res]
            tmat = [ti + mm(ti, bdiag(pi.astype(BF16))) for ti, pi in zip(tmat, p)]
            uw = [mm(ti, jnp.concatenate([bdiag(vb[sl]), bdiag(kbe[sl])], axis=1)).astype(BF16)
                  for ti, sl in zip(tmat, sls)]
            aw_au = [mm(ai, jnp.concatenate([bdiag(x[:, ln:]), bdiag(x[:, :ln])], axis=1))
                     for ai, x in zip(intra, uw)]
            mn = [lax.dot_general(k_dec[sl], jnp.concatenate([x[:, ln:], x[:, :ln]], axis=1), tn_dims,
                                  preferred_element_type=F32) for x, sl in zip(uw, sls)]
            for ch in chunks:
                ci = blk * DN_BLOCK + ch
                p_c = q_dec[sls[ch]] - aw_au[ch][:, :ln]
                mp_sc[ci] = jnp.concatenate([fold(mn[ch][:, :ln]), p_c], axis=0).astype(BF16)
                n_sc[ci] = fold(mn[ch][:, ln:])
                r_sc[ci] = aw_au[ch][:, ln:]
                gam_sc[ci] = gamma[ch * c:ch * c + 8]
            return carry

        lax.fori_loop(0, n_chunks // DN_BLOCK, prepare, 0)

        def scan(s, state, d=d):
            if d == 0:
                ci = s
            else:
                ci = jnp.where(s < n_ctx_chunks, n_ctx_chunks - 1 - s, n_chunks - 1 + n_ctx_chunks - s)
            res = jnp.dot(mp_sc[ci], bdiag(state.astype(BF16)), preferred_element_type=F32)
            rows = pl.ds(pl.multiple_of(ci * c, c), c)
            o = res[c:] + r_sc[ci]
            if d == 0:
                o_ref[0, rows, :] = o
            else:
                o_ref[0, rows, :] += o
            return gam_sc[ci][0:1] * state - res[:c] + n_sc[ci]

        lax.fori_loop(0, n_chunks, scan, jnp.zeros((c, ln), F32))


def _deltanet(dq, dk, dv, bg, *, n_ctx):
    b, t, _ = dq.shape
    n_chunks = t // CHUNK
    assert n_chunks % DN_BLOCK == 0 and DN_GROUP * CHUNK == DN_LANES
    ng = DN_HEADS // DN_GROUP
    bgh = bg.reshape(b, t, 2, 2, ng, DN_GROUP).transpose(0, 4, 1, 3, 2, 5).reshape(b, ng, t, 4 * DN_GROUP)
    bgh = jnp.pad(bgh, ((0, 0), (0, 0), (0, 0), (0, SCALAR_LANES - 4 * DN_GROUP)))
    seq = pl.BlockSpec((1, t, DN_LANES), lambda bi, gi: (bi, 0, gi))
    kern = functools.partial(_dn_kernel, n_chunks=n_chunks, n_ctx_chunks=n_ctx // CHUNK)
    return pl.pallas_call(
        kern, grid=(b, ng),
        in_specs=[seq, seq, seq, pl.BlockSpec((1, 1, t, SCALAR_LANES), lambda bi, gi: (bi, gi, 0, 0))],
        out_specs=seq,
        out_shape=jax.ShapeDtypeStruct((b, t, DN_WIDTH), F32),
        scratch_shapes=[pltpu.VMEM((n_chunks, 2 * CHUNK, DN_LANES), BF16),
                        pltpu.VMEM((n_chunks, CHUNK, DN_LANES), F32),
                        pltpu.VMEM((n_chunks, CHUNK, DN_LANES), F32),
                        pltpu.VMEM((n_chunks, 8, DN_LANES), F32)],
        compiler_params=_cparams("parallel", "parallel"),
        name="gated_deltanet",
    )(dq, dk, dv, bgh)


def _outproj_kernel(a_ref, d_ref, gate_ref, x_ref, mod_ref, dnw_ref, bd_ref, woa_ref, wod_ref, nfw_ref,
                    wr_ref, br_ref, xo_ref, hf_ref, route_ref):
    dd = d_ref[0]
    gate = gate_ref[0].astype(F32)
    dn = dd * lax.rsqrt(_group_mean_sq(dd, bd_ref[...], DN_DIM) + EPS) * dnw_ref[...] * _silu(gate)
    y = (jnp.dot(a_ref[0], woa_ref[...], preferred_element_type=F32)
         + jnp.dot(dn.astype(BF16), wod_ref[...], preferred_element_type=F32))
    x = x_ref[0] + mod_ref[0, 2:3, :] * y
    xo_ref[0] = x
    h = x * lax.rsqrt(jnp.mean(x * x, axis=-1, keepdims=True) + EPS) * nfw_ref[...]
    h = h * (1.0 + mod_ref[0, 4:5, :]) + mod_ref[0, 3:4, :]
    hb = h.astype(BF16)
    hf_ref[0] = hb
    logits = jnp.dot(hb, wr_ref[...], preferred_element_type=F32) + br_ref[...]
    lane = lax.broadcasted_iota(jnp.int32, logits.shape, 1)
    neg = jnp.float32(-jnp.inf)
    big = jnp.int32(ROUTE_LANES)

    def first_argmax(vals, vmax):
        return jnp.min(jnp.where(vals == vmax, lane, big), axis=-1, keepdims=True)

    is_g = jnp.logical_and(lane >= N_EXPERTS, lane < N_EXPERTS + N_GROUPS)
    gl = jnp.where(is_g, logits, neg)
    g_max = jnp.max(gl, axis=-1, keepdims=True)
    g_sel = first_argmax(gl, g_max) - N_EXPERTS
    p_g = 1.0 / jnp.sum(jnp.exp(gl - g_max), axis=-1, keepdims=True)
    e_lo = g_sel * EXPERTS_PER_GROUP
    in_grp = jnp.logical_and(lane >= e_lo, lane < e_lo + EXPERTS_PER_GROUP)
    el = jnp.where(in_grp, logits, neg)
    e_max = jnp.max(el, axis=-1, keepdims=True)
    i1 = first_argmax(el, e_max)
    el2 = jnp.where(lane == i1, neg, el)
    e_max2 = jnp.max(el2, axis=-1, keepdims=True)
    i2 = first_argmax(el2, e_max2)
    p2 = jnp.exp(e_max2 - e_max)
    w1 = p_g / (1.0 + p2)
    w2 = p_g * p2 / (1.0 + p2)
    route_ref[0] = jnp.where(lane == i1, w1, jnp.where(lane == i2, w2, 0.0))


def _outproj(a, dsum, gate, xs, mods, dnw, bd, woa, wod, nfw, wr, br, *, n_ctx_tiles):
    b, t, d = xs.shape
    tm = TOKEN_TILE
    tok = lambda i, j: (j, i, 0)
    const = lambda i, j: (0, 0)
    modi = lambda i, j: (jnp.where(i < n_ctx_tiles, b, j), 0, 0)
    return pl.pallas_call(
        _outproj_kernel, grid=(t // tm, b),
        in_specs=[pl.BlockSpec((1, tm, ATTN_WIDTH), tok), pl.BlockSpec((1, tm, DN_WIDTH), tok),
                  pl.BlockSpec((1, tm, DN_WIDTH), tok), pl.BlockSpec((1, tm, d), tok),
                  pl.BlockSpec((1, 6, d), modi), pl.BlockSpec((1, DN_WIDTH), const),
                  pl.BlockSpec((DN_WIDTH, DN_WIDTH), const), pl.BlockSpec((ATTN_WIDTH, d), const),
                  pl.BlockSpec((DN_WIDTH, d), const), pl.BlockSpec((1, d), const),
                  pl.BlockSpec((d, ROUTE_LANES), const), pl.BlockSpec((1, ROUTE_LANES), const)],
        out_specs=[pl.BlockSpec((1, tm, d), tok), pl.BlockSpec((1, tm, d), tok),
                   pl.BlockSpec((1, tm, ROUTE_LANES), tok)],
        out_shape=[jax.ShapeDtypeStruct((b, t, d), F32), jax.ShapeDtypeStruct((b, t, d), BF16),
                   jax.ShapeDtypeStruct((b, t, ROUTE_LANES), F32)],
        input_output_aliases={3: 0},
        compiler_params=_cparams("parallel", "parallel"),
        name="outproj_router",
    )(a, dsum, gate, xs, mods, dnw, bd, woa, wod, nfw, wr, br)


def _moe_kernel(h_ref, r_ref, w1_ref, w3_ref, w2_ref, f_ref, acc_sc):
    e = pl.program_id(1)

    @pl.when(e == 0)
    def _():
        acc_sc[...] = jnp.zeros(acc_sc.shape, F32)

    r = r_ref[...]
    lane = lax.broadcasted_iota(jnp.int32, r.shape, 1)
    wt = jnp.sum(jnp.where(lane == e, r, 0.0), axis=-1, keepdims=True)
    hb = h_ref[...]
    a = jnp.dot(hb, w1_ref[0], preferred_element_type=F32)
    g = jnp.dot(hb, w3_ref[0], preferred_element_type=F32)
    mid = (_silu(a) * g * wt).astype(BF16)
    acc_sc[...] += jnp.dot(mid, w2_ref[0], preferred_element_type=F32)

    @pl.when(e == pl.num_programs(1) - 1)
    def _():
        f_ref[...] = acc_sc[...].astype(f_ref.dtype)


def _moe(hf, route, w1, w3, w2):
    n, d = hf.shape
    ne, _, ff = w1.shape
    tm = next(c for c in (2048, 1024, 512, 256) if n % c == 0)
    return pl.pallas_call(
        _moe_kernel, grid=(n // tm, ne),
        in_specs=[pl.BlockSpec((tm, d), lambda i, e: (i, 0)),
                  pl.BlockSpec((tm, ROUTE_LANES), lambda i, e: (i, 0)),
                  pl.BlockSpec((1, d, ff), lambda i, e: (e, 0, 0)),
                  pl.BlockSpec((1, d, ff), lambda i, e: (e, 0, 0)),
                  pl.BlockSpec((1, ff, d), lambda i, e: (e, 0, 0))],
        out_specs=pl.BlockSpec((tm, d), lambda i, e: (i, 0)),
        out_shape=jax.ShapeDtypeStruct((n, d), BF16),
        scratch_shapes=[pltpu.VMEM((tm, d), F32)],
        compiler_params=_cparams("parallel", "arbitrary"),
        name="moe_experts",
    )(hf, route, w1, w3, w2)


def _final_kernel(x_ref, f_ref, mod_ref, w_ref, o_ref):
    x = x_ref[0] + mod_ref[0, 5:6, :] * f_ref[0].astype(F32)
    o_ref[0] = x * lax.rsqrt(jnp.mean(x * x, axis=-1, keepdims=True) + EPS) * w_ref[...]


def _final_norm(xs, f_prev, mods, w, *, n_ctx_tiles):
    b, t, d = xs.shape
    tm = TOKEN_TILE
    n_lat = t // tm - n_ctx_tiles
    lat = lambda i, j: (j, i + n_ctx_tiles, 0)
    return pl.pallas_call(
        _final_kernel, grid=(n_lat, b),
        in_specs=[pl.BlockSpec((1, tm, d), lat), pl.BlockSpec((1, tm, d), lat),
                  pl.BlockSpec((1, 6, d), lambda i, j: (j, 0, 0)), pl.BlockSpec((1, d), lambda i, j: (0, 0))],
        out_specs=pl.BlockSpec((1, tm, d), lambda i, j: (j, i, 0)),
        out_shape=jax.ShapeDtypeStruct((b, n_lat * tm, d), F32),
        compiler_params=_cparams("parallel", "parallel"),
        name="final_norm",
    )(xs, f_prev, mods, w)


def _rope_tables(n_ctx, n_lat):
    pos = jnp.arange(n_lat, dtype=jnp.int32)
    inv = ROPE_THETA ** (-jnp.arange(ROPE_NF, dtype=F32) / ROPE_NF)
    ang_r = (pos // GRID_W).astype(F32)[:, None] * inv
    ang_c = (pos % GRID_W).astype(F32)[:, None] * inv
    cos = jnp.concatenate([jnp.cos(ang_r)] * 2 + [jnp.cos(ang_c)] * 2, axis=-1)
    sin = jnp.concatenate([-jnp.sin(ang_r), jnp.sin(ang_r), -jnp.sin(ang_c), jnp.sin(ang_c)], axis=-1)
    cos = jnp.concatenate([jnp.ones((n_ctx, HEAD_DIM), F32), cos], axis=0)
    sin = jnp.concatenate([jnp.zeros((n_ctx, HEAD_DIM), F32), sin], axis=0)
    reps = QK_WIDTH // HEAD_DIM
    return jnp.tile(cos, (1, reps)), jnp.tile(sin, (1, reps))


def kernel(x, c, ctx, c_ctx, ada_w, ada_b, norm_mix_w, norm_ffn_w, w_in, q_norm_w, k_norm_w, conv_w,
           dn_A_log, dn_dt_bias, dn_norm_w, w_out, rg_w, rg_b, re_w, re_b, w1, w3, w2, final_norm_w):
    b, s, d = x.shape
    n_ctx = ctx.shape[1]
    depth = w_in.shape[0]
    t = n_ctx + s
    assert n_ctx % TOKEN_TILE == 0 and s % TOKEN_TILE == 0 and s % GRID_W == 0
    n_ctx_tiles = n_ctx // TOKEN_TILE

    xs = jnp.concatenate([ctx, x], axis=1)
    mod_rows = -(-(b + 1) // 8) * 8
    cs = jnp.zeros((mod_rows, d), F32).at[:b].set(c).at[b].set(c_ctx)
    mods = _ada_mods(cs, ada_w, ada_b).reshape(depth, mod_rows, 6, d)

    cos_t, sin_t = _rope_tables(n_ctx, s)
    bd_qk = _block_ones(QK_WIDTH, HEAD_DIM)
    bd_dn = _block_ones(DN_WIDTH, DN_DIM)
    zeros16 = jnp.zeros((2 * DN_HEADS,), F32)

    f_prev = None
    for l in range(depth):
        qkw = jnp.concatenate([jnp.tile(q_norm_w[l], ATTN_HEADS), jnp.tile(k_norm_w[l], KV_HEADS)])[None]
        dnp = jnp.stack([jnp.concatenate([zeros16, dn_A_log[l].reshape(-1)]),
                         jnp.concatenate([zeros16, dn_dt_bias[l].reshape(-1)])])
        outs = _inproj(xs, f_prev, mods[l - 1] if l else None, mods[l], norm_mix_w[l][None],
                       w_in[l].astype(BF16), qkw, cos_t, sin_t, bd_qk, dnp, n_ctx_tiles=n_ctx_tiles)
        q, k, v, dqkv, gate, bg = outs[:6]
        if l:
            xs = outs[6]
        dq, dk, dv = _dn_conv(dqkv, conv_w[l], bd_dn, n_ctx_tiles=n_ctx_tiles)
        a = _attention(q, k, v, n_ctx=n_ctx)
        dsum = _deltanet(dq, dk, dv, bg, n_ctx=n_ctx)
        wo = w_out[l].astype(BF16)
        wr = jnp.zeros((d, ROUTE_LANES), F32).at[:, :N_EXPERTS].set(re_w[l]).at[
            :, N_EXPERTS:N_EXPERTS + N_GROUPS].set(rg_w[l]).astype(BF16)
        br = jnp.zeros((1, ROUTE_LANES), F32).at[0, :N_EXPERTS].set(re_b[l]).at[
            0, N_EXPERTS:N_EXPERTS + N_GROUPS].set(rg_b[l])
        xs, hf, route = _outproj(a, dsum, gate, xs, mods[l], jnp.tile(dn_norm_w[l], DN_HEADS)[None], bd_dn,
                                 wo[:ATTN_WIDTH], wo[ATTN_WIDTH:], norm_ffn_w[l][None], wr, br,
                                 n_ctx_tiles=n_ctx_tiles)
        f_prev = _moe(hf.reshape(b * t, d), route.reshape(b * t, ROUTE_LANES),
                      w1[l].astype(BF16), w3[l].astype(BF16), w2[l].astype(BF16)).reshape(b, t, d)
    return _final_norm(xs, f_prev, mods[depth - 1], final_norm_w[None], n_ctx_tiles=n_ctx_tiles)
```

```python
import functools
import math

import jax
import jax.numpy as jnp
from jax import lax
from jax.experimental import pallas as pl
from jax.experimental.pallas import tpu as pltpu

F32 = jnp.float32
BF16 = jnp.bfloat16
HIGHEST = lax.Precision.HIGHEST

GRID_W = 64
HEAD_DIM = 64
ATTN_HEADS = 8
KV_HEADS = 2
GQA_GROUP = ATTN_HEADS // KV_HEADS
ATTN_WIDTH = ATTN_HEADS * HEAD_DIM
KV_WIDTH = KV_HEADS * HEAD_DIM
ATTN_SCALE = HEAD_DIM ** -0.5
LOG2_E = math.log2(math.e)
ROPE_THETA = 10000.0
ROPE_NF = HEAD_DIM // 4
DN_HEADS = 8
DN_DIM = 64
DN_WIDTH = DN_HEADS * DN_DIM
DN_SCALE = DN_DIM ** -0.5
DN_GROUP = 4
CONV_K = 5
CHUNK = 64
N_GROUPS = 4
EXPERTS_PER_GROUP = 8
N_EXPERTS = N_GROUPS * EXPERTS_PER_GROUP
EPS = 1e-6
QK_WIDTH = ATTN_WIDTH + KV_WIDTH
TOKEN_TILE = 256
ATTN_KEY_TILE = 256
ONES_ROWS = 16
HALO = 16
ROUTE_LANES = 128
MOE_TILES = (1024, 512, 256)
MOE_SUB = 256
VMEM_LIMIT = 56 * 1024 * 1024


def _cparams(*sem):
    return pltpu.CompilerParams(dimension_semantics=sem, vmem_limit_bytes=VMEM_LIMIT)


def _silu(x):
    return x * jax.nn.sigmoid(x)


def _block_ones(n, blk):
    i = lax.broadcasted_iota(jnp.int32, (n, n), 0) // blk
    j = lax.broadcasted_iota(jnp.int32, (n, n), 1) // blk
    return (i == j).astype(BF16)


def _group_mean_sq(x, ones_bd, width):
    return jnp.dot((x * x).astype(BF16), ones_bd, preferred_element_type=F32) * (1.0 / width)


def _ada_kernel(cs_ref, w_ref, b_ref, o_ref):
    o_ref[0] = jnp.dot(_silu(cs_ref[...]), w_ref[0], preferred_element_type=F32,
                       precision=HIGHEST) + b_ref[0]


def _ada_mods(cs, ada_w, ada_b):
    depth, d, n6 = ada_w.shape
    rows = cs.shape[0]
    tn = 1536
    return pl.pallas_call(
        _ada_kernel,
        grid=(depth, n6 // tn),
        in_specs=[pl.BlockSpec((rows, d), lambda l, j: (0, 0)),
                  pl.BlockSpec((1, d, tn), lambda l, j: (l, 0, j)),
                  pl.BlockSpec((1, 1, tn), lambda l, j: (l, 0, j))],
        out_specs=pl.BlockSpec((1, rows, tn), lambda l, j: (l, 0, j)),
        out_shape=jax.ShapeDtypeStruct((depth, rows, n6), F32),
        compiler_params=_cparams("parallel", "parallel"),
        name="ada_mods",
    )(cs, ada_w, ada_b.reshape(depth, 1, n6))


def _inproj_kernel(*refs, has_prev):
    if has_prev:
        (x_ref, f_ref, modp_ref, mod_ref, nw_ref, win_ref, qkw_ref, cos_ref, sin_ref, bd_ref, dnp_ref,
         qt_ref, k_ref, vt_ref, dqkv_ref, gate_ref, bg_ref, xo_ref) = refs
        x = x_ref[0] + modp_ref[0, 5:6, :] * f_ref[0].astype(F32)
        xo_ref[0] = x
    else:
        (x_ref, mod_ref, nw_ref, win_ref, qkw_ref, cos_ref, sin_ref, bd_ref, dnp_ref,
         qt_ref, k_ref, vt_ref, dqkv_ref, gate_ref, bg_ref) = refs
        x = x_ref[0]
    h = x * lax.rsqrt(jnp.mean(x * x, axis=-1, keepdims=True) + EPS) * nw_ref[...]
    h = h * (1.0 + mod_ref[0, 1:2, :]) + mod_ref[0, 0:1, :]
    acc = jnp.dot(h.astype(BF16), win_ref[...], preferred_element_type=F32)
    qk = acc[:, :QK_WIDTH]
    qn = qk * lax.rsqrt(_group_mean_sq(qk, bd_ref[...], HEAD_DIM) + EPS) * qkw_ref[...]
    lane = lax.broadcasted_iota(jnp.int32, qn.shape, 1)
    partner = jnp.where(lane % (2 * ROPE_NF) < ROPE_NF,
                        pltpu.roll(qn, QK_WIDTH - ROPE_NF, 1), pltpu.roll(qn, ROPE_NF, 1))
    qr = qn * cos_ref[...] + partner * sin_ref[...]
    qt_ref[0] = (qr[:, :ATTN_WIDTH] * (ATTN_SCALE * LOG2_E)).T.astype(BF16)
    k_ref[0] = qr[:, ATTN_WIDTH:].astype(BF16)
    c0 = QK_WIDTH
    vt_ref[0] = acc[:, c0:c0 + KV_WIDTH].T.astype(BF16)
    c0 += KV_WIDTH
    dqkv_ref[0] = acc[:, c0:c0 + 3 * DN_WIDTH].astype(BF16)
    c0 += 3 * DN_WIDTH
    gate_ref[0] = acc[:, c0:c0 + DN_WIDTH].astype(BF16)
    c0 += DN_WIDTH
    z = acc[:, c0:c0 + 4 * DN_HEADS]
    zb = z + dnp_ref[1:2, :]
    softplus = jnp.maximum(zb, 0.0) + jnp.log1p(jnp.exp(-jnp.abs(zb)))
    lane_z = lax.broadcasted_iota(jnp.int32, z.shape, 1)
    bg_ref[0] = jnp.where(lane_z < 2 * DN_HEADS, jax.nn.sigmoid(z), -jnp.exp(dnp_ref[0:1, :]) * softplus)


def _inproj(xs, f_prev, mods_prev, mods, nw, win, qkw, cos_t, sin_t, bd, dnp, *, n_ctx_tiles):
    b, t, d = xs.shape
    tm = TOKEN_TILE
    nt = t // tm
    proj = win.shape[1]
    has_prev = f_prev is not None
    tok = lambda i, j: (j, i, 0)
    modi = lambda i, j: (jnp.where(i < n_ctx_tiles, b, j), 0, 0)
    const = lambda i, j: (0, 0)
    in_specs = [pl.BlockSpec((1, tm, d), tok)]
    args = [xs]
    if has_prev:
        in_specs += [pl.BlockSpec((1, tm, d), tok), pl.BlockSpec((1, 6, d), modi)]
        args += [f_prev, mods_prev]
    in_specs += [pl.BlockSpec((1, 6, d), modi), pl.BlockSpec((1, d), const),
                 pl.BlockSpec((d, proj), const), pl.BlockSpec((1, QK_WIDTH), const),
                 pl.BlockSpec((tm, QK_WIDTH), lambda i, j: (i, 0)),
                 pl.BlockSpec((tm, QK_WIDTH), lambda i, j: (i, 0)),
                 pl.BlockSpec((QK_WIDTH, QK_WIDTH), const), pl.BlockSpec((2, 4 * DN_HEADS), const)]
    args += [mods, nw, win, qkw, cos_t, sin_t, bd, dnp]
    seq_out = lambda w, dt: (pl.BlockSpec((1, tm, w), tok), jax.ShapeDtypeStruct((b, t, w), dt))
    tr_out = lambda w: (pl.BlockSpec((1, w, tm), lambda i, j: (j, 0, i)), jax.ShapeDtypeStruct((b, w, t), BF16))
    outs = [tr_out(ATTN_WIDTH), seq_out(KV_WIDTH, BF16), tr_out(KV_WIDTH),
            seq_out(3 * DN_WIDTH, BF16), seq_out(DN_WIDTH, BF16), seq_out(4 * DN_HEADS, F32)]
    out_specs = [o[0] for o in outs]
    out_shape = [o[1] for o in outs]
    aliases = {}
    if has_prev:
        out_specs.append(pl.BlockSpec((1, tm, d), tok))
        out_shape.append(jax.ShapeDtypeStruct((b, t, d), F32))
        aliases = {0: len(out_shape) - 1}
    return pl.pallas_call(
        functools.partial(_inproj_kernel, has_prev=has_prev),
        grid=(nt, b), in_specs=in_specs, out_specs=out_specs, out_shape=out_shape,
        input_output_aliases=aliases,
        compiler_params=_cparams("parallel", "parallel"),
        name="inproj",
    )(*args)


def _conv_kernel(x_ref, xb_ref, xa_ref, w_ref, bd_ref, dq_ref, dk_ref, dv_ref, *, n_ctx_tiles, n_tiles):
    i = pl.program_id(1)
    tm = x_ref.shape[1]
    first = jnp.logical_or(i == 0, i == n_ctx_tiles)
    last = jnp.logical_or(i == n_ctx_tiles - 1, i == n_tiles - 1)
    before = xb_ref[0].astype(F32)[HALO - 8:] * jnp.where(first, 0.0, 1.0)
    after = xa_ref[0].astype(F32)[:8] * jnp.where(last, 0.0, 1.0)
    ext = jnp.concatenate([before, x_ref[0].astype(F32), after], axis=0)
    n_ext = tm + 16
    y = None
    for j in range(CONV_K):
        shift = (CONV_K // 2 - j) % n_ext
        tap = ext if shift == 0 else pltpu.roll(ext, shift, 0)
        term = tap[8:8 + tm] * w_ref[j:j + 1, :]
        y = term if y is None else y + term
    y = _silu(y)
    q, k, v = y[:, :DN_WIDTH], y[:, DN_WIDTH:2 * DN_WIDTH], y[:, 2 * DN_WIDTH:]
    bd = bd_ref[...]
    dq_ref[0] = (q * lax.rsqrt(_group_mean_sq(q, bd, 1.0) + EPS) * DN_SCALE).astype(BF16)
    dk_ref[0] = (k * lax.rsqrt(_group_mean_sq(k, bd, 1.0) + EPS)).astype(BF16)
    dv_ref[0] = v.astype(BF16)


def _dn_conv(dqkv, conv_w, bd, *, n_ctx_tiles):
    b, t, c = dqkv.shape
    tm = TOKEN_TILE
    nt = t // tm
    r = tm // HALO
    n_halo = t // HALO
    kern = functools.partial(_conv_kernel, n_ctx_tiles=n_ctx_tiles, n_tiles=nt)
    out = jax.ShapeDtypeStruct((b, t, DN_WIDTH), BF16)
    return pl.pallas_call(
        kern, grid=(b, nt),
        in_specs=[pl.BlockSpec((1, tm, c), lambda bi, i: (bi, i, 0)),
                  pl.BlockSpec((1, HALO, c), lambda bi, i: (bi, jnp.maximum(i * r - 1, 0), 0)),
                  pl.BlockSpec((1, HALO, c), lambda bi, i: (bi, jnp.minimum((i + 1) * r, n_halo - 1), 0)),
                  pl.BlockSpec((CONV_K, c), lambda bi, i: (0, 0)),
                  pl.BlockSpec((DN_WIDTH, DN_WIDTH), lambda bi, i: (0, 0))],
        out_specs=[pl.BlockSpec((1, tm, DN_WIDTH), lambda bi, i: (bi, i, 0))] * 3,
        out_shape=[out, out, out],
        compiler_params=_cparams("parallel", "parallel"),
        name="dn_conv",
    )(dqkv, dqkv, dqkv, conv_w, bd)


def _attn_kernel(qt_ref, k_ref, vt_ref, o_ref, acc_sc, *, tk, n_ctx_q, n_ctx_k, n_all_k):
    tq = qt_ref.shape[2]
    n_kv = jnp.where(pl.program_id(1) < n_ctx_q, n_ctx_k, n_all_k)
    heads = range(ATTN_HEADS)
    acc_sc[...] = jnp.zeros(acc_sc.shape, F32)

    def body(i, carry):
        rows = pl.ds(pl.multiple_of(i * tk, tk), tk)
        m_prev, l_prev = carry[:ATTN_HEADS], carry[ATTN_HEADS:]
        s = [jnp.dot(k_ref[0, rows, (h // GQA_GROUP) * HEAD_DIM:(h // GQA_GROUP + 1) * HEAD_DIM],
                     qt_ref[0, h * HEAD_DIM:(h + 1) * HEAD_DIM, :], preferred_element_type=F32)
             for h in heads]
        m_new = [jnp.maximum(m_prev[h], jnp.max(s[h], axis=0, keepdims=True)) for h in heads]
        p = [jnp.exp2(s[h] - m_new[h]) for h in heads]
        alpha = [jnp.exp2(m_prev[h] - m_new[h]) for h in heads]
        ones = jnp.ones((ONES_ROWS, tk), BF16)
        vt1 = [jnp.concatenate([vt_ref[0, g * HEAD_DIM:(g + 1) * HEAD_DIM, rows], ones], axis=0)
               for g in range(KV_HEADS)]
        pv = [jnp.dot(vt1[h // GQA_GROUP], p[h].astype(BF16), preferred_element_type=F32) for h in heads]
        for h in heads:
            acc_sc[h] = alpha[h] * acc_sc[h] + pv[h][:HEAD_DIM]
        l_new = [alpha[h] * l_prev[h] + pv[h][HEAD_DIM:HEAD_DIM + 1] for h in heads]
        return tuple(m_new) + tuple(l_new)

    init = (jnp.full((1, tq), -jnp.inf, F32),) * ATTN_HEADS + (jnp.zeros((1, tq), F32),) * ATTN_HEADS
    stats = lax.fori_loop(0, n_kv, body, init)
    og = jnp.concatenate([acc_sc[h] / stats[ATTN_HEADS + h] for h in heads], axis=0)
    o_ref[0] = og.T.astype(BF16)


def _attention(qt, k, vt, *, n_ctx):
    b, _, t = qt.shape
    tq, tk = TOKEN_TILE, ATTN_KEY_TILE
    kern = functools.partial(_attn_kernel, tk=tk, n_ctx_q=n_ctx // tq, n_ctx_k=n_ctx // tk, n_all_k=t // tk)
    return pl.pallas_call(
        kern, grid=(b, t // tq),
        in_specs=[pl.BlockSpec((1, ATTN_WIDTH, tq), lambda bi, i: (bi, 0, i)),
                  pl.BlockSpec((1, t, KV_WIDTH), lambda bi, i: (bi, 0, 0)),
                  pl.BlockSpec((1, KV_WIDTH, t), lambda bi, i: (bi, 0, 0))],
        out_specs=pl.BlockSpec((1, tq, ATTN_WIDTH), lambda bi, i: (bi, i, 0)),
        out_shape=jax.ShapeDtypeStruct((b, t, ATTN_WIDTH), BF16),
        scratch_shapes=[pltpu.VMEM((ATTN_HEADS, HEAD_DIM, tq), F32)],
        compiler_params=_cparams("parallel", "parallel"),
        name="gqa_attention",
    )(qt, k, vt)


DN_LANES = DN_GROUP * DN_DIM
DN_BLOCK = 4
INV_BLOCK = 16
SCALAR_LANES = 128


def _split3(x):
    x1 = x.astype(BF16)
    r1 = x - x1.astype(F32)
    x2 = r1.astype(BF16)
    x3 = (r1 - x2.astype(F32)).astype(BF16)
    return jnp.concatenate([x1, x2, x3], axis=1)


def _sum3(y):
    w = y.shape[1] // 3
    return y[:, :w] + y[:, w:2 * w] + y[:, 2 * w:]


def _dn_kernel(dq_ref, dk_ref, dv_ref, bg_ref, o_ref, mp_sc, n_sc, r_sc, gam_sc, *, n_chunks, n_ctx_chunks):
    c = CHUNK
    rows_b = DN_BLOCK * c
    ln = DN_LANES
    ri = lax.broadcasted_iota(jnp.int32, (rows_b, ln), 0)
    li = lax.broadcasted_iota(jnp.int32, (rows_b, ln), 1)
    same = (ri // c) == (li // c)
    i_in, j_in = ri % c, li % c
    eye_rc = i_in == j_in
    near = (i_in // INV_BLOCK) == (j_in // INV_BLOCK)
    e_row = lax.broadcasted_iota(jnp.int32, (3 * SCALAR_LANES, 2 * ln), 0) % SCALAR_LANES
    e_blk = lax.broadcasted_iota(jnp.int32, (3 * SCALAR_LANES, 2 * ln), 1) // c
    zl = lax.broadcasted_iota(jnp.int32, (rows_b, SCALAR_LANES), 1)
    nt_dims = (((1,), (1,)), ((), ()))
    tn_dims = (((0,), (0,)), ((), ()))

    def bdiag(x):
        return jnp.where(same, jnp.concatenate([x] * DN_GROUP, axis=0), jnp.zeros((), x.dtype))

    def fold(x):
        x = jnp.where(same, x, 0.0)
        return x[0:c] + x[c:2 * c] + x[2 * c:3 * c] + x[3 * c:4 * c]

    for d in range(2):
        incl = (i_in >= j_in) if d == 0 else (i_in <= j_in)
        strict = (i_in > j_in) if d == 0 else (i_in < j_in)
        last = c - 1 if d == 0 else 0
        tri_bd = jnp.logical_and(same, incl).astype(BF16)
        sel_last = jnp.logical_and(same, j_in == last).astype(BF16)
        same_b = same.astype(BF16)
        col0 = d * 2 * DN_GROUP
        expand = jnp.logical_and(e_row == e_blk + col0, e_blk < 2 * DN_GROUP).astype(BF16)
        is_beta = jnp.logical_and(zl >= col0, zl < col0 + DN_GROUP)

        def prepare(blk, carry, d=d, incl=incl, strict=strict, tri_bd=tri_bd, sel_last=sel_last,
                    same_b=same_b, expand=expand, is_beta=is_beta):
            rs = pl.ds(pl.multiple_of(blk * rows_b, rows_b), rows_b)
            z = bg_ref[0, 0, rs, :]
            cum = _sum3(jnp.dot(tri_bd, _split3(z), preferred_element_type=F32))
            zz = jnp.where(is_beta, z, cum)
            ex = jnp.dot(_split3(zz), expand, preferred_element_type=F32)
            beta, gce = ex[:, :ln], ex[:, ln:]
            g_row = _sum3(jnp.dot(same_b, _split3(jnp.where(eye_rc, gce, 0.0)), preferred_element_type=F32))
            g_last = _sum3(jnp.dot(sel_last, _split3(gce), preferred_element_type=F32))
            decay = jnp.where(incl, jnp.exp(jnp.where(incl, gce - g_row, 0.0)), 0.0)
            e_g = jnp.exp(gce)
            kbf = dk_ref[0, rs, :]
            qbf = dq_ref[0, rs, :]
            k = kbf.astype(F32)
            kb = k * beta
            kbe = (kb * e_g).astype(BF16)
            kbb = kb.astype(BF16)
            vb = (dv_ref[0, rs, :].astype(F32) * beta).astype(BF16)
            q_dec = qbf.astype(F32) * e_g
            k_dec = (k * jnp.exp(g_last - gce)).astype(BF16)
            gamma = jnp.exp(g_last)
            eye_f = eye_rc.astype(F32)
            chunks = range(DN_BLOCK)
            sls = [slice(ch * c, (ch + 1) * c) for ch in chunks]
            mm = lambda a, bmat: jnp.dot(a.astype(BF16), bmat, preferred_element_type=F32)
            sc = [lax.dot_general(jnp.concatenate([kbb[sl], qbf[sl]], axis=0), bdiag(kbf[sl]), nt_dims,
                                  preferred_element_type=F32) for sl in sls]
            intra = [jnp.where(incl[sl], s[c:] * decay[sl], 0.0).astype(BF16) for s, sl in zip(sc, sls)]
            lfull = [jnp.where(strict[sl], s[:c] * decay[sl], 0.0) for s, sl in zip(sc, sls)]
            p = [jnp.where(near[sl], -lf, 0.0) for lf, sl in zip(lfull, sls)]
            l_off = [jnp.where(near[sl], 0.0, lf).astype(BF16) for lf, sl in zip(lfull, sls)]
            tmat = [eye_f[sl] + pi for pi, sl in zip(p, sls)]
            p = [mm(pi, bdiag(pi.astype(BF16))) for pi in p]
            for _ in range(int(math.log2(INV_BLOCK)) - 2):
                res = [mm(jnp.concatenate([ti, pi], axis=0), bdiag(pi.astype(BF16))) for ti, pi in zip(tmat, p)]
                tmat = [ti + ri[:c] for ti, ri in zip(tmat, res)]
                p = [ri[c:] for ri in res]
            tmat = [ti + mm(ti, bdiag(pi.astype(BF16))) for ti, pi in zip(tmat, p)]
            nmat = [mm(ti, bdiag(lo)) for ti, lo in zip(tmat, l_off)]
            n2 = [mm(ni, bdiag(ni.astype(BF16))) for ni in nmat]
            tmat = [ti + mm(qi, bdiag(ti.astype(BF16))) for ti, qi in zip(tmat, n2)]
            tmat = [ti - mm(ni, bdiag(ti.astype(BF16))) for ti, ni in zip(tmat, nmat)]
            uw = [mm(ti, jnp.concatenate([bdiag(vb[sl]), bdiag(kbe[sl])], axis=1)).astype(BF16)
                  for ti, sl in zip(tmat, sls)]
            aw_au = [mm(ai, jnp.concatenate([bdiag(x[:, ln:]), bdiag(x[:, :ln])], axis=1))
                     for ai, x in zip(intra, uw)]
            mn = [lax.dot_general(k_dec[sl], jnp.concatenate([x[:, ln:], x[:, :ln]], axis=1), tn_dims,
                                  preferred_element_type=F32) for x, sl in zip(uw, sls)]
            for ch in chunks:
                ci = blk * DN_BLOCK + ch
                p_c = q_dec[sls[ch]] - aw_au[ch][:, :ln]
                mp_sc[ci] = jnp.concatenate([fold(mn[ch][:, :ln]), p_c], axis=0).astype(BF16)
                n_sc[ci] = fold(mn[ch][:, ln:])
                r_sc[ci] = aw_au[ch][:, ln:]
                gam_sc[ci] = gamma[ch * c:ch * c + 8]
            return carry

        lax.fori_loop(0, n_chunks // DN_BLOCK, prepare, 0)

        def scan(s, state, d=d):
            if d == 0:
                ci = s
            else:
                ci = jnp.where(s < n_ctx_chunks, n_ctx_chunks - 1 - s, n_chunks - 1 + n_ctx_chunks - s)
            res = jnp.dot(mp_sc[ci], bdiag(state.astype(BF16)), preferred_element_type=F32)
            rows = pl.ds(pl.multiple_of(ci * c, c), c)
            o = res[c:] + r_sc[ci]
            if d == 0:
                o_ref[0, rows, :] = o
            else:
                o_ref[0, rows, :] += o
            return gam_sc[ci][0:1] * state - res[:c] + n_sc[ci]

        lax.fori_loop(0, n_chunks, scan, jnp.zeros((c, ln), F32))


def _deltanet(dq, dk, dv, bg, *, n_ctx):
    b, t, _ = dq.shape
    n_chunks = t // CHUNK
    assert n_chunks % DN_BLOCK == 0 and DN_GROUP * CHUNK == DN_LANES
    ng = DN_HEADS // DN_GROUP
    bgh = bg.reshape(b, t, 2, 2, ng, DN_GROUP).transpose(0, 4, 1, 3, 2, 5).reshape(b, ng, t, 4 * DN_GROUP)
    bgh = jnp.pad(bgh, ((0, 0), (0, 0), (0, 0), (0, SCALAR_LANES - 4 * DN_GROUP)))
    seq = pl.BlockSpec((1, t, DN_LANES), lambda bi, gi: (bi, 0, gi))
    kern = functools.partial(_dn_kernel, n_chunks=n_chunks, n_ctx_chunks=n_ctx // CHUNK)
    return pl.pallas_call(
        kern, grid=(b, ng),
        in_specs=[seq, seq, seq, pl.BlockSpec((1, 1, t, SCALAR_LANES), lambda bi, gi: (bi, gi, 0, 0))],
        out_specs=seq,
        out_shape=jax.ShapeDtypeStruct((b, t, DN_WIDTH), F32),
        scratch_shapes=[pltpu.VMEM((n_chunks, 2 * CHUNK, DN_LANES), BF16),
                        pltpu.VMEM((n_chunks, CHUNK, DN_LANES), F32),
                        pltpu.VMEM((n_chunks, CHUNK, DN_LANES), F32),
                        pltpu.VMEM((n_chunks, 8, DN_LANES), F32)],
        compiler_params=_cparams("parallel", "parallel"),
        name="gated_deltanet",
    )(dq, dk, dv, bgh)


def _outproj_kernel(a_ref, d_ref, gate_ref, x_ref, mod_ref, dnw_ref, bd_ref, woa_ref, wod_ref, nfw_ref,
                    wr_ref, br_ref, xo_ref, hf_ref, route_ref, grp_ref, cnt_ref):
    dd = d_ref[0]
    gate = gate_ref[0].astype(F32)
    dn = dd * lax.rsqrt(_group_mean_sq(dd, bd_ref[...], DN_DIM) + EPS) * dnw_ref[...] * _silu(gate)
    y = (jnp.dot(a_ref[0], woa_ref[...], preferred_element_type=F32)
         + jnp.dot(dn.astype(BF16), wod_ref[...], preferred_element_type=F32))
    x = x_ref[0] + mod_ref[0, 2:3, :] * y
    xo_ref[0] = x
    h = x * lax.rsqrt(jnp.mean(x * x, axis=-1, keepdims=True) + EPS) * nfw_ref[...]
    h = h * (1.0 + mod_ref[0, 4:5, :]) + mod_ref[0, 3:4, :]
    hb = h.astype(BF16)
    hf_ref[0] = hb
    logits = jnp.dot(hb, wr_ref[...], preferred_element_type=F32) + br_ref[...]
    lane = lax.broadcasted_iota(jnp.int32, logits.shape, 1)
    neg = jnp.float32(-jnp.inf)
    big = jnp.int32(ROUTE_LANES)

    def first_argmax(vals, vmax):
        return jnp.min(jnp.where(vals == vmax, lane, big), axis=-1, keepdims=True)

    is_g = jnp.logical_and(lane >= N_EXPERTS, lane < N_EXPERTS + N_GROUPS)
    gl = jnp.where(is_g, logits, neg)
    g_max = jnp.max(gl, axis=-1, keepdims=True)
    g_sel = first_argmax(gl, g_max) - N_EXPERTS
    p_g = 1.0 / jnp.sum(jnp.exp(gl - g_max), axis=-1, keepdims=True)
    e_lo = g_sel * EXPERTS_PER_GROUP
    in_grp = jnp.logical_and(lane >= e_lo, lane < e_lo + EXPERTS_PER_GROUP)
    el = jnp.where(in_grp, logits, neg)
    e_max = jnp.max(el, axis=-1, keepdims=True)
    i1 = first_argmax(el, e_max)
    el2 = jnp.where(lane == i1, neg, el)
    e_max2 = jnp.max(el2, axis=-1, keepdims=True)
    i2 = first_argmax(el2, e_max2)
    p2 = jnp.exp(e_max2 - e_max)
    w1 = p_g / (1.0 + p2)
    w2 = p_g * p2 / (1.0 + p2)
    route_ref[0] = jnp.where(lane == i1, w1, jnp.where(lane == i2, w2, 0.0)).astype(BF16)
    g_lanes = jnp.broadcast_to(g_sel.astype(F32), logits.shape)
    grp_ref[...] = g_lanes.T[:8]
    cnt_ref[0, 0] = jnp.broadcast_to(
        jnp.sum(jnp.where(lane == g_sel, 1.0, 0.0), axis=0, keepdims=True), (8, ROUTE_LANES))


def _outproj(a, dsum, gate, xs, mods, dnw, bd, woa, wod, nfw, wr, br, *, n_ctx_tiles):
    b, t, d = xs.shape
    tm = TOKEN_TILE
    tok = lambda i, j: (j, i, 0)
    const = lambda i, j: (0, 0)
    modi = lambda i, j: (jnp.where(i < n_ctx_tiles, b, j), 0, 0)
    return pl.pallas_call(
        _outproj_kernel, grid=(t // tm, b),
        in_specs=[pl.BlockSpec((1, tm, ATTN_WIDTH), tok), pl.BlockSpec((1, tm, DN_WIDTH), tok),
                  pl.BlockSpec((1, tm, DN_WIDTH), tok), pl.BlockSpec((1, tm, d), tok),
                  pl.BlockSpec((1, 6, d), modi), pl.BlockSpec((1, DN_WIDTH), const),
                  pl.BlockSpec((DN_WIDTH, DN_WIDTH), const), pl.BlockSpec((ATTN_WIDTH, d), const),
                  pl.BlockSpec((DN_WIDTH, d), const), pl.BlockSpec((1, d), const),
                  pl.BlockSpec((d, ROUTE_LANES), const), pl.BlockSpec((1, ROUTE_LANES), const)],
        out_specs=[pl.BlockSpec((1, tm, d), tok), pl.BlockSpec((1, tm, d), tok),
                   pl.BlockSpec((1, tm, ROUTE_LANES), tok),
                   pl.BlockSpec((8, tm), lambda i, j: (0, j * (t // tm) + i)),
                   pl.BlockSpec((1, 1, 8, ROUTE_LANES), lambda i, j: (j, i, 0, 0))],
        out_shape=[jax.ShapeDtypeStruct((b, t, d), F32), jax.ShapeDtypeStruct((b, t, d), BF16),
                   jax.ShapeDtypeStruct((b, t, ROUTE_LANES), BF16),
                   jax.ShapeDtypeStruct((8, b * t), F32),
                   jax.ShapeDtypeStruct((b, t // tm, 8, ROUTE_LANES), F32)],
        input_output_aliases={3: 0},
        compiler_params=_cparams("parallel", "parallel"),
        name="outproj_router",
    )(a, dsum, gate, xs, mods, dnw, bd, woa, wod, nfw, wr, br)


def _moe_kernel(cnt_ref, h_ref, r_ref, grp_ref, tri_ref, w1_ref, w3_ref, w2_ref, f_ref, acc_sc):
    m, g = pl.program_id(0), pl.program_id(1)
    tm = h_ref.shape[0]
    ff = w1_ref.shape[2]

    @pl.when(g == 0)
    def _():
        acc_sc[...] = jnp.zeros(acc_sc.shape, F32)

    member = grp_ref[0:1, :] == g.astype(F32)
    prefix = jnp.dot(jnp.broadcast_to(member.astype(BF16), (16, tm)), tri_ref[...],
                     preferred_element_type=F32)[0:1]
    pos = jnp.where(member, prefix - 1.0, -1.0)
    e_row = lax.broadcasted_iota(jnp.int32, (ROUTE_LANES, EXPERTS_PER_GROUP * ff), 0)
    e_col = lax.broadcasted_iota(jnp.int32, (ROUTE_LANES, EXPERTS_PER_GROUP * ff), 1) // ff
    expand = (e_row == g * EXPERTS_PER_GROUP + e_col).astype(BF16)
    slot = lax.broadcasted_iota(jnp.int32, (MOE_SUB, tm), 0).astype(F32)

    def sub_block(sb, carry):
        sel = (pos == slot + (sb * MOE_SUB).astype(F32)).astype(BF16)
        xs = jnp.dot(sel, h_ref[...], preferred_element_type=F32).astype(BF16)
        wt = jnp.dot(sel, r_ref[...], preferred_element_type=F32).astype(BF16)
        wexp = jnp.dot(wt, expand, preferred_element_type=F32)
        y = None
        for e in range(EXPERTS_PER_GROUP):
            a = jnp.dot(xs, w1_ref[e], preferred_element_type=F32)
            gate = jnp.dot(xs, w3_ref[e], preferred_element_type=F32)
            mid = (_silu(a) * gate * wexp[:, e * ff:(e + 1) * ff]).astype(BF16)
            ye = jnp.dot(mid, w2_ref[e], preferred_element_type=F32)
            y = ye if y is None else y + ye
        acc_sc[...] += lax.dot_general(sel, y.astype(BF16), (((0,), (0,)), ((), ())),
                                       preferred_element_type=F32)
        return carry

    n_sub = (cnt_ref[m * N_GROUPS + g] + MOE_SUB - 1) // MOE_SUB
    lax.fori_loop(0, n_sub, sub_block, 0)

    @pl.when(g == N_GROUPS - 1)
    def _():
        f_ref[...] = acc_sc[...].astype(f_ref.dtype)


def _moe(hf, route, grp, counts, tri, w1, w3, w2):
    n, d = hf.shape
    _, _, ff = w1.shape
    tm = tri.shape[0]
    epg = EXPERTS_PER_GROUP
    grid_spec = pltpu.PrefetchScalarGridSpec(
        num_scalar_prefetch=1, grid=(n // tm, N_GROUPS),
        in_specs=[pl.BlockSpec((tm, d), lambda i, g, c: (i, 0)),
                  pl.BlockSpec((tm, ROUTE_LANES), lambda i, g, c: (i, 0)),
                  pl.BlockSpec((8, tm), lambda i, g, c: (0, i)),
                  pl.BlockSpec((tm, tm), lambda i, g, c: (0, 0)),
                  pl.BlockSpec((epg, d, ff), lambda i, g, c: (g, 0, 0)),
                  pl.BlockSpec((epg, d, ff), lambda i, g, c: (g, 0, 0)),
                  pl.BlockSpec((epg, ff, d), lambda i, g, c: (g, 0, 0))],
        out_specs=pl.BlockSpec((tm, d), lambda i, g, c: (i, 0)),
        scratch_shapes=[pltpu.VMEM((tm, d), F32)])
    return pl.pallas_call(
        _moe_kernel, grid_spec=grid_spec,
        out_shape=jax.ShapeDtypeStruct((n, d), BF16),
        compiler_params=_cparams("parallel", "arbitrary"),
        name="moe_experts",
    )(counts, hf, route, grp, tri, w1, w3, w2)


def _final_kernel(x_ref, f_ref, mod_ref, w_ref, o_ref):
    x = x_ref[0] + mod_ref[0, 5:6, :] * f_ref[0].astype(F32)
    o_ref[0] = x * lax.rsqrt(jnp.mean(x * x, axis=-1, keepdims=True) + EPS) * w_ref[...]


def _final_norm(xs, f_prev, mods, w, *, n_ctx_tiles):
    b, t, d = xs.shape
    tm = TOKEN_TILE
    n_lat = t // tm - n_ctx_tiles
    lat = lambda i, j: (j, i + n_ctx_tiles, 0)
    return pl.pallas_call(
        _final_kernel, grid=(n_lat, b),
        in_specs=[pl.BlockSpec((1, tm, d), lat), pl.BlockSpec((1, tm, d), lat),
                  pl.BlockSpec((1, 6, d), lambda i, j: (j, 0, 0)), pl.BlockSpec((1, d), lambda i, j: (0, 0))],
        out_specs=pl.BlockSpec((1, tm, d), lambda i, j: (j, i, 0)),
        out_shape=jax.ShapeDtypeStruct((b, n_lat * tm, d), F32),
        compiler_params=_cparams("parallel", "parallel"),
        name="final_norm",
    )(xs, f_prev, mods, w)


def _rope_tables(n_ctx, n_lat):
    pos = jnp.arange(n_lat, dtype=jnp.int32)
    inv = ROPE_THETA ** (-jnp.arange(ROPE_NF, dtype=F32) / ROPE_NF)
    ang_r = (pos // GRID_W).astype(F32)[:, None] * inv
    ang_c = (pos % GRID_W).astype(F32)[:, None] * inv
    cos = jnp.concatenate([jnp.cos(ang_r)] * 2 + [jnp.cos(ang_c)] * 2, axis=-1)
    sin = jnp.concatenate([-jnp.sin(ang_r), jnp.sin(ang_r), -jnp.sin(ang_c), jnp.sin(ang_c)], axis=-1)
    cos = jnp.concatenate([jnp.ones((n_ctx, HEAD_DIM), F32), cos], axis=0)
    sin = jnp.concatenate([jnp.zeros((n_ctx, HEAD_DIM), F32), sin], axis=0)
    reps = QK_WIDTH // HEAD_DIM
    return jnp.tile(cos, (1, reps)), jnp.tile(sin, (1, reps))


def kernel(x, c, ctx, c_ctx, ada_w, ada_b, norm_mix_w, norm_ffn_w, w_in, q_norm_w, k_norm_w, conv_w,
           dn_A_log, dn_dt_bias, dn_norm_w, w_out, rg_w, rg_b, re_w, re_b, w1, w3, w2, final_norm_w):
    b, s, d = x.shape
    n_ctx = ctx.shape[1]
    depth = w_in.shape[0]
    t = n_ctx + s
    assert n_ctx % TOKEN_TILE == 0 and s % TOKEN_TILE == 0 and s % GRID_W == 0
    n_ctx_tiles = n_ctx // TOKEN_TILE

    xs = jnp.concatenate([ctx, x], axis=1)
    mod_rows = -(-(b + 1) // 8) * 8
    cs = jnp.zeros((mod_rows, d), F32).at[:b].set(c).at[b].set(c_ctx)
    mods = _ada_mods(cs, ada_w, ada_b).reshape(depth, mod_rows, 6, d)

    cos_t, sin_t = _rope_tables(n_ctx, s)
    bd_qk = _block_ones(QK_WIDTH, HEAD_DIM)
    bd_dn = _block_ones(DN_WIDTH, DN_DIM)
    zeros16 = jnp.zeros((2 * DN_HEADS,), F32)
    moe_tile = next(c for c in MOE_TILES if (b * t) % c == 0)
    tri = (lax.broadcasted_iota(jnp.int32, (moe_tile, moe_tile), 0)
           <= lax.broadcasted_iota(jnp.int32, (moe_tile, moe_tile), 1)).astype(BF16)

    f_prev = None
    for l in range(depth):
        qkw = jnp.concatenate([jnp.tile(q_norm_w[l], ATTN_HEADS), jnp.tile(k_norm_w[l], KV_HEADS)])[None]
        dnp = jnp.stack([jnp.concatenate([zeros16, dn_A_log[l].reshape(-1)]),
                         jnp.concatenate([zeros16, dn_dt_bias[l].reshape(-1)])])
        outs = _inproj(xs, f_prev, mods[l - 1] if l else None, mods[l], norm_mix_w[l][None],
                       w_in[l].astype(BF16), qkw, cos_t, sin_t, bd_qk, dnp, n_ctx_tiles=n_ctx_tiles)
        qt, k, vt, dqkv, gate, bg = outs[:6]
        if l:
            xs = outs[6]
        dq, dk, dv = _dn_conv(dqkv, conv_w[l], bd_dn, n_ctx_tiles=n_ctx_tiles)
        a = _attention(qt, k, vt, n_ctx=n_ctx)
        dsum = _deltanet(dq, dk, dv, bg, n_ctx=n_ctx)
        wo = w_out[l].astype(BF16)
        wr = jnp.zeros((d, ROUTE_LANES), F32).at[:, :N_EXPERTS].set(re_w[l]).at[
            :, N_EXPERTS:N_EXPERTS + N_GROUPS].set(rg_w[l]).astype(BF16)
        br = jnp.zeros((1, ROUTE_LANES), F32).at[0, :N_EXPERTS].set(re_b[l]).at[
            0, N_EXPERTS:N_EXPERTS + N_GROUPS].set(rg_b[l])
        xs, hf, route, grp, cnt = _outproj(
            a, dsum, gate, xs, mods[l], jnp.tile(dn_norm_w[l], DN_HEADS)[None], bd_dn,
            wo[:ATTN_WIDTH], wo[ATTN_WIDTH:], norm_ffn_w[l][None], wr, br, n_ctx_tiles=n_ctx_tiles)
        counts = cnt[:, :, 0, :N_GROUPS].reshape(-1, moe_tile // TOKEN_TILE, N_GROUPS).sum(axis=1)
        f_prev = _moe(hf.reshape(b * t, d), route.reshape(b * t, ROUTE_LANES), grp,
                      counts.astype(jnp.int32).reshape(-1), tri,
                      w1[l].astype(BF16), w3[l].astype(BF16), w2[l].astype(BF16)).reshape(b, t, d)
    return _final_norm(xs, f_prev, mods[depth - 1], final_norm_w[None], n_ctx_tiles=n_ctx_tiles)
```

```python
import functools
import math

import jax
import jax.numpy as jnp
from jax import lax
from jax.experimental import pallas as pl
from jax.experimental.pallas import tpu as pltpu

F32 = jnp.float32
BF16 = jnp.bfloat16
HIGHEST = lax.Precision.HIGHEST

GRID_W = 64
HEAD_DIM = 64
ATTN_HEADS = 8
KV_HEADS = 2
GQA_GROUP = ATTN_HEADS // KV_HEADS
ATTN_WIDTH = ATTN_HEADS * HEAD_DIM
KV_WIDTH = KV_HEADS * HEAD_DIM
ATTN_SCALE = HEAD_DIM ** -0.5
LOG2_E = math.log2(math.e)
ROPE_THETA = 10000.0
ROPE_NF = HEAD_DIM // 4
DN_HEADS = 8
DN_DIM = 64
DN_WIDTH = DN_HEADS * DN_DIM
DN_SCALE = DN_DIM ** -0.5
DN_GROUP = 4
CONV_K = 5
CHUNK = 64
N_GROUPS = 4
EXPERTS_PER_GROUP = 8
N_EXPERTS = N_GROUPS * EXPERTS_PER_GROUP
EPS = 1e-6
QK_WIDTH = ATTN_WIDTH + KV_WIDTH
TOKEN_TILE = 256
ATTN_KEY_TILE = 256
ONES_ROWS = 16
HALO = 16
ROUTE_LANES = 128
MOE_TILES = (1024, 512, 256)
MOE_SUB = 256
MOE_TAILS = (64, 128, 256)
VMEM_LIMIT = 56 * 1024 * 1024


def _cparams(*sem):
    return pltpu.CompilerParams(dimension_semantics=sem, vmem_limit_bytes=VMEM_LIMIT)


def _silu(x):
    return x * jax.nn.sigmoid(x)


def _block_ones(n, blk):
    i = lax.broadcasted_iota(jnp.int32, (n, n), 0) // blk
    j = lax.broadcasted_iota(jnp.int32, (n, n), 1) // blk
    return (i == j).astype(BF16)


def _group_mean_sq(x, ones_bd, width):
    return jnp.dot((x * x).astype(BF16), ones_bd, preferred_element_type=F32) * (1.0 / width)


def _ada_kernel(cs_ref, w_ref, b_ref, o_ref):
    o_ref[0] = jnp.dot(_silu(cs_ref[...]), w_ref[0], preferred_element_type=F32,
                       precision=HIGHEST) + b_ref[0]


def _ada_mods(cs, ada_w, ada_b):
    depth, d, n6 = ada_w.shape
    rows = cs.shape[0]
    tn = 1536
    return pl.pallas_call(
        _ada_kernel,
        grid=(depth, n6 // tn),
        in_specs=[pl.BlockSpec((rows, d), lambda l, j: (0, 0)),
                  pl.BlockSpec((1, d, tn), lambda l, j: (l, 0, j)),
                  pl.BlockSpec((1, 1, tn), lambda l, j: (l, 0, j))],
        out_specs=pl.BlockSpec((1, rows, tn), lambda l, j: (l, 0, j)),
        out_shape=jax.ShapeDtypeStruct((depth, rows, n6), F32),
        compiler_params=_cparams("parallel", "parallel"),
        name="ada_mods",
    )(cs, ada_w, ada_b.reshape(depth, 1, n6))


def _inproj_kernel(*refs, has_prev):
    if has_prev:
        (x_ref, f_ref, modp_ref, mod_ref, nw_ref, win_ref, qkw_ref, cos_ref, sin_ref, bd_ref, dnp_ref,
         qt_ref, k_ref, vt_ref, dqkv_ref, gate_ref, bg_ref, xo_ref) = refs
        x = x_ref[0] + modp_ref[0, 5:6, :] * f_ref[0].astype(F32)
        xo_ref[0] = x
    else:
        (x_ref, mod_ref, nw_ref, win_ref, qkw_ref, cos_ref, sin_ref, bd_ref, dnp_ref,
         qt_ref, k_ref, vt_ref, dqkv_ref, gate_ref, bg_ref) = refs
        x = x_ref[0]
    h = x * lax.rsqrt(jnp.mean(x * x, axis=-1, keepdims=True) + EPS) * nw_ref[...]
    h = h * (1.0 + mod_ref[0, 1:2, :]) + mod_ref[0, 0:1, :]
    acc = jnp.dot(h.astype(BF16), win_ref[...], preferred_element_type=F32)
    qk = acc[:, :QK_WIDTH]
    qn = qk * lax.rsqrt(_group_mean_sq(qk, bd_ref[...], HEAD_DIM) + EPS) * qkw_ref[...]
    lane = lax.broadcasted_iota(jnp.int32, qn.shape, 1)
    partner = jnp.where(lane % (2 * ROPE_NF) < ROPE_NF,
                        pltpu.roll(qn, QK_WIDTH - ROPE_NF, 1), pltpu.roll(qn, ROPE_NF, 1))
    qr = qn * cos_ref[...] + partner * sin_ref[...]
    qt_ref[0] = (qr[:, :ATTN_WIDTH] * (ATTN_SCALE * LOG2_E)).T.astype(BF16)
    k_ref[0] = qr[:, ATTN_WIDTH:].astype(BF16)
    c0 = QK_WIDTH
    vt_ref[0] = acc[:, c0:c0 + KV_WIDTH].T.astype(BF16)
    c0 += KV_WIDTH
    dqkv_ref[0] = acc[:, c0:c0 + 3 * DN_WIDTH].astype(BF16)
    c0 += 3 * DN_WIDTH
    gate_ref[0] = acc[:, c0:c0 + DN_WIDTH].astype(BF16)
    c0 += DN_WIDTH
    z = acc[:, c0:c0 + 4 * DN_HEADS]
    zb = z + dnp_ref[1:2, :]
    softplus = jnp.maximum(zb, 0.0) + jnp.log1p(jnp.exp(-jnp.abs(zb)))
    lane_z = lax.broadcasted_iota(jnp.int32, z.shape, 1)
    bg_ref[0] = jnp.where(lane_z < 2 * DN_HEADS, jax.nn.sigmoid(z), -jnp.exp(dnp_ref[0:1, :]) * softplus)


def _inproj(xs, f_prev, mods_prev, mods, nw, win, qkw, cos_t, sin_t, bd, dnp, *, n_ctx_tiles):
    b, t, d = xs.shape
    tm = TOKEN_TILE
    nt = t // tm
    proj = win.shape[1]
    has_prev = f_prev is not None
    tok = lambda i, j: (j, i, 0)
    modi = lambda i, j: (jnp.where(i < n_ctx_tiles, b, j), 0, 0)
    const = lambda i, j: (0, 0)
    in_specs = [pl.BlockSpec((1, tm, d), tok)]
    args = [xs]
    if has_prev:
        in_specs += [pl.BlockSpec((1, tm, d), tok), pl.BlockSpec((1, 6, d), modi)]
        args += [f_prev, mods_prev]
    in_specs += [pl.BlockSpec((1, 6, d), modi), pl.BlockSpec((1, d), const),
                 pl.BlockSpec((d, proj), const), pl.BlockSpec((1, QK_WIDTH), const),
                 pl.BlockSpec((tm, QK_WIDTH), lambda i, j: (i, 0)),
                 pl.BlockSpec((tm, QK_WIDTH), lambda i, j: (i, 0)),
                 pl.BlockSpec((QK_WIDTH, QK_WIDTH), const), pl.BlockSpec((2, 4 * DN_HEADS), const)]
    args += [mods, nw, win, qkw, cos_t, sin_t, bd, dnp]
    seq_out = lambda w, dt: (pl.BlockSpec((1, tm, w), tok), jax.ShapeDtypeStruct((b, t, w), dt))
    tr_out = lambda w: (pl.BlockSpec((1, w, tm), lambda i, j: (j, 0, i)), jax.ShapeDtypeStruct((b, w, t), BF16))
    outs = [tr_out(ATTN_WIDTH), seq_out(KV_WIDTH, BF16), tr_out(KV_WIDTH),
            seq_out(3 * DN_WIDTH, BF16), seq_out(DN_WIDTH, BF16), seq_out(4 * DN_HEADS, F32)]
    out_specs = [o[0] for o in outs]
    out_shape = [o[1] for o in outs]
    aliases = {}
    if has_prev:
        out_specs.append(pl.BlockSpec((1, tm, d), tok))
        out_shape.append(jax.ShapeDtypeStruct((b, t, d), F32))
        aliases = {0: len(out_shape) - 1}
    return pl.pallas_call(
        functools.partial(_inproj_kernel, has_prev=has_prev),
        grid=(nt, b), in_specs=in_specs, out_specs=out_specs, out_shape=out_shape,
        input_output_aliases=aliases,
        compiler_params=_cparams("parallel", "parallel"),
        name="inproj",
    )(*args)


def _conv_kernel(x_ref, xb_ref, xa_ref, w_ref, bd_ref, dq_ref, dk_ref, dv_ref, *, n_ctx_tiles, n_tiles):
    i = pl.program_id(1)
    tm = x_ref.shape[1]
    first = jnp.logical_or(i == 0, i == n_ctx_tiles)
    last = jnp.logical_or(i == n_ctx_tiles - 1, i == n_tiles - 1)
    before = xb_ref[0].astype(F32)[HALO - 8:] * jnp.where(first, 0.0, 1.0)
    after = xa_ref[0].astype(F32)[:8] * jnp.where(last, 0.0, 1.0)
    ext = jnp.concatenate([before, x_ref[0].astype(F32), after], axis=0)
    n_ext = tm + 16
    y = None
    for j in range(CONV_K):
        shift = (CONV_K // 2 - j) % n_ext
        tap = ext if shift == 0 else pltpu.roll(ext, shift, 0)
        term = tap[8:8 + tm] * w_ref[j:j + 1, :]
        y = term if y is None else y + term
    y = _silu(y)
    q, k, v = y[:, :DN_WIDTH], y[:, DN_WIDTH:2 * DN_WIDTH], y[:, 2 * DN_WIDTH:]
    bd = bd_ref[...]
    dq_ref[0] = (q * lax.rsqrt(_group_mean_sq(q, bd, 1.0) + EPS) * DN_SCALE).astype(BF16)
    dk_ref[0] = (k * lax.rsqrt(_group_mean_sq(k, bd, 1.0) + EPS)).astype(BF16)
    dv_ref[0] = v.astype(BF16)


def _dn_conv(dqkv, conv_w, bd, *, n_ctx_tiles):
    b, t, c = dqkv.shape
    tm = TOKEN_TILE
    nt = t // tm
    r = tm // HALO
    n_halo = t // HALO
    kern = functools.partial(_conv_kernel, n_ctx_tiles=n_ctx_tiles, n_tiles=nt)
    out = jax.ShapeDtypeStruct((b, t, DN_WIDTH), BF16)
    return pl.pallas_call(
        kern, grid=(b, nt),
        in_specs=[pl.BlockSpec((1, tm, c), lambda bi, i: (bi, i, 0)),
                  pl.BlockSpec((1, HALO, c), lambda bi, i: (bi, jnp.maximum(i * r - 1, 0), 0)),
                  pl.BlockSpec((1, HALO, c), lambda bi, i: (bi, jnp.minimum((i + 1) * r, n_halo - 1), 0)),
                  pl.BlockSpec((CONV_K, c), lambda bi, i: (0, 0)),
                  pl.BlockSpec((DN_WIDTH, DN_WIDTH), lambda bi, i: (0, 0))],
        out_specs=[pl.BlockSpec((1, tm, DN_WIDTH), lambda bi, i: (bi, i, 0))] * 3,
        out_shape=[out, out, out],
        compiler_params=_cparams("parallel", "parallel"),
        name="dn_conv",
    )(dqkv, dqkv, dqkv, conv_w, bd)


def _attn_kernel(qt_ref, k_ref, vt_ref, o_ref, acc_sc, sa_sc, sb_sc, *, tk, n_ctx_q, n_ctx_k, n_all_k):
    tq = qt_ref.shape[2]
    n_kv = jnp.where(pl.program_id(1) < n_ctx_q, n_ctx_k, n_all_k)
    heads = range(ATTN_HEADS)
    acc_sc[...] = jnp.zeros(acc_sc.shape, F32)

    def tile_rows(i):
        return pl.ds(pl.multiple_of(i * tk, tk), tk)

    def scores(i, s_sc):
        rows = tile_rows(i)
        for h in heads:
            g = h // GQA_GROUP
            s_sc[h] = jnp.dot(k_ref[0, rows, g * HEAD_DIM:(g + 1) * HEAD_DIM],
                              qt_ref[0, h * HEAD_DIM:(h + 1) * HEAD_DIM, :], preferred_element_type=F32)

    def softmax_pv(i, s_sc, carry):
        rows = tile_rows(i)
        m_prev, l_prev = carry[:ATTN_HEADS], carry[ATTN_HEADS:]
        s = [s_sc[h] for h in heads]
        m_new = [jnp.maximum(m_prev[h], jnp.max(s[h], axis=0, keepdims=True)) for h in heads]
        p = [jnp.exp2(s[h] - m_new[h]) for h in heads]
        alpha = [jnp.exp2(m_prev[h] - m_new[h]) for h in heads]
        ones = jnp.ones((ONES_ROWS, tk), BF16)
        vt1 = [jnp.concatenate([vt_ref[0, g * HEAD_DIM:(g + 1) * HEAD_DIM, rows], ones], axis=0)
               for g in range(KV_HEADS)]
        pv = [jnp.dot(vt1[h // GQA_GROUP], p[h].astype(BF16), preferred_element_type=F32) for h in heads]
        for h in heads:
            acc_sc[h] = alpha[h] * acc_sc[h] + pv[h][:HEAD_DIM]
        l_new = [alpha[h] * l_prev[h] + pv[h][HEAD_DIM:HEAD_DIM + 1] for h in heads]
        return tuple(m_new) + tuple(l_new)

    last = n_kv - 1
    init = (jnp.full((1, tq), -jnp.inf, F32),) * ATTN_HEADS + (jnp.zeros((1, tq), F32),) * ATTN_HEADS
    scores(0, sa_sc)
    scores(jnp.minimum(1, last), sb_sc)
    stats = softmax_pv(0, sa_sc, init)

    def pair(j, carry):
        a = 2 * j + 1
        scores(a + 1, sa_sc)
        carry = softmax_pv(a, sb_sc, carry)
        scores(jnp.minimum(a + 2, last), sb_sc)
        return softmax_pv(a + 1, sa_sc, carry)

    stats = lax.fori_loop(0, last // 2, pair, stats)
    og = jnp.concatenate([acc_sc[h] / stats[ATTN_HEADS + h] for h in heads], axis=0)
    o_ref[0] = og.T.astype(BF16)


def _attention(qt, k, vt, *, n_ctx):
    b, _, t = qt.shape
    tq, tk = TOKEN_TILE, ATTN_KEY_TILE
    assert (n_ctx // tk) % 2 == 1 and (t // tk) % 2 == 1
    kern = functools.partial(_attn_kernel, tk=tk, n_ctx_q=n_ctx // tq, n_ctx_k=n_ctx // tk, n_all_k=t // tk)
    s_buf = pltpu.VMEM((ATTN_HEADS, tk, tq), F32)
    return pl.pallas_call(
        kern, grid=(b, t // tq),
        in_specs=[pl.BlockSpec((1, ATTN_WIDTH, tq), lambda bi, i: (bi, 0, i)),
                  pl.BlockSpec((1, t, KV_WIDTH), lambda bi, i: (bi, 0, 0)),
                  pl.BlockSpec((1, KV_WIDTH, t), lambda bi, i: (bi, 0, 0))],
        out_specs=pl.BlockSpec((1, tq, ATTN_WIDTH), lambda bi, i: (bi, i, 0)),
        out_shape=jax.ShapeDtypeStruct((b, t, ATTN_WIDTH), BF16),
        scratch_shapes=[pltpu.VMEM((ATTN_HEADS, HEAD_DIM, tq), F32), s_buf, s_buf],
        compiler_params=_cparams("parallel", "parallel"),
        name="gqa_attention",
    )(qt, k, vt)


DN_LANES = DN_GROUP * DN_DIM
DN_BLOCK = 4
INV_BLOCK = 16
SCALAR_LANES = 128


def _split3(x):
    x1 = x.astype(BF16)
    r1 = x - x1.astype(F32)
    x2 = r1.astype(BF16)
    x3 = (r1 - x2.astype(F32)).astype(BF16)
    return jnp.concatenate([x1, x2, x3], axis=1)


def _sum3(y):
    w = y.shape[1] // 3
    return y[:, :w] + y[:, w:2 * w] + y[:, 2 * w:]


def _dn_kernel(dq_ref, dk_ref, dv_ref, bg_ref, o_ref, mp_sc, n_sc, r_sc, gam_sc, *, n_chunks, n_ctx_chunks):
    c = CHUNK
    rows_b = DN_BLOCK * c
    ln = DN_LANES
    ri = lax.broadcasted_iota(jnp.int32, (rows_b, ln), 0)
    li = lax.broadcasted_iota(jnp.int32, (rows_b, ln), 1)
    same = (ri // c) == (li // c)
    i_in, j_in = ri % c, li % c
    eye_rc = i_in == j_in
    near = (i_in // INV_BLOCK) == (j_in // INV_BLOCK)
    e_row = lax.broadcasted_iota(jnp.int32, (3 * SCALAR_LANES, 2 * ln), 0) % SCALAR_LANES
    e_blk = lax.broadcasted_iota(jnp.int32, (3 * SCALAR_LANES, 2 * ln), 1) // c
    zl = lax.broadcasted_iota(jnp.int32, (rows_b, SCALAR_LANES), 1)
    nt_dims = (((1,), (1,)), ((), ()))
    tn_dims = (((0,), (0,)), ((), ()))

    def bdiag(x):
        return jnp.where(same, jnp.concatenate([x] * DN_GROUP, axis=0), jnp.zeros((), x.dtype))

    def fold(x):
        x = jnp.where(same, x, 0.0)
        return x[0:c] + x[c:2 * c] + x[2 * c:3 * c] + x[3 * c:4 * c]

    for d in range(2):
        incl = (i_in >= j_in) if d == 0 else (i_in <= j_in)
        strict = (i_in > j_in) if d == 0 else (i_in < j_in)
        last = c - 1 if d == 0 else 0
        tri_bd = jnp.logical_and(same, incl).astype(BF16)
        sel_last = jnp.logical_and(same, j_in == last).astype(BF16)
        same_b = same.astype(BF16)
        col0 = d * 2 * DN_GROUP
        expand = jnp.logical_and(e_row == e_blk + col0, e_blk < 2 * DN_GROUP).astype(BF16)
        is_beta = jnp.logical_and(zl >= col0, zl < col0 + DN_GROUP)

        def prepare(blk, carry, d=d, incl=incl, strict=strict, tri_bd=tri_bd, sel_last=sel_last,
                    same_b=same_b, expand=expand, is_beta=is_beta):
            rs = pl.ds(pl.multiple_of(blk * rows_b, rows_b), rows_b)
            z = bg_ref[0, 0, rs, :]
            cum = _sum3(jnp.dot(tri_bd, _split3(z), preferred_element_type=F32))
            zz = jnp.where(is_beta, z, cum)
            ex = jnp.dot(_split3(zz), expand, preferred_element_type=F32)
            beta, gce = ex[:, :ln], ex[:, ln:]
            g_row = _sum3(jnp.dot(same_b, _split3(jnp.where(eye_rc, gce, 0.0)), preferred_element_type=F32))
            g_last = _sum3(jnp.dot(sel_last, _split3(gce), preferred_element_type=F32))
            decay = jnp.where(incl, jnp.exp(jnp.where(incl, gce - g_row, 0.0)), 0.0)
            e_g = jnp.exp(gce)
            kbf = dk_ref[0, rs, :]
            qbf = dq_ref[0, rs, :]
            k = kbf.astype(F32)
            kb = k * beta
            kbe = (kb * e_g).astype(BF16)
            kbb = kb.astype(BF16)
            vb = (dv_ref[0, rs, :].astype(F32) * beta).astype(BF16)
            q_dec = qbf.astype(F32) * e_g
            k_dec = (k * jnp.exp(g_last - gce)).astype(BF16)
            gamma = jnp.exp(g_last)
            eye_f = eye_rc.astype(F32)
            chunks = range(DN_BLOCK)
            sls = [slice(ch * c, (ch + 1) * c) for ch in chunks]
            mm = lambda a, bmat: jnp.dot(a.astype(BF16), bmat, preferred_element_type=F32)
            sc = [lax.dot_general(jnp.concatenate([kbb[sl], qbf[sl]], axis=0), bdiag(kbf[sl]), nt_dims,
                                  preferred_element_type=F32) for sl in sls]
            intra = [jnp.where(incl[sl], s[c:] * decay[sl], 0.0).astype(BF16) for s, sl in zip(sc, sls)]
            lfull = [jnp.where(strict[sl], s[:c] * decay[sl], 0.0) for s, sl in zip(sc, sls)]
            p = [jnp.where(near[sl], -lf, 0.0) for lf, sl in zip(lfull, sls)]
            l_off = [jnp.where(near[sl], 0.0, lf).astype(BF16) for lf, sl in zip(lfull, sls)]
            tmat = [eye_f[sl] + pi for pi, sl in zip(p, sls)]
            p = [mm(pi, bdiag(pi.astype(BF16))) for pi in p]
            for _ in range(int(math.log2(INV_BLOCK)) - 2):
                res = [mm(jnp.concatenate([ti, pi], axis=0), bdiag(pi.astype(BF16))) for ti, pi in zip(tmat, p)]
                tmat = [ti + ri[:c] for ti, ri in zip(tmat, res)]
                p = [ri[c:] for ri in res]
            tmat = [ti + mm(ti, bdiag(pi.astype(BF16))) for ti, pi in zip(tmat, p)]
            nmat = [mm(ti, bdiag(lo)) for ti, lo in zip(tmat, l_off)]
            n2 = [mm(ni, bdiag(ni.astype(BF16))) for ni in nmat]
            tmat = [ti + mm(qi, bdiag(ti.astype(BF16))) for ti, qi in zip(tmat, n2)]
            tmat = [ti - mm(ni, bdiag(ti.astype(BF16))) for ti, ni in zip(tmat, nmat)]
            uw = [mm(ti, jnp.concatenate([bdiag(vb[sl]), bdiag(kbe[sl])], axis=1)).astype(BF16)
                  for ti, sl in zip(tmat, sls)]
            aw_au = [mm(ai, jnp.concatenate([bdiag(x[:, ln:]), bdiag(x[:, :ln])], axis=1))
                     for ai, x in zip(intra, uw)]
            mn = [lax.dot_general(k_dec[sl], jnp.concatenate([x[:, ln:], x[:, :ln]], axis=1), tn_dims,
                                  preferred_element_type=F32) for x, sl in zip(uw, sls)]
            for ch in chunks:
                ci = blk * DN_BLOCK + ch
                p_c = q_dec[sls[ch]] - aw_au[ch][:, :ln]
                mp_sc[ci] = jnp.concatenate([fold(mn[ch][:, :ln]), p_c], axis=0).astype(BF16)
                n_sc[ci] = fold(mn[ch][:, ln:])
                r_sc[ci] = aw_au[ch][:, ln:]
                gam_sc[ci] = gamma[ch * c:ch * c + 8]
            return carry

        lax.fori_loop(0, n_chunks // DN_BLOCK, prepare, 0)

        def scan(s, state, d=d):
            if d == 0:
                ci = s
            else:
                ci = jnp.where(s < n_ctx_chunks, n_ctx_chunks - 1 - s, n_chunks - 1 + n_ctx_chunks - s)
            res = jnp.dot(mp_sc[ci], bdiag(state.astype(BF16)), preferred_element_type=F32)
            rows = pl.ds(pl.multiple_of(ci * c, c), c)
            o = res[c:] + r_sc[ci]
            if d == 0:
                o_ref[0, rows, :] = o
            else:
                o_ref[0, rows, :] += o
            return gam_sc[ci][0:1] * state - res[:c] + n_sc[ci]

        lax.fori_loop(0, n_chunks, scan, jnp.zeros((c, ln), F32))


def _deltanet(dq, dk, dv, bg, *, n_ctx):
    b, t, _ = dq.shape
    n_chunks = t // CHUNK
    assert n_chunks % DN_BLOCK == 0 and DN_GROUP * CHUNK == DN_LANES
    ng = DN_HEADS // DN_GROUP
    bgh = bg.reshape(b, t, 2, 2, ng, DN_GROUP).transpose(0, 4, 1, 3, 2, 5).reshape(b, ng, t, 4 * DN_GROUP)
    bgh = jnp.pad(bgh, ((0, 0), (0, 0), (0, 0), (0, SCALAR_LANES - 4 * DN_GROUP)))
    seq = pl.BlockSpec((1, t, DN_LANES), lambda bi, gi: (bi, 0, gi))
    kern = functools.partial(_dn_kernel, n_chunks=n_chunks, n_ctx_chunks=n_ctx // CHUNK)
    return pl.pallas_call(
        kern, grid=(b, ng),
        in_specs=[seq, seq, seq, pl.BlockSpec((1, 1, t, SCALAR_LANES), lambda bi, gi: (bi, gi, 0, 0))],
        out_specs=seq,
        out_shape=jax.ShapeDtypeStruct((b, t, DN_WIDTH), F32),
        scratch_shapes=[pltpu.VMEM((n_chunks, 2 * CHUNK, DN_LANES), BF16),
                        pltpu.VMEM((n_chunks, CHUNK, DN_LANES), F32),
                        pltpu.VMEM((n_chunks, CHUNK, DN_LANES), F32),
                        pltpu.VMEM((n_chunks, 8, DN_LANES), F32)],
        compiler_params=_cparams("parallel", "parallel"),
        name="gated_deltanet",
    )(dq, dk, dv, bgh)


def _outproj_kernel(a_ref, d_ref, gate_ref, x_ref, mod_ref, dnw_ref, bd_ref, woa_ref, wod_ref, nfw_ref,
                    wr_ref, br_ref, xo_ref, hf_ref, route_ref, grp_ref, cnt_ref):
    dd = d_ref[0]
    gate = gate_ref[0].astype(F32)
    dn = dd * lax.rsqrt(_group_mean_sq(dd, bd_ref[...], DN_DIM) + EPS) * dnw_ref[...] * _silu(gate)
    y = (jnp.dot(a_ref[0], woa_ref[...], preferred_element_type=F32)
         + jnp.dot(dn.astype(BF16), wod_ref[...], preferred_element_type=F32))
    x = x_ref[0] + mod_ref[0, 2:3, :] * y
    xo_ref[0] = x
    h = x * lax.rsqrt(jnp.mean(x * x, axis=-1, keepdims=True) + EPS) * nfw_ref[...]
    h = h * (1.0 + mod_ref[0, 4:5, :]) + mod_ref[0, 3:4, :]
    hb = h.astype(BF16)
    hf_ref[0] = hb
    logits = jnp.dot(hb, wr_ref[...], preferred_element_type=F32) + br_ref[...]
    lane = lax.broadcasted_iota(jnp.int32, logits.shape, 1)
    neg = jnp.float32(-jnp.inf)
    big = jnp.int32(ROUTE_LANES)

    def first_argmax(vals, vmax):
        return jnp.min(jnp.where(vals == vmax, lane, big), axis=-1, keepdims=True)

    is_g = jnp.logical_and(lane >= N_EXPERTS, lane < N_EXPERTS + N_GROUPS)
    gl = jnp.where(is_g, logits, neg)
    g_max = jnp.max(gl, axis=-1, keepdims=True)
    g_sel = first_argmax(gl, g_max) - N_EXPERTS
    p_g = 1.0 / jnp.sum(jnp.exp(gl - g_max), axis=-1, keepdims=True)
    e_lo = g_sel * EXPERTS_PER_GROUP
    in_grp = jnp.logical_and(lane >= e_lo, lane < e_lo + EXPERTS_PER_GROUP)
    el = jnp.where(in_grp, logits, neg)
    e_max = jnp.max(el, axis=-1, keepdims=True)
    i1 = first_argmax(el, e_max)
    el2 = jnp.where(lane == i1, neg, el)
    e_max2 = jnp.max(el2, axis=-1, keepdims=True)
    i2 = first_argmax(el2, e_max2)
    p2 = jnp.exp(e_max2 - e_max)
    w1 = p_g / (1.0 + p2)
    w2 = p_g * p2 / (1.0 + p2)
    route_ref[0] = jnp.where(lane == i1, w1, jnp.where(lane == i2, w2, 0.0)).astype(BF16)
    g_lanes = jnp.broadcast_to(g_sel.astype(F32), logits.shape)
    grp_ref[...] = g_lanes.T[:8]
    cnt_ref[0, 0] = jnp.broadcast_to(
        jnp.sum(jnp.where(lane == g_sel, 1.0, 0.0), axis=0, keepdims=True), (8, ROUTE_LANES))


def _outproj(a, dsum, gate, xs, mods, dnw, bd, woa, wod, nfw, wr, br, *, n_ctx_tiles):
    b, t, d = xs.shape
    tm = TOKEN_TILE
    tok = lambda i, j: (j, i, 0)
    const = lambda i, j: (0, 0)
    modi = lambda i, j: (jnp.where(i < n_ctx_tiles, b, j), 0, 0)
    return pl.pallas_call(
        _outproj_kernel, grid=(t // tm, b),
        in_specs=[pl.BlockSpec((1, tm, ATTN_WIDTH), tok), pl.BlockSpec((1, tm, DN_WIDTH), tok),
                  pl.BlockSpec((1, tm, DN_WIDTH), tok), pl.BlockSpec((1, tm, d), tok),
                  pl.BlockSpec((1, 6, d), modi), pl.BlockSpec((1, DN_WIDTH), const),
                  pl.BlockSpec((DN_WIDTH, DN_WIDTH), const), pl.BlockSpec((ATTN_WIDTH, d), const),
                  pl.BlockSpec((DN_WIDTH, d), const), pl.BlockSpec((1, d), const),
                  pl.BlockSpec((d, ROUTE_LANES), const), pl.BlockSpec((1, ROUTE_LANES), const)],
        out_specs=[pl.BlockSpec((1, tm, d), tok), pl.BlockSpec((1, tm, d), tok),
                   pl.BlockSpec((1, tm, ROUTE_LANES), tok),
                   pl.BlockSpec((8, tm), lambda i, j: (0, j * (t // tm) + i)),
                   pl.BlockSpec((1, 1, 8, ROUTE_LANES), lambda i, j: (j, i, 0, 0))],
        out_shape=[jax.ShapeDtypeStruct((b, t, d), F32), jax.ShapeDtypeStruct((b, t, d), BF16),
                   jax.ShapeDtypeStruct((b, t, ROUTE_LANES), BF16),
                   jax.ShapeDtypeStruct((8, b * t), F32),
                   jax.ShapeDtypeStruct((b, t // tm, 8, ROUTE_LANES), F32)],
        input_output_aliases={3: 0},
        compiler_params=_cparams("parallel", "parallel"),
        name="outproj_router",
    )(a, dsum, gate, xs, mods, dnw, bd, woa, wod, nfw, wr, br)


def _moe_kernel(cnt_ref, h_ref, r_ref, grp_ref, tri_ref, w1_ref, w3_ref, w2_ref, f_ref, acc_sc):
    m, g = pl.program_id(0), pl.program_id(1)
    tm = h_ref.shape[0]
    ff = w1_ref.shape[2]

    @pl.when(g == 0)
    def _():
        acc_sc[...] = jnp.zeros(acc_sc.shape, F32)

    member = grp_ref[0:1, :] == g.astype(F32)
    prefix = jnp.dot(jnp.broadcast_to(member.astype(BF16), (16, tm)), tri_ref[...],
                     preferred_element_type=F32)[0:1]
    pos = jnp.where(member, prefix - 1.0, -1.0)
    e_row = lax.broadcasted_iota(jnp.int32, (ROUTE_LANES, EXPERTS_PER_GROUP * ff), 0)
    e_col = lax.broadcasted_iota(jnp.int32, (ROUTE_LANES, EXPERTS_PER_GROUP * ff), 1) // ff
    expand = (e_row == g * EXPERTS_PER_GROUP + e_col).astype(BF16)

    def sub_block(first_slot, rows):
        slot = lax.broadcasted_iota(jnp.int32, (rows, tm), 0) + first_slot
        sel = (pos == slot.astype(F32)).astype(BF16)
        xs = jnp.dot(sel, h_ref[...], preferred_element_type=F32).astype(BF16)
        wt = jnp.dot(sel, r_ref[...], preferred_element_type=F32).astype(BF16)
        wexp = jnp.dot(wt, expand, preferred_element_type=F32)
        y = None
        for e in range(EXPERTS_PER_GROUP):
            a = jnp.dot(xs, w1_ref[e], preferred_element_type=F32)
            gate = jnp.dot(xs, w3_ref[e], preferred_element_type=F32)
            mid = (_silu(a) * gate * wexp[:, e * ff:(e + 1) * ff]).astype(BF16)
            ye = jnp.dot(mid, w2_ref[e], preferred_element_type=F32)
            y = ye if y is None else y + ye
        acc_sc[...] += lax.dot_general(sel, y.astype(BF16), (((0,), (0,)), ((), ())),
                                       preferred_element_type=F32)

    count = cnt_ref[m * N_GROUPS + g]
    n_full = count // MOE_SUB
    rem = count - n_full * MOE_SUB

    def full_block(sb, carry):
        sub_block(sb * MOE_SUB, MOE_SUB)
        return carry

    lax.fori_loop(0, n_full, full_block, 0)
    lo = 0
    for rows in MOE_TAILS:
        @pl.when(jnp.logical_and(rem > lo, rem <= rows))
        def _(rows=rows):
            sub_block(n_full * MOE_SUB, rows)
        lo = rows

    @pl.when(g == N_GROUPS - 1)
    def _():
        f_ref[...] = acc_sc[...].astype(f_ref.dtype)


def _moe(hf, route, grp, counts, tri, w1, w3, w2):
    n, d = hf.shape
    _, _, ff = w1.shape
    tm = tri.shape[0]
    epg = EXPERTS_PER_GROUP
    grid_spec = pltpu.PrefetchScalarGridSpec(
        num_scalar_prefetch=1, grid=(n // tm, N_GROUPS),
        in_specs=[pl.BlockSpec((tm, d), lambda i, g, c: (i, 0)),
                  pl.BlockSpec((tm, ROUTE_LANES), lambda i, g, c: (i, 0)),
                  pl.BlockSpec((8, tm), lambda i, g, c: (0, i)),
                  pl.BlockSpec((tm, tm), lambda i, g, c: (0, 0)),
                  pl.BlockSpec((epg, d, ff), lambda i, g, c: (g, 0, 0)),
                  pl.BlockSpec((epg, d, ff), lambda i, g, c: (g, 0, 0)),
                  pl.BlockSpec((epg, ff, d), lambda i, g, c: (g, 0, 0))],
        out_specs=pl.BlockSpec((tm, d), lambda i, g, c: (i, 0)),
        scratch_shapes=[pltpu.VMEM((tm, d), F32)])
    return pl.pallas_call(
        _moe_kernel, grid_spec=grid_spec,
        out_shape=jax.ShapeDtypeStruct((n, d), BF16),
        compiler_params=_cparams("parallel", "arbitrary"),
        name="moe_experts",
    )(counts, hf, route, grp, tri, w1, w3, w2)


def _final_kernel(x_ref, f_ref, mod_ref, w_ref, o_ref):
    x = x_ref[0] + mod_ref[0, 5:6, :] * f_ref[0].astype(F32)
    o_ref[0] = x * lax.rsqrt(jnp.mean(x * x, axis=-1, keepdims=True) + EPS) * w_ref[...]


def _final_norm(xs, f_prev, mods, w, *, n_ctx_tiles):
    b, t, d = xs.shape
    tm = TOKEN_TILE
    n_lat = t // tm - n_ctx_tiles
    lat = lambda i, j: (j, i + n_ctx_tiles, 0)
    return pl.pallas_call(
        _final_kernel, grid=(n_lat, b),
        in_specs=[pl.BlockSpec((1, tm, d), lat), pl.BlockSpec((1, tm, d), lat),
                  pl.BlockSpec((1, 6, d), lambda i, j: (j, 0, 0)), pl.BlockSpec((1, d), lambda i, j: (0, 0))],
        out_specs=pl.BlockSpec((1, tm, d), lambda i, j: (j, i, 0)),
        out_shape=jax.ShapeDtypeStruct((b, n_lat * tm, d), F32),
        compiler_params=_cparams("parallel", "parallel"),
        name="final_norm",
    )(xs, f_prev, mods, w)


def _rope_tables(n_ctx, n_lat):
    pos = jnp.arange(n_lat, dtype=jnp.int32)
    inv = ROPE_THETA ** (-jnp.arange(ROPE_NF, dtype=F32) / ROPE_NF)
    ang_r = (pos // GRID_W).astype(F32)[:, None] * inv
    ang_c = (pos % GRID_W).astype(F32)[:, None] * inv
    cos = jnp.concatenate([jnp.cos(ang_r)] * 2 + [jnp.cos(ang_c)] * 2, axis=-1)
    sin = jnp.concatenate([-jnp.sin(ang_r), jnp.sin(ang_r), -jnp.sin(ang_c), jnp.sin(ang_c)], axis=-1)
    cos = jnp.concatenate([jnp.ones((n_ctx, HEAD_DIM), F32), cos], axis=0)
    sin = jnp.concatenate([jnp.zeros((n_ctx, HEAD_DIM), F32), sin], axis=0)
    reps = QK_WIDTH // HEAD_DIM
    return jnp.tile(cos, (1, reps)), jnp.tile(sin, (1, reps))


def kernel(x, c, ctx, c_ctx, ada_w, ada_b, norm_mix_w, norm_ffn_w, w_in, q_norm_w, k_norm_w, conv_w,
           dn_A_log, dn_dt_bias, dn_norm_w, w_out, rg_w, rg_b, re_w, re_b, w1, w3, w2, final_norm_w):
    b, s, d = x.shape
    n_ctx = ctx.shape[1]
    depth = w_in.shape[0]
    t = n_ctx + s
    assert n_ctx % TOKEN_TILE == 0 and s % TOKEN_TILE == 0 and s % GRID_W == 0
    n_ctx_tiles = n_ctx // TOKEN_TILE

    xs = jnp.concatenate([ctx, x], axis=1)
    mod_rows = -(-(b + 1) // 8) * 8
    cs = jnp.zeros((mod_rows, d), F32).at[:b].set(c).at[b].set(c_ctx)
    mods = _ada_mods(cs, ada_w, ada_b).reshape(depth, mod_rows, 6, d)

    cos_t, sin_t = _rope_tables(n_ctx, s)
    bd_qk = _block_ones(QK_WIDTH, HEAD_DIM)
    bd_dn = _block_ones(DN_WIDTH, DN_DIM)
    zeros16 = jnp.zeros((2 * DN_HEADS,), F32)
    moe_tile = next(c for c in MOE_TILES if (b * t) % c == 0)
    tri = (lax.broadcasted_iota(jnp.int32, (moe_tile, moe_tile), 0)
           <= lax.broadcasted_iota(jnp.int32, (moe_tile, moe_tile), 1)).astype(BF16)

    f_prev = None
    for l in range(depth):
        qkw = jnp.concatenate([jnp.tile(q_norm_w[l], ATTN_HEADS), jnp.tile(k_norm_w[l], KV_HEADS)])[None]
        dnp = jnp.stack([jnp.concatenate([zeros16, dn_A_log[l].reshape(-1)]),
                         jnp.concatenate([zeros16, dn_dt_bias[l].reshape(-1)])])
        outs = _inproj(xs, f_prev, mods[l - 1] if l else None, mods[l], norm_mix_w[l][None],
                       w_in[l].astype(BF16), qkw, cos_t, sin_t, bd_qk, dnp, n_ctx_tiles=n_ctx_tiles)
        qt, k, vt, dqkv, gate, bg = outs[:6]
        if l:
            xs = outs[6]
        dq, dk, dv = _dn_conv(dqkv, conv_w[l], bd_dn, n_ctx_tiles=n_ctx_tiles)
        a = _attention(qt, k, vt, n_ctx=n_ctx)
        dsum = _deltanet(dq, dk, dv, bg, n_ctx=n_ctx)
        wo = w_out[l].astype(BF16)
        wr = jnp.zeros((d, ROUTE_LANES), F32).at[:, :N_EXPERTS].set(re_w[l]).at[
            :, N_EXPERTS:N_EXPERTS + N_GROUPS].set(rg_w[l]).astype(BF16)
        br = jnp.zeros((1, ROUTE_LANES), F32).at[0, :N_EXPERTS].set(re_b[l]).at[
            0, N_EXPERTS:N_EXPERTS + N_GROUPS].set(rg_b[l])
        xs, hf, route, grp, cnt = _outproj(
            a, dsum, gate, xs, mods[l], jnp.tile(dn_norm_w[l], DN_HEADS)[None], bd_dn,
            wo[:ATTN_WIDTH], wo[ATTN_WIDTH:], norm_ffn_w[l][None], wr, br, n_ctx_tiles=n_ctx_tiles)
        counts = cnt[:, :, 0, :N_GROUPS].reshape(-1, moe_tile // TOKEN_TILE, N_GROUPS).sum(axis=1)
        f_prev = _moe(hf.reshape(b * t, d), route.reshape(b * t, ROUTE_LANES), grp,
                      counts.astype(jnp.int32).reshape(-1), tri,
                      w1[l].astype(BF16), w3[l].astype(BF16), w2[l].astype(BF16)).reshape(b, t, d)
    return _final_norm(xs, f_prev, mods[depth - 1], final_norm_w[None], n_ctx_tiles=n_ctx_tiles)
```

```python
import functools
import math

import jax
import jax.numpy as jnp
from jax import lax
from jax.experimental import pallas as pl
from jax.experimental.pallas import tpu as pltpu

F32 = jnp.float32
BF16 = jnp.bfloat16
HIGHEST = lax.Precision.HIGHEST

GRID_W = 64
HEAD_DIM = 64
ATTN_HEADS = 8
KV_HEADS = 2
GQA_GROUP = ATTN_HEADS // KV_HEADS
ATTN_WIDTH = ATTN_HEADS * HEAD_DIM
KV_WIDTH = KV_HEADS * HEAD_DIM
ATTN_SCALE = HEAD_DIM ** -0.5
LOG2_E = math.log2(math.e)
ROPE_THETA = 10000.0
ROPE_NF = HEAD_DIM // 4
DN_HEADS = 8
DN_DIM = 64
DN_WIDTH = DN_HEADS * DN_DIM
DN_SCALE = DN_DIM ** -0.5
DN_GROUP = 4
CONV_K = 5
CHUNK = 64
N_GROUPS = 4
EXPERTS_PER_GROUP = 8
N_EXPERTS = N_GROUPS * EXPERTS_PER_GROUP
EPS = 1e-6
QK_WIDTH = ATTN_WIDTH + KV_WIDTH
TOKEN_TILE = 256
ATTN_KEY_TILE = 256
ONES_ROWS = 16
HALO = 16
ROUTE_LANES = 128
MOE_TILES = (1024, 512, 256)
MOE_SUB = 256
MOE_TAILS = (64, 128, 256)
VMEM_LIMIT = 56 * 1024 * 1024


def _cparams(*sem):
    return pltpu.CompilerParams(dimension_semantics=sem, vmem_limit_bytes=VMEM_LIMIT)


def _silu(x):
    return x * jax.nn.sigmoid(x)


def _block_ones(n, blk):
    i = lax.broadcasted_iota(jnp.int32, (n, n), 0) // blk
    j = lax.broadcasted_iota(jnp.int32, (n, n), 1) // blk
    return (i == j).astype(BF16)


def _group_mean_sq(x, ones_bd, width):
    return jnp.dot((x * x).astype(BF16), ones_bd, preferred_element_type=F32) * (1.0 / width)


def _ada_kernel(cs_ref, w_ref, b_ref, o_ref):
    o_ref[0] = jnp.dot(_silu(cs_ref[...]), w_ref[0], preferred_element_type=F32,
                       precision=HIGHEST) + b_ref[0]


def _ada_mods(cs, ada_w, ada_b):
    depth, d, n6 = ada_w.shape
    rows = cs.shape[0]
    tn = 1536
    return pl.pallas_call(
        _ada_kernel,
        grid=(depth, n6 // tn),
        in_specs=[pl.BlockSpec((rows, d), lambda l, j: (0, 0)),
                  pl.BlockSpec((1, d, tn), lambda l, j: (l, 0, j)),
                  pl.BlockSpec((1, 1, tn), lambda l, j: (l, 0, j))],
        out_specs=pl.BlockSpec((1, rows, tn), lambda l, j: (l, 0, j)),
        out_shape=jax.ShapeDtypeStruct((depth, rows, n6), F32),
        compiler_params=_cparams("parallel", "parallel"),
        name="ada_mods",
    )(cs, ada_w, ada_b.reshape(depth, 1, n6))


def _inproj_kernel(*refs, has_prev):
    if has_prev:
        (x_ref, f_ref, modp_ref, mod_ref, nw_ref, win_ref, qkw_ref, cos_ref, sin_ref, bd_ref, dnp_ref,
         qt_ref, k_ref, vt_ref, dqkv_ref, gate_ref, bg_ref, xo_ref) = refs
        x = x_ref[0] + modp_ref[0, 5:6, :] * f_ref[0].astype(F32)
        xo_ref[0] = x
    else:
        (x_ref, mod_ref, nw_ref, win_ref, qkw_ref, cos_ref, sin_ref, bd_ref, dnp_ref,
         qt_ref, k_ref, vt_ref, dqkv_ref, gate_ref, bg_ref) = refs
        x = x_ref[0]
    h = x * lax.rsqrt(jnp.mean(x * x, axis=-1, keepdims=True) + EPS) * nw_ref[...]
    h = h * (1.0 + mod_ref[0, 1:2, :]) + mod_ref[0, 0:1, :]
    acc = jnp.dot(h.astype(BF16), win_ref[...], preferred_element_type=F32)
    qk = acc[:, :QK_WIDTH]
    qn = qk * lax.rsqrt(_group_mean_sq(qk, bd_ref[...], HEAD_DIM) + EPS) * qkw_ref[...]
    lane = lax.broadcasted_iota(jnp.int32, qn.shape, 1)
    partner = jnp.where(lane % (2 * ROPE_NF) < ROPE_NF,
                        pltpu.roll(qn, QK_WIDTH - ROPE_NF, 1), pltpu.roll(qn, ROPE_NF, 1))
    qr = qn * cos_ref[...] + partner * sin_ref[...]
    qt_ref[0] = (qr[:, :ATTN_WIDTH] * (ATTN_SCALE * LOG2_E)).T.astype(BF16)
    k_ref[0] = qr[:, ATTN_WIDTH:].astype(BF16)
    c0 = QK_WIDTH
    vt_ref[0] = acc[:, c0:c0 + KV_WIDTH].T.astype(BF16)
    c0 += KV_WIDTH
    dqkv_ref[0] = acc[:, c0:c0 + 3 * DN_WIDTH].astype(BF16)
    c0 += 3 * DN_WIDTH
    gate_ref[0] = acc[:, c0:c0 + DN_WIDTH].astype(BF16)
    c0 += DN_WIDTH
    z = acc[:, c0:c0 + 4 * DN_HEADS]
    zb = z + dnp_ref[1:2, :]
    softplus = jnp.maximum(zb, 0.0) + jnp.log1p(jnp.exp(-jnp.abs(zb)))
    lane_z = lax.broadcasted_iota(jnp.int32, z.shape, 1)
    bg_ref[0] = jnp.where(lane_z < 2 * DN_HEADS, jax.nn.sigmoid(z), -jnp.exp(dnp_ref[0:1, :]) * softplus)


def _inproj(xs, f_prev, mods_prev, mods, nw, win, qkw, cos_t, sin_t, bd, dnp, *, n_ctx_tiles):
    b, t, d = xs.shape
    tm = TOKEN_TILE
    nt = t // tm
    proj = win.shape[1]
    has_prev = f_prev is not None
    tok = lambda i, j: (j, i, 0)
    modi = lambda i, j: (jnp.where(i < n_ctx_tiles, b, j), 0, 0)
    const = lambda i, j: (0, 0)
    in_specs = [pl.BlockSpec((1, tm, d), tok)]
    args = [xs]
    if has_prev:
        in_specs += [pl.BlockSpec((1, tm, d), tok), pl.BlockSpec((1, 6, d), modi)]
        args += [f_prev, mods_prev]
    in_specs += [pl.BlockSpec((1, 6, d), modi), pl.BlockSpec((1, d), const),
                 pl.BlockSpec((d, proj), const), pl.BlockSpec((1, QK_WIDTH), const),
                 pl.BlockSpec((tm, QK_WIDTH), lambda i, j: (i, 0)),
                 pl.BlockSpec((tm, QK_WIDTH), lambda i, j: (i, 0)),
                 pl.BlockSpec((QK_WIDTH, QK_WIDTH), const), pl.BlockSpec((2, 4 * DN_HEADS), const)]
    args += [mods, nw, win, qkw, cos_t, sin_t, bd, dnp]
    seq_out = lambda w, dt: (pl.BlockSpec((1, tm, w), tok), jax.ShapeDtypeStruct((b, t, w), dt))
    tr_out = lambda w: (pl.BlockSpec((1, w, tm), lambda i, j: (j, 0, i)), jax.ShapeDtypeStruct((b, w, t), BF16))
    outs = [tr_out(ATTN_WIDTH), seq_out(KV_WIDTH, BF16), tr_out(KV_WIDTH),
            seq_out(3 * DN_WIDTH, BF16), seq_out(DN_WIDTH, BF16), seq_out(4 * DN_HEADS, F32)]
    out_specs = [o[0] for o in outs]
    out_shape = [o[1] for o in outs]
    aliases = {}
    if has_prev:
        out_specs.append(pl.BlockSpec((1, tm, d), tok))
        out_shape.append(jax.ShapeDtypeStruct((b, t, d), F32))
        aliases = {0: len(out_shape) - 1}
    return pl.pallas_call(
        functools.partial(_inproj_kernel, has_prev=has_prev),
        grid=(nt, b), in_specs=in_specs, out_specs=out_specs, out_shape=out_shape,
        input_output_aliases=aliases,
        compiler_params=_cparams("parallel", "parallel"),
        name="inproj",
    )(*args)


def _conv_kernel(x_ref, xb_ref, xa_ref, w_ref, bd_ref, dq_ref, dk_ref, dv_ref, *, n_ctx_tiles, n_tiles):
    i = pl.program_id(1)
    tm = x_ref.shape[1]
    first = jnp.logical_or(i == 0, i == n_ctx_tiles)
    last = jnp.logical_or(i == n_ctx_tiles - 1, i == n_tiles - 1)
    before = xb_ref[0].astype(F32)[HALO - 8:] * jnp.where(first, 0.0, 1.0)
    after = xa_ref[0].astype(F32)[:8] * jnp.where(last, 0.0, 1.0)
    ext = jnp.concatenate([before, x_ref[0].astype(F32), after], axis=0)
    n_ext = tm + 16
    y = None
    for j in range(CONV_K):
        shift = (CONV_K // 2 - j) % n_ext
        tap = ext if shift == 0 else pltpu.roll(ext, shift, 0)
        term = tap[8:8 + tm] * w_ref[j:j + 1, :]
        y = term if y is None else y + term
    y = _silu(y)
    q, k, v = y[:, :DN_WIDTH], y[:, DN_WIDTH:2 * DN_WIDTH], y[:, 2 * DN_WIDTH:]
    bd = bd_ref[...]
    dq_ref[0] = (q * lax.rsqrt(_group_mean_sq(q, bd, 1.0) + EPS) * DN_SCALE).astype(BF16)
    dk_ref[0] = (k * lax.rsqrt(_group_mean_sq(k, bd, 1.0) + EPS)).astype(BF16)
    dv_ref[0] = v.astype(BF16)


def _dn_conv(dqkv, conv_w, bd, *, n_ctx_tiles):
    b, t, c = dqkv.shape
    tm = TOKEN_TILE
    nt = t // tm
    r = tm // HALO
    n_halo = t // HALO
    kern = functools.partial(_conv_kernel, n_ctx_tiles=n_ctx_tiles, n_tiles=nt)
    out = jax.ShapeDtypeStruct((b, t, DN_WIDTH), BF16)
    return pl.pallas_call(
        kern, grid=(b, nt),
        in_specs=[pl.BlockSpec((1, tm, c), lambda bi, i: (bi, i, 0)),
                  pl.BlockSpec((1, HALO, c), lambda bi, i: (bi, jnp.maximum(i * r - 1, 0), 0)),
                  pl.BlockSpec((1, HALO, c), lambda bi, i: (bi, jnp.minimum((i + 1) * r, n_halo - 1), 0)),
                  pl.BlockSpec((CONV_K, c), lambda bi, i: (0, 0)),
                  pl.BlockSpec((DN_WIDTH, DN_WIDTH), lambda bi, i: (0, 0))],
        out_specs=[pl.BlockSpec((1, tm, DN_WIDTH), lambda bi, i: (bi, i, 0))] * 3,
        out_shape=[out, out, out],
        compiler_params=_cparams("parallel", "parallel"),
        name="dn_conv",
    )(dqkv, dqkv, dqkv, conv_w, bd)


def _attn_kernel(qt_ref, k_ref, vt_ref, o_ref, acc_sc, sa_sc, sb_sc, *, tk, n_ctx_q, n_ctx_k, n_all_k):
    tq = qt_ref.shape[2]
    n_kv = jnp.where(pl.program_id(1) < n_ctx_q, n_ctx_k, n_all_k)
    heads = range(ATTN_HEADS)
    acc_sc[...] = jnp.zeros(acc_sc.shape, F32)

    def tile_rows(i):
        return pl.ds(pl.multiple_of(i * tk, tk), tk)

    def scores(i, s_sc):
        rows = tile_rows(i)
        for h in heads:
            g = h // GQA_GROUP
            s_sc[h] = jnp.dot(k_ref[0, rows, g * HEAD_DIM:(g + 1) * HEAD_DIM],
                              qt_ref[0, h * HEAD_DIM:(h + 1) * HEAD_DIM, :], preferred_element_type=F32)

    def softmax_pv(i, s_sc, carry):
        rows = tile_rows(i)
        m_prev, l_prev = carry[:ATTN_HEADS], carry[ATTN_HEADS:]
        s = [s_sc[h] for h in heads]
        m_new = [jnp.maximum(m_prev[h], jnp.max(s[h], axis=0, keepdims=True)) for h in heads]
        p = [jnp.exp2(s[h] - m_new[h]) for h in heads]
        alpha = [jnp.exp2(m_prev[h] - m_new[h]) for h in heads]
        ones = jnp.ones((ONES_ROWS, tk), BF16)
        vt1 = [jnp.concatenate([vt_ref[0, g * HEAD_DIM:(g + 1) * HEAD_DIM, rows], ones], axis=0)
               for g in range(KV_HEADS)]
        pv = [jnp.dot(vt1[h // GQA_GROUP], p[h].astype(BF16), preferred_element_type=F32) for h in heads]
        for h in heads:
            acc_sc[h] = alpha[h] * acc_sc[h] + pv[h][:HEAD_DIM]
        l_new = [alpha[h] * l_prev[h] + pv[h][HEAD_DIM:HEAD_DIM + 1] for h in heads]
        return tuple(m_new) + tuple(l_new)

    last = n_kv - 1
    init = (jnp.full((1, tq), -jnp.inf, F32),) * ATTN_HEADS + (jnp.zeros((1, tq), F32),) * ATTN_HEADS
    scores(0, sa_sc)
    scores(jnp.minimum(1, last), sb_sc)
    stats = softmax_pv(0, sa_sc, init)

    def pair(j, carry):
        a = 2 * j + 1
        scores(a + 1, sa_sc)
        carry = softmax_pv(a, sb_sc, carry)
        scores(jnp.minimum(a + 2, last), sb_sc)
        return softmax_pv(a + 1, sa_sc, carry)

    stats = lax.fori_loop(0, last // 2, pair, stats)
    og = jnp.concatenate([acc_sc[h] / stats[ATTN_HEADS + h] for h in heads], axis=0)
    o_ref[0] = og.T.astype(BF16)


def _attention(qt, k, vt, *, n_ctx):
    b, _, t = qt.shape
    tq, tk = TOKEN_TILE, ATTN_KEY_TILE
    assert (n_ctx // tk) % 2 == 1 and (t // tk) % 2 == 1
    kern = functools.partial(_attn_kernel, tk=tk, n_ctx_q=n_ctx // tq, n_ctx_k=n_ctx // tk, n_all_k=t // tk)
    s_buf = pltpu.VMEM((ATTN_HEADS, tk, tq), F32)
    return pl.pallas_call(
        kern, grid=(b, t // tq),
        in_specs=[pl.BlockSpec((1, ATTN_WIDTH, tq), lambda bi, i: (bi, 0, i)),
                  pl.BlockSpec((1, t, KV_WIDTH), lambda bi, i: (bi, 0, 0)),
                  pl.BlockSpec((1, KV_WIDTH, t), lambda bi, i: (bi, 0, 0))],
        out_specs=pl.BlockSpec((1, tq, ATTN_WIDTH), lambda bi, i: (bi, i, 0)),
        out_shape=jax.ShapeDtypeStruct((b, t, ATTN_WIDTH), BF16),
        scratch_shapes=[pltpu.VMEM((ATTN_HEADS, HEAD_DIM, tq), F32), s_buf, s_buf],
        compiler_params=_cparams("parallel", "parallel"),
        name="gqa_attention",
    )(qt, k, vt)


DN_LANES = DN_GROUP * DN_DIM
DN_BLOCK = 4
INV_BLOCK = 16
SCALAR_LANES = 128


def _split3(x):
    x1 = x.astype(BF16)
    r1 = x - x1.astype(F32)
    x2 = r1.astype(BF16)
    x3 = (r1 - x2.astype(F32)).astype(BF16)
    return jnp.concatenate([x1, x2, x3], axis=1)


def _sum3(y):
    w = y.shape[1] // 3
    return y[:, :w] + y[:, w:2 * w] + y[:, 2 * w:]


def _dn_kernel(dq_ref, dk_ref, dv_ref, bg_ref, o_ref, mp_sc, n_sc, r_sc, gam_sc, *, n_chunks, n_ctx_chunks):
    c = CHUNK
    rows_b = DN_BLOCK * c
    ln = DN_LANES
    ri = lax.broadcasted_iota(jnp.int32, (rows_b, ln), 0)
    li = lax.broadcasted_iota(jnp.int32, (rows_b, ln), 1)
    same = (ri // c) == (li // c)
    i_in, j_in = ri % c, li % c
    eye_rc = i_in == j_in
    near = (i_in // INV_BLOCK) == (j_in // INV_BLOCK)
    e_row = lax.broadcasted_iota(jnp.int32, (3 * SCALAR_LANES, 2 * ln), 0) % SCALAR_LANES
    e_blk = lax.broadcasted_iota(jnp.int32, (3 * SCALAR_LANES, 2 * ln), 1) // c
    zl = lax.broadcasted_iota(jnp.int32, (rows_b, SCALAR_LANES), 1)
    nt_dims = (((1,), (1,)), ((), ()))
    tn_dims = (((0,), (0,)), ((), ()))

    def bdiag(x):
        return jnp.where(same, jnp.concatenate([x] * DN_GROUP, axis=0), jnp.zeros((), x.dtype))

    def fold(x):
        x = jnp.where(same, x, 0.0)
        return x[0:c] + x[c:2 * c] + x[2 * c:3 * c] + x[3 * c:4 * c]

    dirs = range(2)
    incl = [i_in >= j_in, i_in <= j_in]
    strict = [i_in > j_in, i_in < j_in]
    tri_bd = [jnp.logical_and(same, incl[d]).astype(BF16) for d in dirs]
    sel_last = [jnp.logical_and(same, j_in == (c - 1, 0)[d]).astype(BF16) for d in dirs]
    same_b = same.astype(BF16)
    col0 = [d * 2 * DN_GROUP for d in dirs]
    expand = [(e_row == e_blk + col0[d]).astype(BF16) for d in dirs]
    is_beta = [jnp.logical_and(zl >= col0[d], zl < col0[d] + DN_GROUP) for d in dirs]
    eye_f = eye_rc.astype(F32)
    chunks = range(DN_BLOCK)
    sls = [slice(ch * c, (ch + 1) * c) for ch in chunks]
    mm = lambda a, bmat: jnp.dot(a.astype(BF16), bmat, preferred_element_type=F32)

    def prepare(blk, carry):
        rs = pl.ds(pl.multiple_of(blk * rows_b, rows_b), rows_b)
        z = bg_ref[0, 0, rs, :]
        kbf = dk_ref[0, rs, :]
        qbf = dq_ref[0, rs, :]
        k = kbf.astype(F32)
        q = qbf.astype(F32)
        v = dv_ref[0, rs, :].astype(F32)
        sc = [lax.dot_general(jnp.concatenate([kbf[sl], qbf[sl]], axis=0), bdiag(kbf[sl]), nt_dims,
                              preferred_element_type=F32) for sl in sls]
        lfull, intra, vb, kbe, q_dec, k_dec, gamma = [], [], [], [], [], [], []
        for d in dirs:
            cum = _sum3(jnp.dot(tri_bd[d], _split3(z), preferred_element_type=F32))
            ex = jnp.dot(_split3(jnp.where(is_beta[d], z, cum)), expand[d], preferred_element_type=F32)
            beta, gce = ex[:, :ln], ex[:, ln:]
            g_row = _sum3(jnp.dot(same_b, _split3(jnp.where(eye_rc, gce, 0.0)), preferred_element_type=F32))
            g_last = _sum3(jnp.dot(sel_last[d], _split3(gce), preferred_element_type=F32))
            decay = jnp.where(incl[d], jnp.exp(jnp.where(incl[d], gce - g_row, 0.0)), 0.0)
            bdecay = jnp.where(strict[d], beta * decay, 0.0)
            e_g = jnp.exp(gce)
            kb = k * beta
            kbe_d = (kb * e_g).astype(BF16)
            vb_d = (v * beta).astype(BF16)
            q_dec_d = q * e_g
            k_dec_d = (k * jnp.exp(g_last - gce)).astype(BF16)
            gamma_d = jnp.exp(g_last)
            lfull += [s[:c] * bdecay[sl] for s, sl in zip(sc, sls)]
            intra += [(s[c:] * decay[sl]).astype(BF16) for s, sl in zip(sc, sls)]
            vb += [vb_d[sl] for sl in sls]
            kbe += [kbe_d[sl] for sl in sls]
            q_dec += [q_dec_d[sl] for sl in sls]
            k_dec += [k_dec_d[sl] for sl in sls]
            gamma += [gamma_d[ch * c:ch * c + 8] for ch in chunks]
        near_c = near[sls[0]]
        p = [jnp.where(near_c, -lf, 0.0) for lf in lfull]
        l_off = [jnp.where(near_c, 0.0, lf).astype(BF16) for lf in lfull]
        tmat = [eye_f[sls[0]] + pi for pi in p]
        p = [mm(pi, bdiag(pi.astype(BF16))) for pi in p]
        for _ in range(int(math.log2(INV_BLOCK)) - 2):
            res = [mm(jnp.concatenate([ti, pi], axis=0), bdiag(pi.astype(BF16))) for ti, pi in zip(tmat, p)]
            tmat = [ti + ri[:c] for ti, ri in zip(tmat, res)]
            p = [ri[c:] for ri in res]
        tmat = [ti + mm(ti, bdiag(pi.astype(BF16))) for ti, pi in zip(tmat, p)]
        nmat = [mm(ti, bdiag(lo)) for ti, lo in zip(tmat, l_off)]
        n2 = [mm(ni, bdiag(ni.astype(BF16))) for ni in nmat]
        tmat = [ti + mm(qi, bdiag(ti.astype(BF16))) for ti, qi in zip(tmat, n2)]
        tmat = [ti - mm(ni, bdiag(ti.astype(BF16))) for ti, ni in zip(tmat, nmat)]
        uw = [mm(ti, jnp.concatenate([bdiag(vi), bdiag(ki)], axis=1)).astype(BF16)
              for ti, vi, ki in zip(tmat, vb, kbe)]
        aw_au = [mm(ai, jnp.concatenate([bdiag(x[:, ln:]), bdiag(x[:, :ln])], axis=1))
                 for ai, x in zip(intra, uw)]
        mn = [lax.dot_general(kd, jnp.concatenate([x[:, ln:], x[:, :ln]], axis=1), tn_dims,
                              preferred_element_type=F32) for kd, x in zip(k_dec, uw)]
        for d in dirs:
            for ch in chunks:
                st = d * DN_BLOCK + ch
                ci = blk * DN_BLOCK + ch
                p_c = q_dec[st] - aw_au[st][:, :ln]
                mp_sc[d, ci] = jnp.concatenate([fold(mn[st][:, :ln]), p_c], axis=0).astype(BF16)
                n_sc[d, ci] = fold(mn[st][:, ln:])
                r_sc[d, ci] = aw_au[st][:, ln:].astype(BF16)
                gam_sc[d, ci] = gamma[st]
        return carry

    lax.fori_loop(0, n_chunks // DN_BLOCK, prepare, 0)

    o_ref[...] = jnp.zeros(o_ref.shape, F32)

    def scan(s, states):
        ci = (s, jnp.where(s < n_ctx_chunks, n_ctx_chunks - 1 - s, n_chunks - 1 + n_ctx_chunks - s))
        res = [jnp.dot(mp_sc[d, ci[d]], bdiag(states[d].astype(BF16)), preferred_element_type=F32)
               for d in dirs]
        for d in dirs:
            rows = pl.ds(pl.multiple_of(ci[d] * c, c), c)
            o_ref[0, rows, :] += res[d][c:] + r_sc[d, ci[d]].astype(F32)
        return tuple(gam_sc[d, ci[d]][0:1] * states[d] - res[d][:c] + n_sc[d, ci[d]] for d in dirs)

    zero = jnp.zeros((c, ln), F32)
    lax.fori_loop(0, n_chunks, scan, (zero, zero))


def _deltanet(dq, dk, dv, bg, *, n_ctx):
    b, t, _ = dq.shape
    n_chunks = t // CHUNK
    assert n_chunks % DN_BLOCK == 0 and DN_GROUP * CHUNK == DN_LANES
    ng = DN_HEADS // DN_GROUP
    bgh = bg.reshape(b, t, 2, 2, ng, DN_GROUP).transpose(0, 4, 1, 3, 2, 5).reshape(b, ng, t, 4 * DN_GROUP)
    bgh = jnp.pad(bgh, ((0, 0), (0, 0), (0, 0), (0, SCALAR_LANES - 4 * DN_GROUP)))
    seq = lambda **kw: pl.BlockSpec((1, t, DN_LANES), lambda bi, gi: (bi, 0, gi), **kw)
    once = dict(pipeline_mode=pl.Buffered(1))
    kern = functools.partial(_dn_kernel, n_chunks=n_chunks, n_ctx_chunks=n_ctx // CHUNK)
    return pl.pallas_call(
        kern, grid=(b, ng),
        in_specs=[seq(**once), seq(**once), seq(**once),
                  pl.BlockSpec((1, 1, t, SCALAR_LANES), lambda bi, gi: (bi, gi, 0, 0), **once)],
        out_specs=seq(),
        out_shape=jax.ShapeDtypeStruct((b, t, DN_WIDTH), F32),
        scratch_shapes=[pltpu.VMEM((2, n_chunks, 2 * CHUNK, DN_LANES), BF16),
                        pltpu.VMEM((2, n_chunks, CHUNK, DN_LANES), F32),
                        pltpu.VMEM((2, n_chunks, CHUNK, DN_LANES), BF16),
                        pltpu.VMEM((2, n_chunks, 8, DN_LANES), F32)],
        compiler_params=_cparams("parallel", "parallel"),
        name="gated_deltanet",
    )(dq, dk, dv, bgh)


def _outproj_kernel(a_ref, d_ref, gate_ref, x_ref, mod_ref, dnw_ref, bd_ref, woa_ref, wod_ref, nfw_ref,
                    wr_ref, br_ref, xo_ref, hf_ref, route_ref, grp_ref, cnt_ref):
    dd = d_ref[0]
    gate = gate_ref[0].astype(F32)
    dn = dd * lax.rsqrt(_group_mean_sq(dd, bd_ref[...], DN_DIM) + EPS) * dnw_ref[...] * _silu(gate)
    y = (jnp.dot(a_ref[0], woa_ref[...], preferred_element_type=F32)
         + jnp.dot(dn.astype(BF16), wod_ref[...], preferred_element_type=F32))
    x = x_ref[0] + mod_ref[0, 2:3, :] * y
    xo_ref[0] = x
    h = x * lax.rsqrt(jnp.mean(x * x, axis=-1, keepdims=True) + EPS) * nfw_ref[...]
    h = h * (1.0 + mod_ref[0, 4:5, :]) + mod_ref[0, 3:4, :]
    hb = h.astype(BF16)
    hf_ref[0] = hb
    logits = jnp.dot(hb, wr_ref[...], preferred_element_type=F32) + br_ref[...]
    lane = lax.broadcasted_iota(jnp.int32, logits.shape, 1)
    neg = jnp.float32(-jnp.inf)
    big = jnp.int32(ROUTE_LANES)

    def first_argmax(vals, vmax):
        return jnp.min(jnp.where(vals == vmax, lane, big), axis=-1, keepdims=True)

    is_g = jnp.logical_and(lane >= N_EXPERTS, lane < N_EXPERTS + N_GROUPS)
    gl = jnp.where(is_g, logits, neg)
    g_max = jnp.max(gl, axis=-1, keepdims=True)
    g_sel = first_argmax(gl, g_max) - N_EXPERTS
    p_g = 1.0 / jnp.sum(jnp.exp(gl - g_max), axis=-1, keepdims=True)
    e_lo = g_sel * EXPERTS_PER_GROUP
    in_grp = jnp.logical_and(lane >= e_lo, lane < e_lo + EXPERTS_PER_GROUP)
    el = jnp.where(in_grp, logits, neg)
    e_max = jnp.max(el, axis=-1, keepdims=True)
    i1 = first_argmax(el, e_max)
    el2 = jnp.where(lane == i1, neg, el)
    e_max2 = jnp.max(el2, axis=-1, keepdims=True)
    i2 = first_argmax(el2, e_max2)
    p2 = jnp.exp(e_max2 - e_max)
    w1 = p_g / (1.0 + p2)
    w2 = p_g * p2 / (1.0 + p2)
    route_ref[0] = jnp.where(lane == i1, w1, jnp.where(lane == i2, w2, 0.0)).astype(BF16)
    g_lanes = jnp.broadcast_to(g_sel.astype(F32), logits.shape)
    grp_ref[...] = g_lanes.T[:8]
    cnt_ref[0, 0] = jnp.broadcast_to(
        jnp.sum(jnp.where(lane == g_sel, 1.0, 0.0), axis=0, keepdims=True), (8, ROUTE_LANES))


def _outproj(a, dsum, gate, xs, mods, dnw, bd, woa, wod, nfw, wr, br, *, n_ctx_tiles):
    b, t, d = xs.shape
    tm = TOKEN_TILE
    tok = lambda i, j: (j, i, 0)
    const = lambda i, j: (0, 0)
    modi = lambda i, j: (jnp.where(i < n_ctx_tiles, b, j), 0, 0)
    return pl.pallas_call(
        _outproj_kernel, grid=(t // tm, b),
        in_specs=[pl.BlockSpec((1, tm, ATTN_WIDTH), tok), pl.BlockSpec((1, tm, DN_WIDTH), tok),
                  pl.BlockSpec((1, tm, DN_WIDTH), tok), pl.BlockSpec((1, tm, d), tok),
                  pl.BlockSpec((1, 6, d), modi), pl.BlockSpec((1, DN_WIDTH), const),
                  pl.BlockSpec((DN_WIDTH, DN_WIDTH), const), pl.BlockSpec((ATTN_WIDTH, d), const),
                  pl.BlockSpec((DN_WIDTH, d), const), pl.BlockSpec((1, d), const),
                  pl.BlockSpec((d, ROUTE_LANES), const), pl.BlockSpec((1, ROUTE_LANES), const)],
        out_specs=[pl.BlockSpec((1, tm, d), tok), pl.BlockSpec((1, tm, d), tok),
                   pl.BlockSpec((1, tm, ROUTE_LANES), tok),
                   pl.BlockSpec((8, tm), lambda i, j: (0, j * (t // tm) + i)),
                   pl.BlockSpec((1, 1, 8, ROUTE_LANES), lambda i, j: (j, i, 0, 0))],
        out_shape=[jax.ShapeDtypeStruct((b, t, d), F32), jax.ShapeDtypeStruct((b, t, d), BF16),
                   jax.ShapeDtypeStruct((b, t, ROUTE_LANES), BF16),
                   jax.ShapeDtypeStruct((8, b * t), F32),
                   jax.ShapeDtypeStruct((b, t // tm, 8, ROUTE_LANES), F32)],
        input_output_aliases={3: 0},
        compiler_params=_cparams("parallel", "parallel"),
        name="outproj_router",
    )(a, dsum, gate, xs, mods, dnw, bd, woa, wod, nfw, wr, br)


def _moe_kernel(cnt_ref, h_ref, r_ref, grp_ref, tri_ref, w1_ref, w3_ref, w2_ref, f_ref, acc_sc):
    m, g = pl.program_id(0), pl.program_id(1)
    tm = h_ref.shape[0]
    ff = w1_ref.shape[2]

    @pl.when(g == 0)
    def _():
        acc_sc[...] = jnp.zeros(acc_sc.shape, F32)

    member = grp_ref[0:1, :] == g.astype(F32)
    prefix = jnp.dot(jnp.broadcast_to(member.astype(BF16), (16, tm)), tri_ref[...],
                     preferred_element_type=F32)[0:1]
    pos = jnp.where(member, prefix - 1.0, -1.0)
    e_row = lax.broadcasted_iota(jnp.int32, (ROUTE_LANES, EXPERTS_PER_GROUP * ff), 0)
    e_col = lax.broadcasted_iota(jnp.int32, (ROUTE_LANES, EXPERTS_PER_GROUP * ff), 1) // ff
    expand = (e_row == g * EXPERTS_PER_GROUP + e_col).astype(BF16)

    def sub_block(first_slot, rows):
        slot = lax.broadcasted_iota(jnp.int32, (rows, tm), 0) + first_slot
        sel = (pos == slot.astype(F32)).astype(BF16)
        xs = jnp.dot(sel, h_ref[...], preferred_element_type=F32).astype(BF16)
        wt = jnp.dot(sel, r_ref[...], preferred_element_type=F32).astype(BF16)
        wexp = jnp.dot(wt, expand, preferred_element_type=F32)
        y = None
        for e in range(EXPERTS_PER_GROUP):
            a = jnp.dot(xs, w1_ref[e], preferred_element_type=F32)
            gate = jnp.dot(xs, w3_ref[e], preferred_element_type=F32)
            mid = (_silu(a) * gate * wexp[:, e * ff:(e + 1) * ff]).astype(BF16)
            ye = jnp.dot(mid, w2_ref[e], preferred_element_type=F32)
            y = ye if y is None else y + ye
        acc_sc[...] += lax.dot_general(sel, y.astype(BF16), (((0,), (0,)), ((), ())),
                                       preferred_element_type=F32)

    count = cnt_ref[m * N_GROUPS + g]
    n_full = count // MOE_SUB
    rem = count - n_full * MOE_SUB

    def full_block(sb, carry):
        sub_block(sb * MOE_SUB, MOE_SUB)
        return carry

    lax.fori_loop(0, n_full, full_block, 0)
    lo = 0
    for rows in MOE_TAILS:
        @pl.when(jnp.logical_and(rem > lo, rem <= rows))
        def _(rows=rows):
            sub_block(n_full * MOE_SUB, rows)
        lo = rows

    @pl.when(g == N_GROUPS - 1)
    def _():
        f_ref[...] = acc_sc[...].astype(f_ref.dtype)


def _moe(hf, route, grp, counts, tri, w1, w3, w2):
    n, d = hf.shape
    _, _, ff = w1.shape
    tm = tri.shape[0]
    epg = EXPERTS_PER_GROUP
    grid_spec = pltpu.PrefetchScalarGridSpec(
        num_scalar_prefetch=1, grid=(n // tm, N_GROUPS),
        in_specs=[pl.BlockSpec((tm, d), lambda i, g, c: (i, 0)),
                  pl.BlockSpec((tm, ROUTE_LANES), lambda i, g, c: (i, 0)),
                  pl.BlockSpec((8, tm), lambda i, g, c: (0, i)),
                  pl.BlockSpec((tm, tm), lambda i, g, c: (0, 0)),
                  pl.BlockSpec((epg, d, ff), lambda i, g, c: (g, 0, 0)),
                  pl.BlockSpec((epg, d, ff), lambda i, g, c: (g, 0, 0)),
                  pl.BlockSpec((epg, ff, d), lambda i, g, c: (g, 0, 0))],
        out_specs=pl.BlockSpec((tm, d), lambda i, g, c: (i, 0)),
        scratch_shapes=[pltpu.VMEM((tm, d), F32)])
    return pl.pallas_call(
        _moe_kernel, grid_spec=grid_spec,
        out_shape=jax.ShapeDtypeStruct((n, d), BF16),
        compiler_params=_cparams("parallel", "arbitrary"),
        name="moe_experts",
    )(counts, hf, route, grp, tri, w1, w3, w2)


def _final_kernel(x_ref, f_ref, mod_ref, w_ref, o_ref):
    x = x_ref[0] + mod_ref[0, 5:6, :] * f_ref[0].astype(F32)
    o_ref[0] = x * lax.rsqrt(jnp.mean(x * x, axis=-1, keepdims=True) + EPS) * w_ref[...]


def _final_norm(xs, f_prev, mods, w, *, n_ctx_tiles):
    b, t, d = xs.shape
    tm = TOKEN_TILE
    n_lat = t // tm - n_ctx_tiles
    lat = lambda i, j: (j, i + n_ctx_tiles, 0)
    return pl.pallas_call(
        _final_kernel, grid=(n_lat, b),
        in_specs=[pl.BlockSpec((1, tm, d), lat), pl.BlockSpec((1, tm, d), lat),
                  pl.BlockSpec((1, 6, d), lambda i, j: (j, 0, 0)), pl.BlockSpec((1, d), lambda i, j: (0, 0))],
        out_specs=pl.BlockSpec((1, tm, d), lambda i, j: (j, i, 0)),
        out_shape=jax.ShapeDtypeStruct((b, n_lat * tm, d), F32),
        compiler_params=_cparams("parallel", "parallel"),
        name="final_norm",
    )(xs, f_prev, mods, w)


def _rope_tables(n_ctx, n_lat):
    pos = jnp.arange(n_lat, dtype=jnp.int32)
    inv = ROPE_THETA ** (-jnp.arange(ROPE_NF, dtype=F32) / ROPE_NF)
    ang_r = (pos // GRID_W).astype(F32)[:, None] * inv
    ang_c = (pos % GRID_W).astype(F32)[:, None] * inv
    cos = jnp.concatenate([jnp.cos(ang_r)] * 2 + [jnp.cos(ang_c)] * 2, axis=-1)
    sin = jnp.concatenate([-jnp.sin(ang_r), jnp.sin(ang_r), -jnp.sin(ang_c), jnp.sin(ang_c)], axis=-1)
    cos = jnp.concatenate([jnp.ones((n_ctx, HEAD_DIM), F32), cos], axis=0)
    sin = jnp.concatenate([jnp.zeros((n_ctx, HEAD_DIM), F32), sin], axis=0)
    reps = QK_WIDTH // HEAD_DIM
    return jnp.tile(cos, (1, reps)), jnp.tile(sin, (1, reps))


def kernel(x, c, ctx, c_ctx, ada_w, ada_b, norm_mix_w, norm_ffn_w, w_in, q_norm_w, k_norm_w, conv_w,
           dn_A_log, dn_dt_bias, dn_norm_w, w_out, rg_w, rg_b, re_w, re_b, w1, w3, w2, final_norm_w):
    b, s, d = x.shape
    n_ctx = ctx.shape[1]
    depth = w_in.shape[0]
    t = n_ctx + s
    assert n_ctx % TOKEN_TILE == 0 and s % TOKEN_TILE == 0 and s % GRID_W == 0
    n_ctx_tiles = n_ctx // TOKEN_TILE

    xs = jnp.concatenate([ctx, x], axis=1)
    mod_rows = -(-(b + 1) // 8) * 8
    cs = jnp.zeros((mod_rows, d), F32).at[:b].set(c).at[b].set(c_ctx)
    mods = _ada_mods(cs, ada_w, ada_b).reshape(depth, mod_rows, 6, d)

    cos_t, sin_t = _rope_tables(n_ctx, s)
    bd_qk = _block_ones(QK_WIDTH, HEAD_DIM)
    bd_dn = _block_ones(DN_WIDTH, DN_DIM)
    zeros16 = jnp.zeros((2 * DN_HEADS,), F32)
    moe_tile = next(c for c in MOE_TILES if (b * t) % c == 0)
    tri = (lax.broadcasted_iota(jnp.int32, (moe_tile, moe_tile), 0)
           <= lax.broadcasted_iota(jnp.int32, (moe_tile, moe_tile), 1)).astype(BF16)

    f_prev = None
    for l in range(depth):
        qkw = jnp.concatenate([jnp.tile(q_norm_w[l], ATTN_HEADS), jnp.tile(k_norm_w[l], KV_HEADS)])[None]
        dnp = jnp.stack([jnp.concatenate([zeros16, dn_A_log[l].reshape(-1)]),
                         jnp.concatenate([zeros16, dn_dt_bias[l].reshape(-1)])])
        outs = _inproj(xs, f_prev, mods[l - 1] if l else None, mods[l], norm_mix_w[l][None],
                       w_in[l].astype(BF16), qkw, cos_t, sin_t, bd_qk, dnp, n_ctx_tiles=n_ctx_tiles)
        qt, k, vt, dqkv, gate, bg = outs[:6]
        if l:
            xs = outs[6]
        dq, dk, dv = _dn_conv(dqkv, conv_w[l], bd_dn, n_ctx_tiles=n_ctx_tiles)
        a = _attention(qt, k, vt, n_ctx=n_ctx)
        dsum = _deltanet(dq, dk, dv, bg, n_ctx=n_ctx)
        wo = w_out[l].astype(BF16)
        wr = jnp.zeros((d, ROUTE_LANES), F32).at[:, :N_EXPERTS].set(re_w[l]).at[
            :, N_EXPERTS:N_EXPERTS + N_GROUPS].set(rg_w[l]).astype(BF16)
        br = jnp.zeros((1, ROUTE_LANES), F32).at[0, :N_EXPERTS].set(re_b[l]).at[
            0, N_EXPERTS:N_EXPERTS + N_GROUPS].set(rg_b[l])
        xs, hf, route, grp, cnt = _outproj(
            a, dsum, gate, xs, mods[l], jnp.tile(dn_norm_w[l], DN_HEADS)[None], bd_dn,
            wo[:ATTN_WIDTH], wo[ATTN_WIDTH:], norm_ffn_w[l][None], wr, br, n_ctx_tiles=n_ctx_tiles)
        counts = cnt[:, :, 0, :N_GROUPS].reshape(-1, moe_tile // TOKEN_TILE, N_GROUPS).sum(axis=1)
        f_prev = _moe(hf.reshape(b * t, d), route.reshape(b * t, ROUTE_LANES), grp,
                      counts.astype(jnp.int32).reshape(-1), tri,
                      w1[l].astype(BF16), w3[l].astype(BF16), w2[l].astype(BF16)).reshape(b, t, d)
    return _final_norm(xs, f_prev, mods[depth - 1], final_norm_w[None], n_ctx_tiles=n_ctx_tiles)
```

```python
import functools
import math

import jax
import jax.numpy as jnp
from jax import lax
from jax.experimental import pallas as pl
from jax.experimental.pallas import tpu as pltpu

F32 = jnp.float32
BF16 = jnp.bfloat16
HIGHEST = lax.Precision.HIGHEST

GRID_W = 64
HEAD_DIM = 64
ATTN_HEADS = 8
KV_HEADS = 2
GQA_GROUP = ATTN_HEADS // KV_HEADS
ATTN_WIDTH = ATTN_HEADS * HEAD_DIM
KV_WIDTH = KV_HEADS * HEAD_DIM
ATTN_SCALE = HEAD_DIM ** -0.5
LOG2_E = math.log2(math.e)
ROPE_THETA = 10000.0
ROPE_NF = HEAD_DIM // 4
DN_HEADS = 8
DN_DIM = 64
DN_WIDTH = DN_HEADS * DN_DIM
DN_SCALE = DN_DIM ** -0.5
DN_GROUP = 4
CONV_K = 5
CHUNK = 64
N_GROUPS = 4
EXPERTS_PER_GROUP = 8
N_EXPERTS = N_GROUPS * EXPERTS_PER_GROUP
EPS = 1e-6
QK_WIDTH = ATTN_WIDTH + KV_WIDTH
TOKEN_TILE = 256
ATTN_KEY_TILE = 256
ONES_ROWS = 16
HEADS_PER_PASS = 8
HALO = 16
ROW_PARTS = 1
ROUTE_LANES = 128
MOE_TILES = (1024, 512, 256)
MOE_SUB = 256
MOE_TAILS = (64, 128, 256)
MOE_WIDE = (320, 384)
POS_ROWS = 16
VMEM_LIMIT = 56 * 1024 * 1024


def _cparams(*sem):
    return pltpu.CompilerParams(dimension_semantics=sem, vmem_limit_bytes=VMEM_LIMIT)


def _silu(x):
    return x * jax.nn.sigmoid(x)


def _block_ones(n, blk):
    i = lax.broadcasted_iota(jnp.int32, (n, n), 0) // blk
    j = lax.broadcasted_iota(jnp.int32, (n, n), 1) // blk
    return (i == j).astype(BF16)


def _group_mean_sq(x, ones_bd, width):
    return jnp.dot((x * x).astype(BF16), ones_bd, preferred_element_type=F32) * (1.0 / width)


def _ada_kernel(cs_ref, w_ref, b_ref, o_ref):
    o_ref[0] = jnp.dot(_silu(cs_ref[...]), w_ref[0], preferred_element_type=F32,
                       precision=HIGHEST) + b_ref[0]


def _ada_mods(cs, ada_w, ada_b):
    depth, d, n6 = ada_w.shape
    rows = cs.shape[0]
    tn = 1536
    return pl.pallas_call(
        _ada_kernel,
        grid=(depth, n6 // tn),
        in_specs=[pl.BlockSpec((rows, d), lambda l, j: (0, 0)),
                  pl.BlockSpec((1, d, tn), lambda l, j: (l, 0, j)),
                  pl.BlockSpec((1, 1, tn), lambda l, j: (l, 0, j))],
        out_specs=pl.BlockSpec((1, rows, tn), lambda l, j: (l, 0, j)),
        out_shape=jax.ShapeDtypeStruct((depth, rows, n6), F32),
        compiler_params=_cparams("parallel", "parallel"),
        name="ada_mods",
    )(cs, ada_w, ada_b.reshape(depth, 1, n6))


def _inproj_kernel(*refs, has_prev):
    if has_prev:
        (x_ref, f_ref, modp_ref, mod_ref, nw_ref, win_ref, qkw_ref, cos_ref, sin_ref, bd_ref, dnp_ref,
         qt_ref, k_ref, vt_ref, dqkv_ref, gate_ref, bg_ref, xo_ref) = refs
        x = x_ref[0] + modp_ref[0, 5:6, :] * f_ref[0].astype(F32)
        xo_ref[0] = x
    else:
        (x_ref, mod_ref, nw_ref, win_ref, qkw_ref, cos_ref, sin_ref, bd_ref, dnp_ref,
         qt_ref, k_ref, vt_ref, dqkv_ref, gate_ref, bg_ref) = refs
        x = x_ref[0]
    h = x * lax.rsqrt(jnp.mean(x * x, axis=-1, keepdims=True) + EPS) * nw_ref[...]
    h = h * (1.0 + mod_ref[0, 1:2, :]) + mod_ref[0, 0:1, :]
    acc = jnp.dot(h.astype(BF16), win_ref[...], preferred_element_type=F32)
    qk = acc[:, :QK_WIDTH]
    qn = qk * lax.rsqrt(_group_mean_sq(qk, bd_ref[...], HEAD_DIM) + EPS) * qkw_ref[...]
    lane = lax.broadcasted_iota(jnp.int32, qn.shape, 1)
    partner = jnp.where(lane % (2 * ROPE_NF) < ROPE_NF,
                        pltpu.roll(qn, QK_WIDTH - ROPE_NF, 1), pltpu.roll(qn, ROPE_NF, 1))
    qr = qn * cos_ref[...] + partner * sin_ref[...]
    qt_ref[0] = (qr[:, :ATTN_WIDTH] * (ATTN_SCALE * LOG2_E)).T.astype(BF16)
    k_ref[0] = qr[:, ATTN_WIDTH:].astype(BF16)
    c0 = QK_WIDTH
    vt_ref[0] = acc[:, c0:c0 + KV_WIDTH].T.astype(BF16)
    c0 += KV_WIDTH
    dqkv_ref[0] = acc[:, c0:c0 + 3 * DN_WIDTH].astype(BF16)
    c0 += 3 * DN_WIDTH
    gate_ref[0] = acc[:, c0:c0 + DN_WIDTH].astype(BF16)
    c0 += DN_WIDTH
    z = acc[:, c0:]
    zb = z + dnp_ref[1:2, :]
    softplus = jnp.maximum(zb, 0.0) + jnp.log1p(jnp.exp(-jnp.abs(zb)))
    lane_z = lax.broadcasted_iota(jnp.int32, z.shape, 1)
    bg = jnp.where(lane_z % (2 * DN_GROUP) < DN_GROUP, jax.nn.sigmoid(z), -jnp.exp(dnp_ref[0:1, :]) * softplus)
    for gi in range(DN_HEADS // DN_GROUP):
        bg_ref[0, gi] = bg[:, gi * SCALAR_LANES:(gi + 1) * SCALAR_LANES]


def _inproj(xs, f_prev, mods_prev, mods, nw, win, qkw, cos_t, sin_t, bd, dnp, *, n_ctx_tiles):
    b, t, d = xs.shape
    tm = TOKEN_TILE
    nt = t // tm
    proj = win.shape[1]
    has_prev = f_prev is not None
    tok = lambda i, j: (j, i, 0)
    modi = lambda i, j: (jnp.where(i < n_ctx_tiles, b, j), 0, 0)
    const = lambda i, j: (0, 0)
    in_specs = [pl.BlockSpec((1, tm, d), tok)]
    args = [xs]
    if has_prev:
        in_specs += [pl.BlockSpec((1, tm, d), tok), pl.BlockSpec((1, 6, d), modi)]
        args += [f_prev, mods_prev]
    in_specs += [pl.BlockSpec((1, 6, d), modi), pl.BlockSpec((1, d), const),
                 pl.BlockSpec((d, proj), const), pl.BlockSpec((1, QK_WIDTH), const),
                 pl.BlockSpec((tm, QK_WIDTH), lambda i, j: (i, 0)),
                 pl.BlockSpec((tm, QK_WIDTH), lambda i, j: (i, 0)),
                 pl.BlockSpec((QK_WIDTH, QK_WIDTH), const), pl.BlockSpec((2, dnp.shape[1]), const)]
    args += [mods, nw, win, qkw, cos_t, sin_t, bd, dnp]
    seq_out = lambda w, dt: (pl.BlockSpec((1, tm, w), tok), jax.ShapeDtypeStruct((b, t, w), dt))
    tr_out = lambda w: (pl.BlockSpec((1, w, tm), lambda i, j: (j, 0, i)), jax.ShapeDtypeStruct((b, w, t), BF16))
    ng = DN_HEADS // DN_GROUP
    outs = [tr_out(ATTN_WIDTH), seq_out(KV_WIDTH, BF16), tr_out(KV_WIDTH),
            seq_out(3 * DN_WIDTH, BF16), seq_out(DN_WIDTH, BF16),
            (pl.BlockSpec((1, ng, tm, SCALAR_LANES), lambda i, j: (j, 0, i, 0)),
             jax.ShapeDtypeStruct((b, ng, t, SCALAR_LANES), F32))]
    out_specs = [o[0] for o in outs]
    out_shape = [o[1] for o in outs]
    aliases = {}
    if has_prev:
        out_specs.append(pl.BlockSpec((1, tm, d), tok))
        out_shape.append(jax.ShapeDtypeStruct((b, t, d), F32))
        aliases = {0: len(out_shape) - 1}
    return pl.pallas_call(
        functools.partial(_inproj_kernel, has_prev=has_prev),
        grid=(nt, b), in_specs=in_specs, out_specs=out_specs, out_shape=out_shape,
        input_output_aliases=aliases,
        compiler_params=_cparams("parallel", "parallel"),
        name="inproj",
    )(*args)


def _conv_kernel(x_ref, xb_ref, xa_ref, w_ref, bd_ref, dq_ref, dk_ref, dv_ref, *, n_ctx_tiles, n_tiles):
    i = pl.program_id(1)
    tm = x_ref.shape[1]
    first = jnp.logical_or(i == 0, i == n_ctx_tiles)
    last = jnp.logical_or(i == n_ctx_tiles - 1, i == n_tiles - 1)
    before = xb_ref[0].astype(F32)[HALO - 8:] * jnp.where(first, 0.0, 1.0)
    after = xa_ref[0].astype(F32)[:8] * jnp.where(last, 0.0, 1.0)
    ext = jnp.concatenate([before, x_ref[0].astype(F32), after], axis=0)
    n_ext = tm + 16
    y = None
    for j in range(CONV_K):
        shift = (CONV_K // 2 - j) % n_ext
        tap = ext if shift == 0 else pltpu.roll(ext, shift, 0)
        term = tap[8:8 + tm] * w_ref[j:j + 1, :]
        y = term if y is None else y + term
    y = _silu(y)
    q, k, v = y[:, :DN_WIDTH], y[:, DN_WIDTH:2 * DN_WIDTH], y[:, 2 * DN_WIDTH:]
    bd = bd_ref[...]
    dq_ref[0] = (q * lax.rsqrt(_group_mean_sq(q, bd, 1.0) + EPS) * DN_SCALE).astype(BF16)
    dk_ref[0] = (k * lax.rsqrt(_group_mean_sq(k, bd, 1.0) + EPS)).astype(BF16)
    dv_ref[0] = v.astype(BF16)


def _dn_conv(dqkv, conv_w, bd, *, n_ctx_tiles):
    b, t, c = dqkv.shape
    tm = TOKEN_TILE
    nt = t // tm
    r = tm // HALO
    n_halo = t // HALO
    kern = functools.partial(_conv_kernel, n_ctx_tiles=n_ctx_tiles, n_tiles=nt)
    out = jax.ShapeDtypeStruct((b, t, DN_WIDTH), BF16)
    return pl.pallas_call(
        kern, grid=(b, nt),
        in_specs=[pl.BlockSpec((1, tm, c), lambda bi, i: (bi, i, 0)),
                  pl.BlockSpec((1, HALO, c), lambda bi, i: (bi, jnp.maximum(i * r - 1, 0), 0)),
                  pl.BlockSpec((1, HALO, c), lambda bi, i: (bi, jnp.minimum((i + 1) * r, n_halo - 1), 0)),
                  pl.BlockSpec((CONV_K, c), lambda bi, i: (0, 0)),
                  pl.BlockSpec((DN_WIDTH, DN_WIDTH), lambda bi, i: (0, 0))],
        out_specs=[pl.BlockSpec((1, tm, DN_WIDTH), lambda bi, i: (bi, i, 0))] * 3,
        out_shape=[out, out, out],
        compiler_params=_cparams("parallel", "parallel"),
        name="dn_conv",
    )(dqkv, dqkv, dqkv, conv_w, bd)


def _attn_kernel(qt_ref, k_ref, vt_ref, o_ref, acc_sc, sa_sc, sb_sc, *, tk, n_ctx_q, n_ctx_k, n_all_k):
    tq = qt_ref.shape[2]
    n_kv = jnp.where(pl.program_id(1) < n_ctx_q, n_ctx_k, n_all_k)
    nh = HEADS_PER_PASS
    local = range(nh)
    last = n_kv - 1

    def tile_rows(i):
        return pl.ds(pl.multiple_of(i * tk, tk), tk)

    for h0 in range(0, ATTN_HEADS, nh):
        kv = [(h0 + j) // GQA_GROUP for j in local]
        acc_sc[...] = jnp.zeros(acc_sc.shape, F32)

        def scores(i, s_sc, h0=h0, kv=kv):
            rows = tile_rows(i)
            for j in local:
                s_sc[j] = jnp.dot(k_ref[0, rows, kv[j] * HEAD_DIM:(kv[j] + 1) * HEAD_DIM],
                                  qt_ref[0, (h0 + j) * HEAD_DIM:(h0 + j + 1) * HEAD_DIM, :],
                                  preferred_element_type=F32)

        def softmax_pv(i, s_sc, carry, kv=kv):
            rows = tile_rows(i)
            m_prev, l_prev = carry[:nh], carry[nh:]
            s = [s_sc[j] for j in local]
            m_new = [jnp.maximum(m_prev[j], jnp.max(s[j], axis=0, keepdims=True)) for j in local]
            p = [jnp.exp2(s[j] - m_new[j]) for j in local]
            alpha = [jnp.exp2(m_prev[j] - m_new[j]) for j in local]
            ones = jnp.ones((ONES_ROWS, tk), BF16)
            vt1 = {g: jnp.concatenate([vt_ref[0, g * HEAD_DIM:(g + 1) * HEAD_DIM, rows], ones], axis=0)
                   for g in sorted(set(kv))}
            pv = [jnp.dot(vt1[kv[j]], p[j].astype(BF16), preferred_element_type=F32) for j in local]
            for j in local:
                acc_sc[j] = alpha[j] * acc_sc[j] + pv[j][:HEAD_DIM]
            l_new = [alpha[j] * l_prev[j] + pv[j][HEAD_DIM:HEAD_DIM + 1] for j in local]
            return tuple(m_new) + tuple(l_new)

        init = (jnp.full((1, tq), -jnp.inf, F32),) * nh + (jnp.zeros((1, tq), F32),) * nh
        scores(0, sa_sc)
        scores(jnp.minimum(1, last), sb_sc)
        stats = softmax_pv(0, sa_sc, init)

        def pair(j, carry, scores=scores, softmax_pv=softmax_pv):
            a = 2 * j + 1
            scores(a + 1, sa_sc)
            carry = softmax_pv(a, sb_sc, carry)
            scores(jnp.minimum(a + 2, last), sb_sc)
            return softmax_pv(a + 1, sa_sc, carry)

        stats = lax.fori_loop(0, last // 2, pair, stats)
        og = jnp.concatenate([acc_sc[j] / stats[nh + j] for j in local], axis=0)
        o_ref[0, :, h0 * HEAD_DIM:(h0 + nh) * HEAD_DIM] = og.T.astype(BF16)


def _attention(qt, k, vt, *, n_ctx):
    b, _, t = qt.shape
    tq, tk = TOKEN_TILE, ATTN_KEY_TILE
    assert (n_ctx // tk) % 2 == 1 and (t // tk) % 2 == 1
    kern = functools.partial(_attn_kernel, tk=tk, n_ctx_q=n_ctx // tq, n_ctx_k=n_ctx // tk, n_all_k=t // tk)
    s_buf = pltpu.VMEM((HEADS_PER_PASS, tk, tq), F32)
    return pl.pallas_call(
        kern, grid=(b, t // tq),
        in_specs=[pl.BlockSpec((1, ATTN_WIDTH, tq), lambda bi, i: (bi, 0, i)),
                  pl.BlockSpec((1, t, KV_WIDTH), lambda bi, i: (bi, 0, 0)),
                  pl.BlockSpec((1, KV_WIDTH, t), lambda bi, i: (bi, 0, 0))],
        out_specs=pl.BlockSpec((1, tq, ATTN_WIDTH), lambda bi, i: (bi, i, 0)),
        out_shape=jax.ShapeDtypeStruct((b, t, ATTN_WIDTH), BF16),
        scratch_shapes=[pltpu.VMEM((HEADS_PER_PASS, HEAD_DIM, tq), F32), s_buf, s_buf],
        compiler_params=_cparams("parallel", "parallel"),
        name="gqa_attention",
    )(qt, k, vt)


DN_LANES = DN_GROUP * DN_DIM
DN_BLOCK = 4
INV_BLOCK = 16
SCALAR_LANES = 128


def _split3(x):
    x1 = x.astype(BF16)
    r1 = x - x1.astype(F32)
    x2 = r1.astype(BF16)
    x3 = (r1 - x2.astype(F32)).astype(BF16)
    return jnp.concatenate([x1, x2, x3], axis=1)


def _sum3(y):
    w = y.shape[1] // 3
    return y[:, :w] + y[:, w:2 * w] + y[:, 2 * w:]


def _dn_kernel(dq_ref, dk_ref, dv_ref, bg_ref, o_ref, mp_sc, n_sc, r_sc, gam_sc, *, n_chunks, n_ctx_chunks):
    c = CHUNK
    rows_b = DN_BLOCK * c
    ln = DN_LANES
    ri = lax.broadcasted_iota(jnp.int32, (rows_b, ln), 0)
    li = lax.broadcasted_iota(jnp.int32, (rows_b, ln), 1)
    same = (ri // c) == (li // c)
    i_in, j_in = ri % c, li % c
    eye_rc = i_in == j_in
    near = (i_in // INV_BLOCK) == (j_in // INV_BLOCK)
    e_row = lax.broadcasted_iota(jnp.int32, (3 * SCALAR_LANES, 2 * ln), 0) % SCALAR_LANES
    e_blk = lax.broadcasted_iota(jnp.int32, (3 * SCALAR_LANES, 2 * ln), 1) // c
    zl = lax.broadcasted_iota(jnp.int32, (rows_b, SCALAR_LANES), 1)
    nt_dims = (((1,), (1,)), ((), ()))
    tn_dims = (((0,), (0,)), ((), ()))

    def bdiag(x):
        return jnp.where(same, jnp.concatenate([x] * DN_GROUP, axis=0), jnp.zeros((), x.dtype))

    def fold(x):
        x = jnp.where(same, x, 0.0)
        return x[0:c] + x[c:2 * c] + x[2 * c:3 * c] + x[3 * c:4 * c]

    dirs = range(2)
    incl = [i_in >= j_in, i_in <= j_in]
    strict = [i_in > j_in, i_in < j_in]
    tri_bd = [jnp.logical_and(same, incl[d]).astype(BF16) for d in dirs]
    sel_last = [jnp.logical_and(same, j_in == (c - 1, 0)[d]).astype(BF16) for d in dirs]
    same_b = same.astype(BF16)
    col0 = [d * 2 * DN_GROUP for d in dirs]
    expand = [(e_row == e_blk + col0[d]).astype(BF16) for d in dirs]
    is_beta = [jnp.logical_and(zl >= col0[d], zl < col0[d] + DN_GROUP) for d in dirs]
    eye_f = eye_rc.astype(F32)
    chunks = range(DN_BLOCK)
    sls = [slice(ch * c, (ch + 1) * c) for ch in chunks]
    mm = lambda a, bmat: jnp.dot(a.astype(BF16), bmat, preferred_element_type=F32)

    def prepare(blk, carry):
        rs = pl.ds(pl.multiple_of(blk * rows_b, rows_b), rows_b)
        z = bg_ref[0, 0, rs, :]
        kbf = dk_ref[0, rs, :]
        qbf = dq_ref[0, rs, :]
        k = kbf.astype(F32)
        q = qbf.astype(F32)
        v = dv_ref[0, rs, :].astype(F32)
        sc = [lax.dot_general(jnp.concatenate([kbf[sl], qbf[sl]], axis=0), bdiag(kbf[sl]), nt_dims,
                              preferred_element_type=F32) for sl in sls]
        lfull, intra, vb, kbe, q_dec, k_dec, gamma = [], [], [], [], [], [], []
        for d in dirs:
            cum = _sum3(jnp.dot(tri_bd[d], _split3(z), preferred_element_type=F32))
            ex = jnp.dot(_split3(jnp.where(is_beta[d], z, cum)), expand[d], preferred_element_type=F32)
            beta, gce = ex[:, :ln], ex[:, ln:]
            g_row = _sum3(jnp.dot(same_b, _split3(jnp.where(eye_rc, gce, 0.0)), preferred_element_type=F32))
            g_last = _sum3(jnp.dot(sel_last[d], _split3(gce), preferred_element_type=F32))
            decay = jnp.where(incl[d], jnp.exp(jnp.where(incl[d], gce - g_row, 0.0)), 0.0)
            bdecay = jnp.where(strict[d], beta * decay, 0.0)
            e_g = jnp.exp(gce)
            kb = k * beta
            kbe_d = (kb * e_g).astype(BF16)
            vb_d = (v * beta).astype(BF16)
            q_dec_d = q * e_g
            k_dec_d = (k * jnp.exp(g_last - gce)).astype(BF16)
            gamma_d = jnp.exp(g_last)
            lfull += [s[:c] * bdecay[sl] for s, sl in zip(sc, sls)]
            intra += [(s[c:] * decay[sl]).astype(BF16) for s, sl in zip(sc, sls)]
            vb += [vb_d[sl] for sl in sls]
            kbe += [kbe_d[sl] for sl in sls]
            q_dec += [q_dec_d[sl] for sl in sls]
            k_dec += [k_dec_d[sl] for sl in sls]
            gamma += [gamma_d[ch * c:ch * c + 8] for ch in chunks]
        near_c = near[sls[0]]
        p = [jnp.where(near_c, -lf, 0.0) for lf in lfull]
        l_off = [jnp.where(near_c, 0.0, lf).astype(BF16) for lf in lfull]
        tmat = [eye_f[sls[0]] + pi for pi in p]
        p = [mm(pi, bdiag(pi.astype(BF16))) for pi in p]
        for _ in range(int(math.log2(INV_BLOCK)) - 2):
            res = [mm(jnp.concatenate([ti, pi], axis=0), bdiag(pi.astype(BF16))) for ti, pi in zip(tmat, p)]
            tmat = [ti + ri[:c] for ti, ri in zip(tmat, res)]
            p = [ri[c:] for ri in res]
        tmat = [ti + mm(ti, bdiag(pi.astype(BF16))) for ti, pi in zip(tmat, p)]
        nmat = [mm(ti, bdiag(lo)) for ti, lo in zip(tmat, l_off)]
        n2 = [mm(ni, bdiag(ni.astype(BF16))) for ni in nmat]
        tmat = [ti + mm(qi, bdiag(ti.astype(BF16))) for ti, qi in zip(tmat, n2)]
        tmat = [ti - mm(ni, bdiag(ti.astype(BF16))) for ti, ni in zip(tmat, nmat)]
        uw = [mm(ti, jnp.concatenate([bdiag(vi), bdiag(ki)], axis=1)).astype(BF16)
              for ti, vi, ki in zip(tmat, vb, kbe)]
        aw_au = [mm(ai, jnp.concatenate([bdiag(x[:, ln:]), bdiag(x[:, :ln])], axis=1))
                 for ai, x in zip(intra, uw)]
        mn = [lax.dot_general(kd, jnp.concatenate([x[:, ln:], x[:, :ln]], axis=1), tn_dims,
                              preferred_element_type=F32) for kd, x in zip(k_dec, uw)]
        for d in dirs:
            for ch in chunks:
                st = d * DN_BLOCK + ch
                ci = blk * DN_BLOCK + ch
                p_c = q_dec[st] - aw_au[st][:, :ln]
                mp_sc[d, ci] = jnp.concatenate([fold(mn[st][:, :ln]), p_c], axis=0).astype(BF16)
                n_sc[d, ci] = fold(mn[st][:, ln:])
                r_sc[d, ci] = aw_au[st][:, ln:].astype(BF16)
                gam_sc[d, ci] = gamma[st]
        return carry

    lax.fori_loop(0, n_chunks // DN_BLOCK, prepare, 0)

    o_ref[...] = jnp.zeros(o_ref.shape, F32)

    def scan(s, states):
        ci = (s, jnp.where(s < n_ctx_chunks, n_ctx_chunks - 1 - s, n_chunks - 1 + n_ctx_chunks - s))
        res = [jnp.dot(mp_sc[d, ci[d]], bdiag(states[d].astype(BF16)), preferred_element_type=F32)
               for d in dirs]
        for d in dirs:
            rows = pl.ds(pl.multiple_of(ci[d] * c, c), c)
            o_ref[0, rows, :] += res[d][c:] + r_sc[d, ci[d]].astype(F32)
        return tuple(gam_sc[d, ci[d]][0:1] * states[d] - res[d][:c] + n_sc[d, ci[d]] for d in dirs)

    zero = jnp.zeros((c, ln), F32)
    lax.fori_loop(0, n_chunks, scan, (zero, zero))


def _deltanet(dq, dk, dv, bg, *, n_ctx):
    b, t, _ = dq.shape
    n_chunks = t // CHUNK
    assert n_chunks % DN_BLOCK == 0 and DN_GROUP * CHUNK == DN_LANES
    ng = DN_HEADS // DN_GROUP
    seq = lambda **kw: pl.BlockSpec((1, t, DN_LANES), lambda bi, gi: (bi, 0, gi), **kw)
    once = dict(pipeline_mode=pl.Buffered(1))
    kern = functools.partial(_dn_kernel, n_chunks=n_chunks, n_ctx_chunks=n_ctx // CHUNK)
    return pl.pallas_call(
        kern, grid=(b, ng),
        in_specs=[seq(**once), seq(**once), seq(**once),
                  pl.BlockSpec((1, 1, t, SCALAR_LANES), lambda bi, gi: (bi, gi, 0, 0), **once)],
        out_specs=seq(),
        out_shape=jax.ShapeDtypeStruct((b, t, DN_WIDTH), F32),
        scratch_shapes=[pltpu.VMEM((2, n_chunks, 2 * CHUNK, DN_LANES), BF16),
                        pltpu.VMEM((2, n_chunks, CHUNK, DN_LANES), F32),
                        pltpu.VMEM((2, n_chunks, CHUNK, DN_LANES), BF16),
                        pltpu.VMEM((2, n_chunks, 8, DN_LANES), F32)],
        compiler_params=_cparams("parallel", "parallel"),
        name="gated_deltanet",
    )(dq, dk, dv, bg)


def _outproj_kernel(a_ref, d_ref, gate_ref, x_ref, mod_ref, dnw_ref, bd_ref, woa_ref, wod_ref, nfw_ref,
                    wr_ref, br_ref, xo_ref, hf_ref, route_ref, grp_ref, cnt_ref):
    tm = x_ref.shape[1]
    part = tm // ROW_PARTS
    neg = jnp.float32(-jnp.inf)
    big = jnp.int32(ROUTE_LANES)
    counts = None
    for r in range(ROW_PARTS):
        rs = slice(r * part, (r + 1) * part)
        dd = d_ref[0, rs, :]
        gate = gate_ref[0, rs, :].astype(F32)
        dn = dd * lax.rsqrt(_group_mean_sq(dd, bd_ref[...], DN_DIM) + EPS) * dnw_ref[...] * _silu(gate)
        y = (jnp.dot(a_ref[0, rs, :], woa_ref[...], preferred_element_type=F32)
             + jnp.dot(dn.astype(BF16), wod_ref[...], preferred_element_type=F32))
        x = x_ref[0, rs, :] + mod_ref[0, 2:3, :] * y
        xo_ref[0, rs, :] = x
        h = x * lax.rsqrt(jnp.mean(x * x, axis=-1, keepdims=True) + EPS) * nfw_ref[...]
        h = h * (1.0 + mod_ref[0, 4:5, :]) + mod_ref[0, 3:4, :]
        hb = h.astype(BF16)
        hf_ref[0, rs, :] = hb
        logits = jnp.dot(hb, wr_ref[...], preferred_element_type=F32) + br_ref[...]
        lane = lax.broadcasted_iota(jnp.int32, logits.shape, 1)

        def first_argmax(vals, vmax, lane=lane):
            return jnp.min(jnp.where(vals == vmax, lane, big), axis=-1, keepdims=True)

        is_g = jnp.logical_and(lane >= N_EXPERTS, lane < N_EXPERTS + N_GROUPS)
        gl = jnp.where(is_g, logits, neg)
        g_max = jnp.max(gl, axis=-1, keepdims=True)
        g_sel = first_argmax(gl, g_max) - N_EXPERTS
        p_g = 1.0 / jnp.sum(jnp.exp(gl - g_max), axis=-1, keepdims=True)
        e_lo = g_sel * EXPERTS_PER_GROUP
        in_grp = jnp.logical_and(lane >= e_lo, lane < e_lo + EXPERTS_PER_GROUP)
        el = jnp.where(in_grp, logits, neg)
        e_max = jnp.max(el, axis=-1, keepdims=True)
        i1 = first_argmax(el, e_max)
        el2 = jnp.where(lane == i1, neg, el)
        e_max2 = jnp.max(el2, axis=-1, keepdims=True)
        i2 = first_argmax(el2, e_max2)
        p2 = jnp.exp(e_max2 - e_max)
        w1 = p_g / (1.0 + p2)
        w2 = p_g * p2 / (1.0 + p2)
        route_ref[0, rs, :] = jnp.where(lane == i1, w1, jnp.where(lane == i2, w2, 0.0)).astype(BF16)
        g_lanes = jnp.broadcast_to(g_sel.astype(F32), logits.shape)
        grp_ref[:, rs] = g_lanes.T[:8]
        cnt = jnp.sum(jnp.where(lane == g_sel, 1.0, 0.0), axis=0, keepdims=True)
        counts = cnt if counts is None else counts + cnt
    cnt_ref[0, 0] = jnp.broadcast_to(counts, (8, ROUTE_LANES))


def _outproj(a, dsum, gate, xs, mods, dnw, bd, woa, wod, nfw, wr, br, *, n_ctx_tiles):
    b, t, d = xs.shape
    tm = TOKEN_TILE
    tok = lambda i, j: (j, i, 0)
    const = lambda i, j: (0, 0)
    modi = lambda i, j: (jnp.where(i < n_ctx_tiles, b, j), 0, 0)
    return pl.pallas_call(
        _outproj_kernel, grid=(t // tm, b),
        in_specs=[pl.BlockSpec((1, tm, ATTN_WIDTH), tok), pl.BlockSpec((1, tm, DN_WIDTH), tok),
                  pl.BlockSpec((1, tm, DN_WIDTH), tok), pl.BlockSpec((1, tm, d), tok),
                  pl.BlockSpec((1, 6, d), modi), pl.BlockSpec((1, DN_WIDTH), const),
                  pl.BlockSpec((DN_WIDTH, DN_WIDTH), const), pl.BlockSpec((ATTN_WIDTH, d), const),
                  pl.BlockSpec((DN_WIDTH, d), const), pl.BlockSpec((1, d), const),
                  pl.BlockSpec((d, ROUTE_LANES), const), pl.BlockSpec((1, ROUTE_LANES), const)],
        out_specs=[pl.BlockSpec((1, tm, d), tok), pl.BlockSpec((1, tm, d), tok),
                   pl.BlockSpec((1, tm, ROUTE_LANES), tok),
                   pl.BlockSpec((8, tm), lambda i, j: (0, j * (t // tm) + i)),
                   pl.BlockSpec((1, 1, 8, ROUTE_LANES), lambda i, j: (j, i, 0, 0))],
        out_shape=[jax.ShapeDtypeStruct((b, t, d), F32), jax.ShapeDtypeStruct((b, t, d), BF16),
                   jax.ShapeDtypeStruct((b, t, ROUTE_LANES), BF16),
                   jax.ShapeDtypeStruct((8, b * t), F32),
                   jax.ShapeDtypeStruct((b, t // tm, 8, ROUTE_LANES), F32)],
        input_output_aliases={3: 0},
        compiler_params=_cparams("parallel", "parallel"),
        name="outproj_router",
    )(a, dsum, gate, xs, mods, dnw, bd, woa, wod, nfw, wr, br)


def _moe_kernel(cnt_ref, h_ref, r_ref, grp_ref, tri_ref, w1_ref, w3_ref, w2_ref, f_ref, acc_sc, pos_sc):
    m, g = pl.program_id(0), pl.program_id(1)
    tm = h_ref.shape[0]
    ff = w1_ref.shape[2]

    @pl.when(g == 0)
    def _():
        acc_sc[...] = jnp.zeros(acc_sc.shape, F32)
        grp = jnp.broadcast_to(grp_ref[0:1, :], (POS_ROWS * N_GROUPS, tm))
        row_g = lax.broadcasted_iota(jnp.int32, grp.shape, 0) // POS_ROWS
        member = grp == row_g.astype(F32)
        prefix = jnp.dot(member.astype(BF16), tri_ref[...], preferred_element_type=F32)
        pos_sc[...] = jnp.where(member, prefix - 1.0, -1.0)

    pos = pos_sc[pl.ds(pl.multiple_of(g * POS_ROWS, POS_ROWS), 1), :]
    e_row = lax.broadcasted_iota(jnp.int32, (ROUTE_LANES, EXPERTS_PER_GROUP * ff), 0)
    e_col = lax.broadcasted_iota(jnp.int32, (ROUTE_LANES, EXPERTS_PER_GROUP * ff), 1) // ff
    expand = (e_row == g * EXPERTS_PER_GROUP + e_col).astype(BF16)

    def sub_block(first_slot, rows):
        slot = lax.broadcasted_iota(jnp.int32, (rows, tm), 0) + first_slot
        sel = (pos == slot.astype(F32)).astype(BF16)
        xs = jnp.dot(sel, h_ref[...], preferred_element_type=F32).astype(BF16)
        wt = jnp.dot(sel, r_ref[...], preferred_element_type=F32).astype(BF16)
        wexp = jnp.dot(wt, expand, preferred_element_type=F32)
        y = None
        for e in range(EXPERTS_PER_GROUP):
            a = jnp.dot(xs, w1_ref[e], preferred_element_type=F32)
            gate = jnp.dot(xs, w3_ref[e], preferred_element_type=F32)
            mid = (_silu(a) * gate * wexp[:, e * ff:(e + 1) * ff]).astype(BF16)
            ye = jnp.dot(mid, w2_ref[e], preferred_element_type=F32)
            y = ye if y is None else y + ye
        acc_sc[...] += lax.dot_general(sel, y.astype(BF16), (((0,), (0,)), ((), ())),
                                       preferred_element_type=F32)

    count = cnt_ref[m * N_GROUPS + g]
    wide = jnp.logical_and(count > MOE_SUB, count <= MOE_WIDE[-1])
    n_full = jnp.where(wide, 0, count // MOE_SUB)
    rem = jnp.where(wide, 0, count - n_full * MOE_SUB)

    def full_block(sb, carry):
        sub_block(sb * MOE_SUB, MOE_SUB)
        return carry

    lax.fori_loop(0, n_full, full_block, 0)
    lo = 0
    for rows in MOE_TAILS:
        @pl.when(jnp.logical_and(rem > lo, rem <= rows))
        def _(rows=rows):
            sub_block(n_full * MOE_SUB, rows)
        lo = rows
    lo = MOE_SUB
    for rows in MOE_WIDE:
        @pl.when(jnp.logical_and(count > lo, count <= rows))
        def _(rows=rows):
            sub_block(jnp.int32(0), rows)
        lo = rows

    @pl.when(g == N_GROUPS - 1)
    def _():
        f_ref[...] = acc_sc[...].astype(f_ref.dtype)


def _moe(hf, route, grp, counts, tri, w1, w3, w2):
    n, d = hf.shape
    _, _, ff = w1.shape
    tm = tri.shape[0]
    epg = EXPERTS_PER_GROUP
    grid_spec = pltpu.PrefetchScalarGridSpec(
        num_scalar_prefetch=1, grid=(n // tm, N_GROUPS),
        in_specs=[pl.BlockSpec((tm, d), lambda i, g, c: (i, 0)),
                  pl.BlockSpec((tm, ROUTE_LANES), lambda i, g, c: (i, 0)),
                  pl.BlockSpec((8, tm), lambda i, g, c: (0, i)),
                  pl.BlockSpec((tm, tm), lambda i, g, c: (0, 0)),
                  pl.BlockSpec((epg, d, ff), lambda i, g, c: (g, 0, 0)),
                  pl.BlockSpec((epg, d, ff), lambda i, g, c: (g, 0, 0)),
                  pl.BlockSpec((epg, ff, d), lambda i, g, c: (g, 0, 0))],
        out_specs=pl.BlockSpec((tm, d), lambda i, g, c: (i, 0)),
        scratch_shapes=[pltpu.VMEM((tm, d), F32), pltpu.VMEM((POS_ROWS * N_GROUPS, tm), F32)])
    return pl.pallas_call(
        _moe_kernel, grid_spec=grid_spec,
        out_shape=jax.ShapeDtypeStruct((n, d), BF16),
        compiler_params=_cparams("parallel", "arbitrary"),
        name="moe_experts",
    )(counts, hf, route, grp, tri, w1, w3, w2)


def _final_kernel(x_ref, f_ref, mod_ref, w_ref, o_ref):
    x = x_ref[0] + mod_ref[0, 5:6, :] * f_ref[0].astype(F32)
    o_ref[0] = x * lax.rsqrt(jnp.mean(x * x, axis=-1, keepdims=True) + EPS) * w_ref[...]


def _final_norm(xs, f_prev, mods, w, *, n_ctx_tiles):
    b, t, d = xs.shape
    tm = TOKEN_TILE
    n_lat = t // tm - n_ctx_tiles
    lat = lambda i, j: (j, i + n_ctx_tiles, 0)
    return pl.pallas_call(
        _final_kernel, grid=(n_lat, b),
        in_specs=[pl.BlockSpec((1, tm, d), lat), pl.BlockSpec((1, tm, d), lat),
                  pl.BlockSpec((1, 6, d), lambda i, j: (j, 0, 0)), pl.BlockSpec((1, d), lambda i, j: (0, 0))],
        out_specs=pl.BlockSpec((1, tm, d), lambda i, j: (j, i, 0)),
        out_shape=jax.ShapeDtypeStruct((b, n_lat * tm, d), F32),
        compiler_params=_cparams("parallel", "parallel"),
        name="final_norm",
    )(xs, f_prev, mods, w)


def _rope_tables(n_ctx, n_lat):
    pos = jnp.arange(n_lat, dtype=jnp.int32)
    inv = ROPE_THETA ** (-jnp.arange(ROPE_NF, dtype=F32) / ROPE_NF)
    ang_r = (pos // GRID_W).astype(F32)[:, None] * inv
    ang_c = (pos % GRID_W).astype(F32)[:, None] * inv
    cos = jnp.concatenate([jnp.cos(ang_r)] * 2 + [jnp.cos(ang_c)] * 2, axis=-1)
    sin = jnp.concatenate([-jnp.sin(ang_r), jnp.sin(ang_r), -jnp.sin(ang_c), jnp.sin(ang_c)], axis=-1)
    cos = jnp.concatenate([jnp.ones((n_ctx, HEAD_DIM), F32), cos], axis=0)
    sin = jnp.concatenate([jnp.zeros((n_ctx, HEAD_DIM), F32), sin], axis=0)
    reps = QK_WIDTH // HEAD_DIM
    return jnp.tile(cos, (1, reps)), jnp.tile(sin, (1, reps))


def kernel(x, c, ctx, c_ctx, ada_w, ada_b, norm_mix_w, norm_ffn_w, w_in, q_norm_w, k_norm_w, conv_w,
           dn_A_log, dn_dt_bias, dn_norm_w, w_out, rg_w, rg_b, re_w, re_b, w1, w3, w2, final_norm_w):
    b, s, d = x.shape
    n_ctx = ctx.shape[1]
    depth = w_in.shape[0]
    t = n_ctx + s
    assert n_ctx % TOKEN_TILE == 0 and s % TOKEN_TILE == 0 and s % GRID_W == 0
    n_ctx_tiles = n_ctx // TOKEN_TILE

    xs = jnp.concatenate([ctx, x], axis=1)
    mod_rows = -(-(b + 1) // 8) * 8
    cs = jnp.zeros((mod_rows, d), F32).at[:b].set(c).at[b].set(c_ctx)
    mods = _ada_mods(cs, ada_w, ada_b).reshape(depth, mod_rows, 6, d)

    cos_t, sin_t = _rope_tables(n_ctx, s)
    bd_qk = _block_ones(QK_WIDTH, HEAD_DIM)
    bd_dn = _block_ones(DN_WIDTH, DN_DIM)
    zeros16 = jnp.zeros((2 * DN_HEADS,), F32)
    n_main = w_in.shape[2] - 4 * DN_HEADS
    lane_i = jnp.arange((DN_HEADS // DN_GROUP) * SCALAR_LANES)
    gi, li = lane_i // SCALAR_LANES, lane_i % SCALAR_LANES
    di, ki, hi = li // (2 * DN_GROUP), (li // DN_GROUP) % 2, li % DN_GROUP
    bg_used = li < 4 * DN_GROUP
    bg_src = jnp.where(bg_used, ki * 2 * DN_HEADS + di * DN_HEADS + gi * DN_GROUP + hi, 0)
    moe_tile = next(c for c in MOE_TILES if (b * t) % c == 0)
    tri = (lax.broadcasted_iota(jnp.int32, (moe_tile, moe_tile), 0)
           <= lax.broadcasted_iota(jnp.int32, (moe_tile, moe_tile), 1)).astype(BF16)

    f_prev = None
    for l in range(depth):
        qkw = jnp.concatenate([jnp.tile(q_norm_w[l], ATTN_HEADS), jnp.tile(k_norm_w[l], KV_HEADS)])[None]
        dnp = jnp.stack([jnp.concatenate([zeros16, dn_A_log[l].reshape(-1)]),
                         jnp.concatenate([zeros16, dn_dt_bias[l].reshape(-1)])])
        dnp = jnp.where(bg_used, jnp.take(dnp, bg_src, axis=1), 0.0)
        w_tail = jnp.where(bg_used, jnp.take(w_in[l][:, n_main:], bg_src, axis=1), 0.0)
        w_in_l = jnp.concatenate([w_in[l][:, :n_main], w_tail], axis=1).astype(BF16)
        outs = _inproj(xs, f_prev, mods[l - 1] if l else None, mods[l], norm_mix_w[l][None],
                       w_in_l, qkw, cos_t, sin_t, bd_qk, dnp, n_ctx_tiles=n_ctx_tiles)
        qt, k, vt, dqkv, gate, bg = outs[:6]
        if l:
            xs = outs[6]
        dq, dk, dv = _dn_conv(dqkv, conv_w[l], bd_dn, n_ctx_tiles=n_ctx_tiles)
        a = _attention(qt, k, vt, n_ctx=n_ctx)
        dsum = _deltanet(dq, dk, dv, bg, n_ctx=n_ctx)
        wo = w_out[l].astype(BF16)
        wr = jnp.zeros((d, ROUTE_LANES), F32).at[:, :N_EXPERTS].set(re_w[l]).at[
            :, N_EXPERTS:N_EXPERTS + N_GROUPS].set(rg_w[l]).astype(BF16)
        br = jnp.zeros((1, ROUTE_LANES), F32).at[0, :N_EXPERTS].set(re_b[l]).at[
            0, N_EXPERTS:N_EXPERTS + N_GROUPS].set(rg_b[l])
        xs, hf, route, grp, cnt = _outproj(
            a, dsum, gate, xs, mods[l], jnp.tile(dn_norm_w[l], DN_HEADS)[None], bd_dn,
            wo[:ATTN_WIDTH], wo[ATTN_WIDTH:], norm_ffn_w[l][None], wr, br, n_ctx_tiles=n_ctx_tiles)
        counts = cnt[:, :, 0, :N_GROUPS].reshape(-1, moe_tile // TOKEN_TILE, N_GROUPS).sum(axis=1)
        f_prev = _moe(hf.reshape(b * t, d), route.reshape(b * t, ROUTE_LANES), grp,
                      counts.astype(jnp.int32).reshape(-1), tri,
                      w1[l].astype(BF16), w3[l].astype(BF16), w2[l].astype(BF16)).reshape(b, t, d)
    return _final_norm(xs, f_prev, mods[depth - 1], final_norm_w[None], n_ctx_tiles=n_ctx_tiles)
```

```python
import functools
import math

import jax
import jax.numpy as jnp
from jax import lax
from jax.experimental import pallas as pl
from jax.experimental.pallas import tpu as pltpu

F32 = jnp.float32
BF16 = jnp.bfloat16
HIGHEST = lax.Precision.HIGHEST

GRID_W = 64
HEAD_DIM = 64
ATTN_HEADS = 8
KV_HEADS = 2
GQA_GROUP = ATTN_HEADS // KV_HEADS
ATTN_WIDTH = ATTN_HEADS * HEAD_DIM
KV_WIDTH = KV_HEADS * HEAD_DIM
ATTN_SCALE = HEAD_DIM ** -0.5
LOG2_E = math.log2(math.e)
ROPE_THETA = 10000.0
ROPE_NF = HEAD_DIM // 4
DN_HEADS = 8
DN_DIM = 64
DN_WIDTH = DN_HEADS * DN_DIM
DN_SCALE = DN_DIM ** -0.5
DN_GROUP = 4
CONV_K = 5
CHUNK = 64
N_GROUPS = 4
EXPERTS_PER_GROUP = 8
N_EXPERTS = N_GROUPS * EXPERTS_PER_GROUP
EPS = 1e-6
QK_WIDTH = ATTN_WIDTH + KV_WIDTH
TOKEN_TILE = 256
ATTN_KEY_TILE = 256
ONES_ROWS = 16
HEADS_PER_PASS = 8
HALO = 16
ROUTE_LANES = 128
MOE_TILES = (1024, 512, 256)
MOE_SUB = 256
MOE_TAILS = (64, 128, 256)
MOE_WIDE = (320, 384)
POS_ROWS = 16
VMEM_LIMIT = 56 * 1024 * 1024


def _cparams(*sem):
    return pltpu.CompilerParams(dimension_semantics=sem, vmem_limit_bytes=VMEM_LIMIT)


def _silu(x):
    return x * jax.nn.sigmoid(x)


def _block_ones(n, blk):
    i = lax.broadcasted_iota(jnp.int32, (n, n), 0) // blk
    j = lax.broadcasted_iota(jnp.int32, (n, n), 1) // blk
    return (i == j).astype(BF16)


def _group_mean_sq(x, ones_bd, width):
    return jnp.dot((x * x).astype(BF16), ones_bd, preferred_element_type=F32) * (1.0 / width)


def _ada_kernel(cs_ref, w_ref, b_ref, o_ref):
    o_ref[0] = jnp.dot(_silu(cs_ref[...]), w_ref[0], preferred_element_type=F32,
                       precision=HIGHEST) + b_ref[0]


def _ada_mods(cs, ada_w, ada_b):
    depth, d, n6 = ada_w.shape
    rows = cs.shape[0]
    tn = 1536
    return pl.pallas_call(
        _ada_kernel,
        grid=(depth, n6 // tn),
        in_specs=[pl.BlockSpec((rows, d), lambda l, j: (0, 0)),
                  pl.BlockSpec((1, d, tn), lambda l, j: (l, 0, j)),
                  pl.BlockSpec((1, 1, tn), lambda l, j: (l, 0, j))],
        out_specs=pl.BlockSpec((1, rows, tn), lambda l, j: (l, 0, j)),
        out_shape=jax.ShapeDtypeStruct((depth, rows, n6), F32),
        compiler_params=_cparams("parallel", "parallel"),
        name="ada_mods",
    )(cs, ada_w, ada_b.reshape(depth, 1, n6))


def _inproj_kernel(*refs, has_prev):
    if has_prev:
        (x_ref, f_ref, modp_ref, mod_ref, nw_ref, win_ref, qkw_ref, cos_ref, sin_ref, bd_ref, dnp_ref,
         qt_ref, k_ref, vt_ref, dqkv_ref, gate_ref, bg_ref, xo_ref) = refs
        x = x_ref[0] + modp_ref[0, 5:6, :] * f_ref[0].astype(F32)
        xo_ref[0] = x
    else:
        (x_ref, mod_ref, nw_ref, win_ref, qkw_ref, cos_ref, sin_ref, bd_ref, dnp_ref,
         qt_ref, k_ref, vt_ref, dqkv_ref, gate_ref, bg_ref) = refs
        x = x_ref[0]
    h = x * lax.rsqrt(jnp.mean(x * x, axis=-1, keepdims=True) + EPS) * nw_ref[...]
    h = h * (1.0 + mod_ref[0, 1:2, :]) + mod_ref[0, 0:1, :]
    acc = jnp.dot(h.astype(BF16), win_ref[...], preferred_element_type=F32)
    qk = acc[:, :QK_WIDTH]
    qn = qk * lax.rsqrt(_group_mean_sq(qk, bd_ref[...], HEAD_DIM) + EPS) * qkw_ref[...]
    lane = lax.broadcasted_iota(jnp.int32, qn.shape, 1)
    partner = jnp.where(lane % (2 * ROPE_NF) < ROPE_NF,
                        pltpu.roll(qn, QK_WIDTH - ROPE_NF, 1), pltpu.roll(qn, ROPE_NF, 1))
    qr = qn * cos_ref[...] + partner * sin_ref[...]
    qt_ref[0] = (qr[:, :ATTN_WIDTH] * (ATTN_SCALE * LOG2_E)).T.astype(BF16)
    k_ref[0] = qr[:, ATTN_WIDTH:].astype(BF16)
    c0 = QK_WIDTH
    vt_ref[0] = acc[:, c0:c0 + KV_WIDTH].T.astype(BF16)
    c0 += KV_WIDTH
    dqkv_ref[0] = acc[:, c0:c0 + 3 * DN_WIDTH].astype(BF16)
    c0 += 3 * DN_WIDTH
    gate_ref[0] = acc[:, c0:c0 + DN_WIDTH].astype(BF16)
    c0 += DN_WIDTH
    z = acc[:, c0:]
    zb = z + dnp_ref[1:2, :]
    softplus = jnp.maximum(zb, 0.0) + jnp.log1p(jnp.exp(-jnp.abs(zb)))
    lane_z = lax.broadcasted_iota(jnp.int32, z.shape, 1)
    bg = jnp.where(lane_z % (2 * DN_GROUP) < DN_GROUP, jax.nn.sigmoid(z), -jnp.exp(dnp_ref[0:1, :]) * softplus)
    for gi in range(DN_HEADS // DN_GROUP):
        bg_ref[0, gi] = bg[:, gi * SCALAR_LANES:(gi + 1) * SCALAR_LANES]


def _inproj(xs, f_prev, mods_prev, mods, nw, win, qkw, cos_t, sin_t, bd, dnp, *, n_ctx_tiles):
    b, t, d = xs.shape
    tm = TOKEN_TILE
    nt = t // tm
    proj = win.shape[1]
    has_prev = f_prev is not None
    tok = lambda i, j: (j, i, 0)
    modi = lambda i, j: (jnp.where(i < n_ctx_tiles, b, j), 0, 0)
    const = lambda i, j: (0, 0)
    in_specs = [pl.BlockSpec((1, tm, d), tok)]
    args = [xs]
    if has_prev:
        in_specs += [pl.BlockSpec((1, tm, d), tok), pl.BlockSpec((1, 6, d), modi)]
        args += [f_prev, mods_prev]
    in_specs += [pl.BlockSpec((1, 6, d), modi), pl.BlockSpec((1, d), const),
                 pl.BlockSpec((d, proj), const), pl.BlockSpec((1, QK_WIDTH), const),
                 pl.BlockSpec((tm, QK_WIDTH), lambda i, j: (i, 0)),
                 pl.BlockSpec((tm, QK_WIDTH), lambda i, j: (i, 0)),
                 pl.BlockSpec((QK_WIDTH, QK_WIDTH), const), pl.BlockSpec((2, dnp.shape[1]), const)]
    args += [mods, nw, win, qkw, cos_t, sin_t, bd, dnp]
    seq_out = lambda w, dt: (pl.BlockSpec((1, tm, w), tok), jax.ShapeDtypeStruct((b, t, w), dt))
    tr_out = lambda w: (pl.BlockSpec((1, w, tm), lambda i, j: (j, 0, i)), jax.ShapeDtypeStruct((b, w, t), BF16))
    ng = DN_HEADS // DN_GROUP
    outs = [tr_out(ATTN_WIDTH), seq_out(KV_WIDTH, BF16), tr_out(KV_WIDTH),
            seq_out(3 * DN_WIDTH, BF16), seq_out(DN_WIDTH, BF16),
            (pl.BlockSpec((1, ng, tm, SCALAR_LANES), lambda i, j: (j, 0, i, 0)),
             jax.ShapeDtypeStruct((b, ng, t, SCALAR_LANES), F32))]
    out_specs = [o[0] for o in outs]
    out_shape = [o[1] for o in outs]
    aliases = {}
    if has_prev:
        out_specs.append(pl.BlockSpec((1, tm, d), tok))
        out_shape.append(jax.ShapeDtypeStruct((b, t, d), F32))
        aliases = {0: len(out_shape) - 1}
    return pl.pallas_call(
        functools.partial(_inproj_kernel, has_prev=has_prev),
        grid=(nt, b), in_specs=in_specs, out_specs=out_specs, out_shape=out_shape,
        input_output_aliases=aliases,
        compiler_params=_cparams("parallel", "parallel"),
        name="inproj",
    )(*args)


def _attn_kernel(qt_ref, k_ref, vt_ref, o_ref, acc_sc, sa_sc, sb_sc, *, tk, n_ctx_q, n_ctx_k, n_all_k):
    tq = qt_ref.shape[2]
    n_kv = jnp.where(pl.program_id(1) < n_ctx_q, n_ctx_k, n_all_k)
    nh = HEADS_PER_PASS
    local = range(nh)
    last = n_kv - 1

    def tile_rows(i):
        return pl.ds(pl.multiple_of(i * tk, tk), tk)

    for h0 in range(0, ATTN_HEADS, nh):
        kv = [(h0 + j) // GQA_GROUP for j in local]
        acc_sc[...] = jnp.zeros(acc_sc.shape, F32)

        def scores(i, s_sc, h0=h0, kv=kv):
            rows = tile_rows(i)
            for j in local:
                s_sc[j] = jnp.dot(k_ref[0, rows, kv[j] * HEAD_DIM:(kv[j] + 1) * HEAD_DIM],
                                  qt_ref[0, (h0 + j) * HEAD_DIM:(h0 + j + 1) * HEAD_DIM, :],
                                  preferred_element_type=F32)

        def softmax_pv(i, s_sc, carry, kv=kv):
            rows = tile_rows(i)
            m_prev, l_prev = carry[:nh], carry[nh:]
            s = [s_sc[j] for j in local]
            m_new = [jnp.maximum(m_prev[j], jnp.max(s[j], axis=0, keepdims=True)) for j in local]
            p = [jnp.exp2(s[j] - m_new[j]) for j in local]
            alpha = [jnp.exp2(m_prev[j] - m_new[j]) for j in local]
            ones = jnp.ones((ONES_ROWS, tk), BF16)
            vt1 = {g: jnp.concatenate([vt_ref[0, g * HEAD_DIM:(g + 1) * HEAD_DIM, rows], ones], axis=0)
                   for g in sorted(set(kv))}
            pv = [jnp.dot(vt1[kv[j]], p[j].astype(BF16), preferred_element_type=F32) for j in local]
            for j in local:
                acc_sc[j] = alpha[j] * acc_sc[j] + pv[j][:HEAD_DIM]
            l_new = [alpha[j] * l_prev[j] + pv[j][HEAD_DIM:HEAD_DIM + 1] for j in local]
            return tuple(m_new) + tuple(l_new)

        init = (jnp.full((1, tq), -jnp.inf, F32),) * nh + (jnp.zeros((1, tq), F32),) * nh
        scores(0, sa_sc)
        scores(jnp.minimum(1, last), sb_sc)
        stats = softmax_pv(0, sa_sc, init)

        def pair(j, carry, scores=scores, softmax_pv=softmax_pv):
            a = 2 * j + 1
            scores(a + 1, sa_sc)
            carry = softmax_pv(a, sb_sc, carry)
            scores(jnp.minimum(a + 2, last), sb_sc)
            return softmax_pv(a + 1, sa_sc, carry)

        stats = lax.fori_loop(0, last // 2, pair, stats)
        og = jnp.concatenate([acc_sc[j] / stats[nh + j] for j in local], axis=0)
        o_ref[0, :, h0 * HEAD_DIM:(h0 + nh) * HEAD_DIM] = og.T.astype(BF16)


def _attention(qt, k, vt, *, n_ctx):
    b, _, t = qt.shape
    tq, tk = TOKEN_TILE, ATTN_KEY_TILE
    assert (n_ctx // tk) % 2 == 1 and (t // tk) % 2 == 1
    kern = functools.partial(_attn_kernel, tk=tk, n_ctx_q=n_ctx // tq, n_ctx_k=n_ctx // tk, n_all_k=t // tk)
    s_buf = pltpu.VMEM((HEADS_PER_PASS, tk, tq), F32)
    return pl.pallas_call(
        kern, grid=(b, t // tq),
        in_specs=[pl.BlockSpec((1, ATTN_WIDTH, tq), lambda bi, i: (bi, 0, i)),
                  pl.BlockSpec((1, t, KV_WIDTH), lambda bi, i: (bi, 0, 0)),
                  pl.BlockSpec((1, KV_WIDTH, t), lambda bi, i: (bi, 0, 0))],
        out_specs=pl.BlockSpec((1, tq, ATTN_WIDTH), lambda bi, i: (bi, i, 0)),
        out_shape=jax.ShapeDtypeStruct((b, t, ATTN_WIDTH), BF16),
        scratch_shapes=[pltpu.VMEM((HEADS_PER_PASS, HEAD_DIM, tq), F32), s_buf, s_buf],
        compiler_params=_cparams("parallel", "parallel"),
        name="gqa_attention",
    )(qt, k, vt)


DN_LANES = DN_GROUP * DN_DIM
DN_BLOCK = 4
INV_BLOCK = 16
SCALAR_LANES = 128


def _split3(x):
    x1 = x.astype(BF16)
    r1 = x - x1.astype(F32)
    x2 = r1.astype(BF16)
    x3 = (r1 - x2.astype(F32)).astype(BF16)
    return jnp.concatenate([x1, x2, x3], axis=1)


def _sum3(y):
    w = y.shape[1] // 3
    return y[:, :w] + y[:, w:2 * w] + y[:, 2 * w:]


def _dn_kernel(dq_ref, dk_ref, dv_ref, wq_ref, wk_ref, wv_ref, bg_ref, o_ref, mp_sc, n_sc, r_sc, gam_sc, *,
               n_chunks, n_ctx_chunks):
    c = CHUNK
    rows_b = DN_BLOCK * c
    ln = DN_LANES
    ri = lax.broadcasted_iota(jnp.int32, (rows_b, ln), 0)
    li = lax.broadcasted_iota(jnp.int32, (rows_b, ln), 1)
    same = (ri // c) == (li // c)
    i_in, j_in = ri % c, li % c
    eye_rc = i_in == j_in
    near = (i_in // INV_BLOCK) == (j_in // INV_BLOCK)
    e_row = lax.broadcasted_iota(jnp.int32, (3 * SCALAR_LANES, 2 * ln), 0) % SCALAR_LANES
    e_blk = lax.broadcasted_iota(jnp.int32, (3 * SCALAR_LANES, 2 * ln), 1) // c
    zl = lax.broadcasted_iota(jnp.int32, (rows_b, SCALAR_LANES), 1)
    nt_dims = (((1,), (1,)), ((), ()))
    tn_dims = (((0,), (0,)), ((), ()))

    def bdiag(x):
        return jnp.where(same, jnp.concatenate([x] * DN_GROUP, axis=0), jnp.zeros((), x.dtype))

    def fold(x):
        x = jnp.where(same, x, 0.0)
        return x[0:c] + x[c:2 * c] + x[2 * c:3 * c] + x[3 * c:4 * c]

    dirs = range(2)
    incl = [i_in >= j_in, i_in <= j_in]
    strict = [i_in > j_in, i_in < j_in]
    tri_bd = [jnp.logical_and(same, incl[d]).astype(BF16) for d in dirs]
    sel_last = [jnp.logical_and(same, j_in == (c - 1, 0)[d]).astype(BF16) for d in dirs]
    same_b = same.astype(BF16)
    col0 = [d * 2 * DN_GROUP for d in dirs]
    expand = [(e_row == e_blk + col0[d]).astype(BF16) for d in dirs]
    is_beta = [jnp.logical_and(zl >= col0[d], zl < col0[d] + DN_GROUP) for d in dirs]
    eye_f = eye_rc.astype(F32)
    chunks = range(DN_BLOCK)
    sls = [slice(ch * c, (ch + 1) * c) for ch in chunks]
    mm = lambda a, bmat: jnp.dot(a.astype(BF16), bmat, preferred_element_type=F32)

    def prepare(blk, carry):
        rs = pl.ds(pl.multiple_of(blk * rows_b, rows_b), rows_b)
        z = bg_ref[0, 0, rs, :]
        r0 = blk * rows_b
        keep_prev = jnp.where(jnp.logical_or(blk == 0, blk == n_ctx_chunks // DN_BLOCK), 0.0, 1.0)
        keep_next = jnp.where(jnp.logical_or(blk == n_ctx_chunks // DN_BLOCK - 1,
                                             blk == n_chunks // DN_BLOCK - 1), 0.0, 1.0)
        prev_rows = pl.ds(pl.multiple_of(jnp.maximum(r0 - HALO, 0), HALO), HALO)
        next_rows = pl.ds(pl.multiple_of(jnp.minimum(r0 + rows_b, n_chunks * c - HALO), HALO), HALO)

        def conv_silu(x_ref, w_ref):
            ext = jnp.concatenate([x_ref[0, prev_rows, :].astype(F32)[HALO - 8:] * keep_prev,
                                   x_ref[0, rs, :].astype(F32),
                                   x_ref[0, next_rows, :].astype(F32)[:8] * keep_next], axis=0)
            y = None
            for j in range(CONV_K):
                shift = (CONV_K // 2 - j) % (rows_b + 16)
                tap = ext if shift == 0 else pltpu.roll(ext, shift, 0)
                term = tap[8:8 + rows_b] * w_ref[j:j + 1, :]
                y = term if y is None else y + term
            return _silu(y)

        def l2n(x):
            return x * lax.rsqrt(jnp.dot((x * x).astype(BF16), same_b, preferred_element_type=F32) + EPS)

        q = l2n(conv_silu(dq_ref, wq_ref)) * DN_SCALE
        k = l2n(conv_silu(dk_ref, wk_ref))
        v = conv_silu(dv_ref, wv_ref)
        qbf = q.astype(BF16)
        kbf = k.astype(BF16)
        sc = [lax.dot_general(jnp.concatenate([kbf[sl], qbf[sl]], axis=0), bdiag(kbf[sl]), nt_dims,
                              preferred_element_type=F32) for sl in sls]
        lfull, intra, vb, kbe, q_dec, k_dec, gamma = [], [], [], [], [], [], []
        for d in dirs:
            cum = _sum3(jnp.dot(tri_bd[d], _split3(z), preferred_element_type=F32))
            ex = jnp.dot(_split3(jnp.where(is_beta[d], z, cum)), expand[d], preferred_element_type=F32)
            beta, gce = ex[:, :ln], ex[:, ln:]
            g_row = _sum3(jnp.dot(same_b, _split3(jnp.where(eye_rc, gce, 0.0)), preferred_element_type=F32))
            g_last = _sum3(jnp.dot(sel_last[d], _split3(gce), preferred_element_type=F32))
            decay = jnp.where(incl[d], jnp.exp(jnp.where(incl[d], gce - g_row, 0.0)), 0.0)
            bdecay = jnp.where(strict[d], beta * decay, 0.0)
            e_g = jnp.exp(gce)
            kb = k * beta
            kbe_d = (kb * e_g).astype(BF16)
            vb_d = (v * beta).astype(BF16)
            q_dec_d = q * e_g
            k_dec_d = (k * jnp.exp(g_last - gce)).astype(BF16)
            gamma_d = jnp.exp(g_last)
            lfull += [s[:c] * bdecay[sl] for s, sl in zip(sc, sls)]
            intra += [(s[c:] * decay[sl]).astype(BF16) for s, sl in zip(sc, sls)]
            vb += [vb_d[sl] for sl in sls]
            kbe += [kbe_d[sl] for sl in sls]
            q_dec += [q_dec_d[sl] for sl in sls]
            k_dec += [k_dec_d[sl] for sl in sls]
            gamma += [gamma_d[ch * c:ch * c + 8] for ch in chunks]
        near_c = near[sls[0]]
        p = [jnp.where(near_c, -lf, 0.0) for lf in lfull]
        l_off = [jnp.where(near_c, 0.0, lf).astype(BF16) for lf in lfull]
        tmat = [eye_f[sls[0]] + pi for pi in p]
        p = [mm(pi, bdiag(pi.astype(BF16))) for pi in p]
        for _ in range(int(math.log2(INV_BLOCK)) - 2):
            res = [mm(jnp.concatenate([ti, pi], axis=0), bdiag(pi.astype(BF16))) for ti, pi in zip(tmat, p)]
            tmat = [ti + ri[:c] for ti, ri in zip(tmat, res)]
            p = [ri[c:] for ri in res]
        tmat = [ti + mm(ti, bdiag(pi.astype(BF16))) for ti, pi in zip(tmat, p)]
        nmat = [mm(ti, bdiag(lo)) for ti, lo in zip(tmat, l_off)]
        n2 = [mm(ni, bdiag(ni.astype(BF16))) for ni in nmat]
        tmat = [ti + mm(qi, bdiag(ti.astype(BF16))) for ti, qi in zip(tmat, n2)]
        tmat = [ti - mm(ni, bdiag(ti.astype(BF16))) for ti, ni in zip(tmat, nmat)]
        uw = [mm(ti, jnp.concatenate([bdiag(vi), bdiag(ki)], axis=1)).astype(BF16)
              for ti, vi, ki in zip(tmat, vb, kbe)]
        aw_au = [mm(ai, jnp.concatenate([bdiag(x[:, ln:]), bdiag(x[:, :ln])], axis=1))
                 for ai, x in zip(intra, uw)]
        mn = [lax.dot_general(kd, jnp.concatenate([x[:, ln:], x[:, :ln]], axis=1), tn_dims,
                              preferred_element_type=F32) for kd, x in zip(k_dec, uw)]
        for d in dirs:
            for ch in chunks:
                st = d * DN_BLOCK + ch
                ci = blk * DN_BLOCK + ch
                p_c = q_dec[st] - aw_au[st][:, :ln]
                mp_sc[d, ci] = jnp.concatenate([fold(mn[st][:, :ln]), p_c], axis=0).astype(BF16)
                n_sc[d, ci] = fold(mn[st][:, ln:])
                r_sc[d, ci] = aw_au[st][:, ln:].astype(BF16)
                gam_sc[d, ci] = gamma[st]
        return carry

    lax.fori_loop(0, n_chunks // DN_BLOCK, prepare, 0)

    o_ref[...] = jnp.zeros(o_ref.shape, F32)

    def scan(s, states):
        ci = (s, jnp.where(s < n_ctx_chunks, n_ctx_chunks - 1 - s, n_chunks - 1 + n_ctx_chunks - s))
        res = [jnp.dot(mp_sc[d, ci[d]], bdiag(states[d].astype(BF16)), preferred_element_type=F32)
               for d in dirs]
        for d in dirs:
            rows = pl.ds(pl.multiple_of(ci[d] * c, c), c)
            o_ref[0, rows, :] += res[d][c:] + r_sc[d, ci[d]].astype(F32)
        return tuple(gam_sc[d, ci[d]][0:1] * states[d] - res[d][:c] + n_sc[d, ci[d]] for d in dirs)

    zero = jnp.zeros((c, ln), F32)
    lax.fori_loop(0, n_chunks, scan, (zero, zero))


def _deltanet(dqkv, conv_w, bg, *, n_ctx):
    b, t, _ = dqkv.shape
    n_chunks = t // CHUNK
    block_rows = DN_BLOCK * CHUNK
    assert t % block_rows == 0 and n_ctx % block_rows == 0 and block_rows == DN_LANES == DN_GROUP * DN_DIM
    ng = DN_HEADS // DN_GROUP
    once = dict(pipeline_mode=pl.Buffered(1))
    part = lambda s: pl.BlockSpec((1, t, DN_LANES), lambda bi, gi, s=s: (bi, 0, s * ng + gi), **once)
    wpart = lambda s: pl.BlockSpec((CONV_K, DN_LANES), lambda bi, gi, s=s: (0, s * ng + gi))
    kern = functools.partial(_dn_kernel, n_chunks=n_chunks, n_ctx_chunks=n_ctx // CHUNK)
    return pl.pallas_call(
        kern, grid=(b, ng),
        in_specs=[part(0), part(1), part(2), wpart(0), wpart(1), wpart(2),
                  pl.BlockSpec((1, 1, t, SCALAR_LANES), lambda bi, gi: (bi, gi, 0, 0), **once)],
        out_specs=pl.BlockSpec((1, t, DN_LANES), lambda bi, gi: (bi, 0, gi)),
        out_shape=jax.ShapeDtypeStruct((b, t, DN_WIDTH), F32),
        scratch_shapes=[pltpu.VMEM((2, n_chunks, 2 * CHUNK, DN_LANES), BF16),
                        pltpu.VMEM((2, n_chunks, CHUNK, DN_LANES), F32),
                        pltpu.VMEM((2, n_chunks, CHUNK, DN_LANES), BF16),
                        pltpu.VMEM((2, n_chunks, 8, DN_LANES), F32)],
        compiler_params=_cparams("parallel", "parallel"),
        name="gated_deltanet",
    )(dqkv, dqkv, dqkv, conv_w, conv_w, conv_w, bg)


def _outproj_kernel(a_ref, d_ref, gate_ref, x_ref, mod_ref, dnw_ref, bd_ref, woa_ref, wod_ref, nfw_ref,
                    wr_ref, br_ref, xo_ref, hf_ref, route_ref, grp_ref, cnt_ref):
    dd = d_ref[0]
    gate = gate_ref[0].astype(F32)
    dn = dd * lax.rsqrt(_group_mean_sq(dd, bd_ref[...], DN_DIM) + EPS) * dnw_ref[...] * _silu(gate)
    y = (jnp.dot(a_ref[0], woa_ref[...], preferred_element_type=F32)
         + jnp.dot(dn.astype(BF16), wod_ref[...], preferred_element_type=F32))
    x = x_ref[0] + mod_ref[0, 2:3, :] * y
    xo_ref[0] = x
    h = x * lax.rsqrt(jnp.mean(x * x, axis=-1, keepdims=True) + EPS) * nfw_ref[...]
    h = h * (1.0 + mod_ref[0, 4:5, :]) + mod_ref[0, 3:4, :]
    hb = h.astype(BF16)
    hf_ref[0] = hb
    logits = jnp.dot(hb, wr_ref[...], preferred_element_type=F32) + br_ref[...]
    lane = lax.broadcasted_iota(jnp.int32, logits.shape, 1)
    neg = jnp.float32(-jnp.inf)
    big = jnp.int32(ROUTE_LANES)

    def first_argmax(vals, vmax):
        return jnp.min(jnp.where(vals == vmax, lane, big), axis=-1, keepdims=True)

    is_g = jnp.logical_and(lane >= N_EXPERTS, lane < N_EXPERTS + N_GROUPS)
    gl = jnp.where(is_g, logits, neg)
    g_max = jnp.max(gl, axis=-1, keepdims=True)
    g_sel = first_argmax(gl, g_max) - N_EXPERTS
    p_g = 1.0 / jnp.sum(jnp.exp(gl - g_max), axis=-1, keepdims=True)
    e_lo = g_sel * EXPERTS_PER_GROUP
    in_grp = jnp.logical_and(lane >= e_lo, lane < e_lo + EXPERTS_PER_GROUP)
    el = jnp.where(in_grp, logits, neg)
    e_max = jnp.max(el, axis=-1, keepdims=True)
    i1 = first_argmax(el, e_max)
    el2 = jnp.where(lane == i1, neg, el)
    e_max2 = jnp.max(el2, axis=-1, keepdims=True)
    i2 = first_argmax(el2, e_max2)
    p2 = jnp.exp(e_max2 - e_max)
    w1 = p_g / (1.0 + p2)
    w2 = p_g * p2 / (1.0 + p2)
    route_ref[0] = jnp.where(lane == i1, w1, jnp.where(lane == i2, w2, 0.0)).astype(BF16)
    g_lanes = jnp.broadcast_to(g_sel.astype(F32), logits.shape)
    grp_ref[...] = g_lanes.T[:8]
    cnt_ref[0, 0] = jnp.broadcast_to(
        jnp.sum(jnp.where(lane == g_sel, 1.0, 0.0), axis=0, keepdims=True), (8, ROUTE_LANES))


def _outproj(a, dsum, gate, xs, mods, dnw, bd, woa, wod, nfw, wr, br, *, n_ctx_tiles):
    b, t, d = xs.shape
    tm = TOKEN_TILE
    tok = lambda i, j: (j, i, 0)
    const = lambda i, j: (0, 0)
    modi = lambda i, j: (jnp.where(i < n_ctx_tiles, b, j), 0, 0)
    return pl.pallas_call(
        _outproj_kernel, grid=(t // tm, b),
        in_specs=[pl.BlockSpec((1, tm, ATTN_WIDTH), tok), pl.BlockSpec((1, tm, DN_WIDTH), tok),
                  pl.BlockSpec((1, tm, DN_WIDTH), tok), pl.BlockSpec((1, tm, d), tok),
                  pl.BlockSpec((1, 6, d), modi), pl.BlockSpec((1, DN_WIDTH), const),
                  pl.BlockSpec((DN_WIDTH, DN_WIDTH), const), pl.BlockSpec((ATTN_WIDTH, d), const),
                  pl.BlockSpec((DN_WIDTH, d), const), pl.BlockSpec((1, d), const),
                  pl.BlockSpec((d, ROUTE_LANES), const), pl.BlockSpec((1, ROUTE_LANES), const)],
        out_specs=[pl.BlockSpec((1, tm, d), tok), pl.BlockSpec((1, tm, d), tok),
                   pl.BlockSpec((1, tm, ROUTE_LANES), tok),
                   pl.BlockSpec((8, tm), lambda i, j: (0, j * (t // tm) + i)),
                   pl.BlockSpec((1, 1, 8, ROUTE_LANES), lambda i, j: (j, i, 0, 0))],
        out_shape=[jax.ShapeDtypeStruct((b, t, d), F32), jax.ShapeDtypeStruct((b, t, d), BF16),
                   jax.ShapeDtypeStruct((b, t, ROUTE_LANES), BF16),
                   jax.ShapeDtypeStruct((8, b * t), F32),
                   jax.ShapeDtypeStruct((b, t // tm, 8, ROUTE_LANES), F32)],
        input_output_aliases={3: 0},
        compiler_params=_cparams("parallel", "parallel"),
        name="outproj_router",
    )(a, dsum, gate, xs, mods, dnw, bd, woa, wod, nfw, wr, br)


def _moe_kernel(cnt_ref, h_ref, r_ref, grp_ref, tri_ref, w1_ref, w3_ref, w2_ref, f_ref, acc_sc, pos_sc):
    m, g = pl.program_id(0), pl.program_id(1)
    tm = h_ref.shape[0]
    ff = w1_ref.shape[2]

    @pl.when(g == 0)
    def _():
        acc_sc[...] = jnp.zeros(acc_sc.shape, F32)
        grp = jnp.broadcast_to(grp_ref[0:1, :], (POS_ROWS * N_GROUPS, tm))
        row_g = lax.broadcasted_iota(jnp.int32, grp.shape, 0) // POS_ROWS
        member = grp == row_g.astype(F32)
        prefix = jnp.dot(member.astype(BF16), tri_ref[...], preferred_element_type=F32)
        pos_sc[...] = jnp.where(member, prefix - 1.0, -1.0)

    pos = pos_sc[pl.ds(pl.multiple_of(g * POS_ROWS, POS_ROWS), 1), :]
    e_row = lax.broadcasted_iota(jnp.int32, (ROUTE_LANES, EXPERTS_PER_GROUP * ff), 0)
    e_col = lax.broadcasted_iota(jnp.int32, (ROUTE_LANES, EXPERTS_PER_GROUP * ff), 1) // ff
    expand = (e_row == g * EXPERTS_PER_GROUP + e_col).astype(BF16)

    def sub_block(first_slot, rows):
        slot = lax.broadcasted_iota(jnp.int32, (rows, tm), 0) + first_slot
        sel = (pos == slot.astype(F32)).astype(BF16)
        xs = jnp.dot(sel, h_ref[...], preferred_element_type=F32).astype(BF16)
        wt = jnp.dot(sel, r_ref[...], preferred_element_type=F32).astype(BF16)
        wexp = jnp.dot(wt, expand, preferred_element_type=F32)
        y = None
        for e in range(EXPERTS_PER_GROUP):
            a = jnp.dot(xs, w1_ref[e], preferred_element_type=F32)
            gate = jnp.dot(xs, w3_ref[e], preferred_element_type=F32)
            mid = (_silu(a) * gate * wexp[:, e * ff:(e + 1) * ff]).astype(BF16)
            ye = jnp.dot(mid, w2_ref[e], preferred_element_type=F32)
            y = ye if y is None else y + ye
        acc_sc[...] += lax.dot_general(sel, y.astype(BF16), (((0,), (0,)), ((), ())),
                                       preferred_element_type=F32)

    count = cnt_ref[m * N_GROUPS + g]
    wide = jnp.logical_and(count > MOE_SUB, count <= MOE_WIDE[-1])
    n_full = jnp.where(wide, 0, count // MOE_SUB)
    rem = jnp.where(wide, 0, count - n_full * MOE_SUB)

    def full_block(sb, carry):
        sub_block(sb * MOE_SUB, MOE_SUB)
        return carry

    lax.fori_loop(0, n_full, full_block, 0)
    lo = 0
    for rows in MOE_TAILS:
        @pl.when(jnp.logical_and(rem > lo, rem <= rows))
        def _(rows=rows):
            sub_block(n_full * MOE_SUB, rows)
        lo = rows
    lo = MOE_SUB
    for rows in MOE_WIDE:
        @pl.when(jnp.logical_and(count > lo, count <= rows))
        def _(rows=rows):
            sub_block(jnp.int32(0), rows)
        lo = rows

    @pl.when(g == N_GROUPS - 1)
    def _():
        f_ref[...] = acc_sc[...].astype(f_ref.dtype)


def _moe(hf, route, grp, counts, tri, w1, w3, w2):
    n, d = hf.shape
    _, _, ff = w1.shape
    tm = tri.shape[0]
    epg = EXPERTS_PER_GROUP
    grid_spec = pltpu.PrefetchScalarGridSpec(
        num_scalar_prefetch=1, grid=(n // tm, N_GROUPS),
        in_specs=[pl.BlockSpec((tm, d), lambda i, g, c: (i, 0)),
                  pl.BlockSpec((tm, ROUTE_LANES), lambda i, g, c: (i, 0)),
                  pl.BlockSpec((8, tm), lambda i, g, c: (0, i)),
                  pl.BlockSpec((tm, tm), lambda i, g, c: (0, 0)),
                  pl.BlockSpec((epg, d, ff), lambda i, g, c: (g, 0, 0)),
                  pl.BlockSpec((epg, d, ff), lambda i, g, c: (g, 0, 0)),
                  pl.BlockSpec((epg, ff, d), lambda i, g, c: (g, 0, 0))],
        out_specs=pl.BlockSpec((tm, d), lambda i, g, c: (i, 0)),
        scratch_shapes=[pltpu.VMEM((tm, d), F32), pltpu.VMEM((POS_ROWS * N_GROUPS, tm), F32)])
    return pl.pallas_call(
        _moe_kernel, grid_spec=grid_spec,
        out_shape=jax.ShapeDtypeStruct((n, d), BF16),
        compiler_params=_cparams("parallel", "arbitrary"),
        name="moe_experts",
    )(counts, hf, route, grp, tri, w1, w3, w2)


def _final_kernel(x_ref, f_ref, mod_ref, w_ref, o_ref):
    x = x_ref[0] + mod_ref[0, 5:6, :] * f_ref[0].astype(F32)
    o_ref[0] = x * lax.rsqrt(jnp.mean(x * x, axis=-1, keepdims=True) + EPS) * w_ref[...]


def _final_norm(xs, f_prev, mods, w, *, n_ctx_tiles):
    b, t, d = xs.shape
    tm = TOKEN_TILE
    n_lat = t // tm - n_ctx_tiles
    lat = lambda i, j: (j, i + n_ctx_tiles, 0)
    return pl.pallas_call(
        _final_kernel, grid=(n_lat, b),
        in_specs=[pl.BlockSpec((1, tm, d), lat), pl.BlockSpec((1, tm, d), lat),
                  pl.BlockSpec((1, 6, d), lambda i, j: (j, 0, 0)), pl.BlockSpec((1, d), lambda i, j: (0, 0))],
        out_specs=pl.BlockSpec((1, tm, d), lambda i, j: (j, i, 0)),
        out_shape=jax.ShapeDtypeStruct((b, n_lat * tm, d), F32),
        compiler_params=_cparams("parallel", "parallel"),
        name="final_norm",
    )(xs, f_prev, mods, w)


def _rope_tables(n_ctx, n_lat):
    pos = jnp.arange(n_lat, dtype=jnp.int32)
    inv = ROPE_THETA ** (-jnp.arange(ROPE_NF, dtype=F32) / ROPE_NF)
    ang_r = (pos // GRID_W).astype(F32)[:, None] * inv
    ang_c = (pos % GRID_W).astype(F32)[:, None] * inv
    cos = jnp.concatenate([jnp.cos(ang_r)] * 2 + [jnp.cos(ang_c)] * 2, axis=-1)
    sin = jnp.concatenate([-jnp.sin(ang_r), jnp.sin(ang_r), -jnp.sin(ang_c), jnp.sin(ang_c)], axis=-1)
    cos = jnp.concatenate([jnp.ones((n_ctx, HEAD_DIM), F32), cos], axis=0)
    sin = jnp.concatenate([jnp.zeros((n_ctx, HEAD_DIM), F32), sin], axis=0)
    reps = QK_WIDTH // HEAD_DIM
    return jnp.tile(cos, (1, reps)), jnp.tile(sin, (1, reps))


def kernel(x, c, ctx, c_ctx, ada_w, ada_b, norm_mix_w, norm_ffn_w, w_in, q_norm_w, k_norm_w, conv_w,
           dn_A_log, dn_dt_bias, dn_norm_w, w_out, rg_w, rg_b, re_w, re_b, w1, w3, w2, final_norm_w):
    b, s, d = x.shape
    n_ctx = ctx.shape[1]
    depth = w_in.shape[0]
    t = n_ctx + s
    assert n_ctx % TOKEN_TILE == 0 and s % TOKEN_TILE == 0 and s % GRID_W == 0
    n_ctx_tiles = n_ctx // TOKEN_TILE

    xs = jnp.concatenate([ctx, x], axis=1)
    mod_rows = -(-(b + 1) // 8) * 8
    cs = jnp.zeros((mod_rows, d), F32).at[:b].set(c).at[b].set(c_ctx)
    mods = _ada_mods(cs, ada_w, ada_b).reshape(depth, mod_rows, 6, d)

    cos_t, sin_t = _rope_tables(n_ctx, s)
    bd_qk = _block_ones(QK_WIDTH, HEAD_DIM)
    bd_dn = _block_ones(DN_WIDTH, DN_DIM)
    zeros16 = jnp.zeros((2 * DN_HEADS,), F32)
    n_main = w_in.shape[2] - 4 * DN_HEADS
    lane_i = jnp.arange((DN_HEADS // DN_GROUP) * SCALAR_LANES)
    gi, li = lane_i // SCALAR_LANES, lane_i % SCALAR_LANES
    di, ki, hi = li // (2 * DN_GROUP), (li // DN_GROUP) % 2, li % DN_GROUP
    bg_used = li < 4 * DN_GROUP
    bg_src = jnp.where(bg_used, ki * 2 * DN_HEADS + di * DN_HEADS + gi * DN_GROUP + hi, 0)
    moe_tile = next(c for c in MOE_TILES if (b * t) % c == 0)
    tri = (lax.broadcasted_iota(jnp.int32, (moe_tile, moe_tile), 0)
           <= lax.broadcasted_iota(jnp.int32, (moe_tile, moe_tile), 1)).astype(BF16)

    f_prev = None
    for l in range(depth):
        qkw = jnp.concatenate([jnp.tile(q_norm_w[l], ATTN_HEADS), jnp.tile(k_norm_w[l], KV_HEADS)])[None]
        dnp = jnp.stack([jnp.concatenate([zeros16, dn_A_log[l].reshape(-1)]),
                         jnp.concatenate([zeros16, dn_dt_bias[l].reshape(-1)])])
        dnp = jnp.where(bg_used, jnp.take(dnp, bg_src, axis=1), 0.0)
        w_tail = jnp.where(bg_used, jnp.take(w_in[l][:, n_main:], bg_src, axis=1), 0.0)
        w_in_l = jnp.concatenate([w_in[l][:, :n_main], w_tail], axis=1).astype(BF16)
        outs = _inproj(xs, f_prev, mods[l - 1] if l else None, mods[l], norm_mix_w[l][None],
                       w_in_l, qkw, cos_t, sin_t, bd_qk, dnp, n_ctx_tiles=n_ctx_tiles)
        qt, k, vt, dqkv, gate, bg = outs[:6]
        if l:
            xs = outs[6]
        a = _attention(qt, k, vt, n_ctx=n_ctx)
        dsum = _deltanet(dqkv, conv_w[l], bg, n_ctx=n_ctx)
        wo = w_out[l].astype(BF16)
        wr = jnp.zeros((d, ROUTE_LANES), F32).at[:, :N_EXPERTS].set(re_w[l]).at[
            :, N_EXPERTS:N_EXPERTS + N_GROUPS].set(rg_w[l]).astype(BF16)
        br = jnp.zeros((1, ROUTE_LANES), F32).at[0, :N_EXPERTS].set(re_b[l]).at[
            0, N_EXPERTS:N_EXPERTS + N_GROUPS].set(rg_b[l])
        xs, hf, route, grp, cnt = _outproj(
            a, dsum, gate, xs, mods[l], jnp.tile(dn_norm_w[l], DN_HEADS)[None], bd_dn,
            wo[:ATTN_WIDTH], wo[ATTN_WIDTH:], norm_ffn_w[l][None], wr, br, n_ctx_tiles=n_ctx_tiles)
        counts = cnt[:, :, 0, :N_GROUPS].reshape(-1, moe_tile // TOKEN_TILE, N_GROUPS).sum(axis=1)
        f_prev = _moe(hf.reshape(b * t, d), route.reshape(b * t, ROUTE_LANES), grp,
                      counts.astype(jnp.int32).reshape(-1), tri,
                      w1[l].astype(BF16), w3[l].astype(BF16), w2[l].astype(BF16)).reshape(b, t, d)
    return _final_norm(xs, f_prev, mods[depth - 1], final_norm_w[None], n_ctx_tiles=n_ctx_tiles)
```

```python
import functools
import math

import jax
import jax.numpy as jnp
from jax import lax
from jax.experimental import pallas as pl
from jax.experimental.pallas import tpu as pltpu

F32 = jnp.float32
BF16 = jnp.bfloat16
HIGHEST = lax.Precision.HIGHEST

GRID_W = 64
HEAD_DIM = 64
ATTN_HEADS = 8
KV_HEADS = 2
GQA_GROUP = ATTN_HEADS // KV_HEADS
ATTN_WIDTH = ATTN_HEADS * HEAD_DIM
KV_WIDTH = KV_HEADS * HEAD_DIM
ATTN_SCALE = HEAD_DIM ** -0.5
LOG2_E = math.log2(math.e)
ROPE_THETA = 10000.0
ROPE_NF = HEAD_DIM // 4
DN_HEADS = 8
DN_DIM = 64
DN_WIDTH = DN_HEADS * DN_DIM
DN_SCALE = DN_DIM ** -0.5
DN_GROUP = 4
CONV_K = 5
CHUNK = 64
N_GROUPS = 4
EXPERTS_PER_GROUP = 8
N_EXPERTS = N_GROUPS * EXPERTS_PER_GROUP
EPS = 1e-6
QK_WIDTH = ATTN_WIDTH + KV_WIDTH
SUBLANES = 8
TOKEN_TILE = 256
BATCH_PER_STEP = 2
ATTN_KEY_TILE = 256
ONES_ROWS = 16
HEADS_PER_PASS = 8
HALO = 16
ROUTE_LANES = 128
MOE_TILES = (1024, 512, 256)
MOE_SUB = 256
MOE_TAILS = (64, 128, 256)
MOE_WIDE = (320, 384)
POS_ROWS = 16
VMEM_LIMIT = 56 * 1024 * 1024


def _cparams(*sem):
    return pltpu.CompilerParams(dimension_semantics=sem, vmem_limit_bytes=VMEM_LIMIT)


def _silu(x):
    return x * jax.nn.sigmoid(x)


def _block_ones(n, blk):
    i = lax.broadcasted_iota(jnp.int32, (n, n), 0) // blk
    j = lax.broadcasted_iota(jnp.int32, (n, n), 1) // blk
    return (i == j).astype(BF16)


def _group_mean_sq(x, ones_bd, width):
    return jnp.dot((x * x).astype(BF16), ones_bd, preferred_element_type=F32) * (1.0 / width)


def _ada_kernel(cs_ref, w_ref, b_ref, o_ref):
    o_ref[0] = jnp.dot(_silu(cs_ref[...]), w_ref[0], preferred_element_type=F32,
                       precision=HIGHEST) + b_ref[0]


def _ada_mods(cs, ada_w, ada_b):
    depth, d, n6 = ada_w.shape
    rows = cs.shape[0]
    tn = 1536
    return pl.pallas_call(
        _ada_kernel,
        grid=(depth, n6 // tn),
        in_specs=[pl.BlockSpec((rows, d), lambda l, j: (0, 0)),
                  pl.BlockSpec((1, d, tn), lambda l, j: (l, 0, j)),
                  pl.BlockSpec((1, 1, tn), lambda l, j: (l, 0, j))],
        out_specs=pl.BlockSpec((1, rows, tn), lambda l, j: (l, 0, j)),
        out_shape=jax.ShapeDtypeStruct((depth, rows, n6), F32),
        compiler_params=_cparams("parallel", "parallel"),
        name="ada_mods",
    )(cs, ada_w, ada_b.reshape(depth, 1, n6))


def _inproj_kernel(*refs, has_prev):
    if has_prev:
        (x_ref, f_ref, modp_ref, mod_ref, nw_ref, win_ref, qkw_ref, cos_ref, sin_ref, bd_ref, dnp_ref,
         qt_ref, k_ref, vt_ref, dqkv_ref, gate_ref, bg_ref, xo_ref) = refs
    else:
        (x_ref, mod_ref, nw_ref, win_ref, qkw_ref, cos_ref, sin_ref, bd_ref, dnp_ref,
         qt_ref, k_ref, vt_ref, dqkv_ref, gate_ref, bg_ref) = refs
    nb, tm = x_ref.shape[0], x_ref.shape[1]
    hs = []
    for r in range(nb):
        x = x_ref[r]
        if has_prev:
            x = x + modp_ref[r, 5:6, :] * f_ref[r].astype(F32)
            xo_ref[r] = x
        h = x * lax.rsqrt(jnp.mean(x * x, axis=-1, keepdims=True) + EPS) * nw_ref[...]
        hs.append((h * (1.0 + mod_ref[r, 1:2, :]) + mod_ref[r, 0:1, :]).astype(BF16))
    acc = jnp.dot(jnp.concatenate(hs, axis=0), win_ref[...], preferred_element_type=F32)
    qk = acc[:, :QK_WIDTH]
    qn = qk * lax.rsqrt(_group_mean_sq(qk, bd_ref[...], HEAD_DIM) + EPS) * qkw_ref[...]
    lane = lax.broadcasted_iota(jnp.int32, qn.shape, 1)
    partner = jnp.where(lane % (2 * ROPE_NF) < ROPE_NF,
                        pltpu.roll(qn, QK_WIDTH - ROPE_NF, 1), pltpu.roll(qn, ROPE_NF, 1))
    qr = (qn * jnp.concatenate([cos_ref[...]] * nb, axis=0)
          + partner * jnp.concatenate([sin_ref[...]] * nb, axis=0))
    c_v = QK_WIDTH
    c_dn = c_v + KV_WIDTH
    c_gate = c_dn + 3 * DN_WIDTH
    c_bg = c_gate + DN_WIDTH
    z = acc[:, c_bg:]
    zb = z + dnp_ref[1:2, :]
    softplus = jnp.maximum(zb, 0.0) + jnp.log1p(jnp.exp(-jnp.abs(zb)))
    lane_z = lax.broadcasted_iota(jnp.int32, z.shape, 1)
    bg = jnp.where(lane_z % (2 * DN_GROUP) < DN_GROUP, jax.nn.sigmoid(z), -jnp.exp(dnp_ref[0:1, :]) * softplus)
    for r in range(nb):
        rs = slice(r * tm, (r + 1) * tm)
        qt_ref[r] = (qr[rs, :ATTN_WIDTH] * (ATTN_SCALE * LOG2_E)).T.astype(BF16)
        k_ref[r] = qr[rs, ATTN_WIDTH:].astype(BF16)
        vt_ref[r] = acc[rs, c_v:c_dn].T.astype(BF16)
        dqkv_ref[r] = acc[rs, c_dn:c_gate].astype(BF16)
        gate_ref[r] = acc[rs, c_gate:c_bg].astype(BF16)
        for gi in range(DN_HEADS // DN_GROUP):
            bg_ref[r, gi] = bg[rs, gi * SCALAR_LANES:(gi + 1) * SCALAR_LANES]


def _inproj(xs, f_prev, mods_prev, mods, nw, win, qkw, cos_t, sin_t, bd, dnp, *, n_ctx_tiles):
    b, t, d = xs.shape
    tm = TOKEN_TILE
    nb = BATCH_PER_STEP
    assert b % nb == 0
    nt = t // tm
    proj = win.shape[1]
    has_prev = f_prev is not None
    tok = lambda i, j: (j, i, 0)
    modi = lambda i, j: (jnp.where(i < n_ctx_tiles, b // nb, j), 0, 0)
    const = lambda i, j: (0, 0)
    in_specs = [pl.BlockSpec((nb, tm, d), tok)]
    args = [xs]
    if has_prev:
        in_specs += [pl.BlockSpec((nb, tm, d), tok), pl.BlockSpec((nb, 6, d), modi)]
        args += [f_prev, mods_prev]
    in_specs += [pl.BlockSpec((nb, 6, d), modi), pl.BlockSpec((1, d), const),
                 pl.BlockSpec((d, proj), const), pl.BlockSpec((1, QK_WIDTH), const),
                 pl.BlockSpec((tm, QK_WIDTH), lambda i, j: (i, 0)),
                 pl.BlockSpec((tm, QK_WIDTH), lambda i, j: (i, 0)),
                 pl.BlockSpec((QK_WIDTH, QK_WIDTH), const), pl.BlockSpec((2, dnp.shape[1]), const)]
    args += [mods, nw, win, qkw, cos_t, sin_t, bd, dnp]
    seq_out = lambda w, dt: (pl.BlockSpec((nb, tm, w), tok), jax.ShapeDtypeStruct((b, t, w), dt))
    tr_out = lambda w: (pl.BlockSpec((nb, w, tm), lambda i, j: (j, 0, i)), jax.ShapeDtypeStruct((b, w, t), BF16))
    ng = DN_HEADS // DN_GROUP
    outs = [tr_out(ATTN_WIDTH), seq_out(KV_WIDTH, BF16), tr_out(KV_WIDTH),
            seq_out(3 * DN_WIDTH, BF16), seq_out(DN_WIDTH, BF16),
            (pl.BlockSpec((nb, ng, tm, SCALAR_LANES), lambda i, j: (j, 0, i, 0)),
             jax.ShapeDtypeStruct((b, ng, t, SCALAR_LANES), F32))]
    out_specs = [o[0] for o in outs]
    out_shape = [o[1] for o in outs]
    aliases = {}
    if has_prev:
        out_specs.append(pl.BlockSpec((nb, tm, d), tok))
        out_shape.append(jax.ShapeDtypeStruct((b, t, d), F32))
        aliases = {0: len(out_shape) - 1}
    return pl.pallas_call(
        functools.partial(_inproj_kernel, has_prev=has_prev),
        grid=(nt, b // nb), in_specs=in_specs, out_specs=out_specs, out_shape=out_shape,
        input_output_aliases=aliases,
        compiler_params=_cparams("parallel", "parallel"),
        name="inproj",
    )(*args)


def _attn_kernel(qt_ref, k_ref, vt_ref, o_ref, acc_sc, sa_sc, sb_sc, *, tk, n_ctx_q, n_ctx_k, n_all_k):
    tq = qt_ref.shape[2]
    n_kv = jnp.where(pl.program_id(1) < n_ctx_q, n_ctx_k, n_all_k)
    nh = HEADS_PER_PASS
    local = range(nh)
    last = n_kv - 1

    def tile_rows(i):
        return pl.ds(pl.multiple_of(i * tk, tk), tk)

    for h0 in range(0, ATTN_HEADS, nh):
        kv = [(h0 + j) // GQA_GROUP for j in local]
        acc_sc[...] = jnp.zeros(acc_sc.shape, F32)

        def scores(i, s_sc, h0=h0, kv=kv):
            rows = tile_rows(i)
            for j in local:
                s_sc[j] = jnp.dot(k_ref[0, rows, kv[j] * HEAD_DIM:(kv[j] + 1) * HEAD_DIM],
                                  qt_ref[0, (h0 + j) * HEAD_DIM:(h0 + j + 1) * HEAD_DIM, :],
                                  preferred_element_type=F32)

        def softmax_pv(i, s_sc, carry, kv=kv):
            rows = tile_rows(i)
            m_prev, l_prev = carry[:nh], carry[nh:]
            s = [s_sc[j] for j in local]
            m_new = [jnp.maximum(m_prev[j], jnp.max(s[j], axis=0, keepdims=True)) for j in local]
            p = [jnp.exp2(s[j] - m_new[j]) for j in local]
            alpha = [jnp.exp2(m_prev[j] - m_new[j]) for j in local]
            ones = jnp.ones((ONES_ROWS, tk), BF16)
            vt1 = {g: jnp.concatenate([vt_ref[0, g * HEAD_DIM:(g + 1) * HEAD_DIM, rows], ones], axis=0)
                   for g in sorted(set(kv))}
            pv = [jnp.dot(vt1[kv[j]], p[j].astype(BF16), preferred_element_type=F32) for j in local]
            for j in local:
                acc_sc[j] = alpha[j] * acc_sc[j] + pv[j][:HEAD_DIM]
            l_new = [alpha[j] * l_prev[j] + pv[j][HEAD_DIM:HEAD_DIM + 1] for j in local]
            return tuple(m_new) + tuple(l_new)

        init = (jnp.full((1, tq), -jnp.inf, F32),) * nh + (jnp.zeros((1, tq), F32),) * nh
        scores(0, sa_sc)
        scores(jnp.minimum(1, last), sb_sc)
        stats = softmax_pv(0, sa_sc, init)

        def pair(j, carry, scores=scores, softmax_pv=softmax_pv):
            a = 2 * j + 1
            scores(a + 1, sa_sc)
            carry = softmax_pv(a, sb_sc, carry)
            scores(jnp.minimum(a + 2, last), sb_sc)
            return softmax_pv(a + 1, sa_sc, carry)

        stats = lax.fori_loop(0, last // 2, pair, stats)
        og = jnp.concatenate([acc_sc[j] / stats[nh + j] for j in local], axis=0)
        o_ref[0, :, h0 * HEAD_DIM:(h0 + nh) * HEAD_DIM] = og.T.astype(BF16)


def _attention(qt, k, vt, *, n_ctx):
    b, _, t = qt.shape
    tq, tk = TOKEN_TILE, ATTN_KEY_TILE
    assert (n_ctx // tk) % 2 == 1 and (t // tk) % 2 == 1
    kern = functools.partial(_attn_kernel, tk=tk, n_ctx_q=n_ctx // tq, n_ctx_k=n_ctx // tk, n_all_k=t // tk)
    s_buf = pltpu.VMEM((HEADS_PER_PASS, tk, tq), F32)
    return pl.pallas_call(
        kern, grid=(b, t // tq),
        in_specs=[pl.BlockSpec((1, ATTN_WIDTH, tq), lambda bi, i: (bi, 0, i)),
                  pl.BlockSpec((1, t, KV_WIDTH), lambda bi, i: (bi, 0, 0)),
                  pl.BlockSpec((1, KV_WIDTH, t), lambda bi, i: (bi, 0, 0))],
        out_specs=pl.BlockSpec((1, tq, ATTN_WIDTH), lambda bi, i: (bi, i, 0)),
        out_shape=jax.ShapeDtypeStruct((b, t, ATTN_WIDTH), BF16),
        scratch_shapes=[pltpu.VMEM((HEADS_PER_PASS, HEAD_DIM, tq), F32), s_buf, s_buf],
        compiler_params=_cparams("parallel", "parallel"),
        name="gqa_attention",
    )(qt, k, vt)


DN_LANES = DN_GROUP * DN_DIM
DN_BLOCK = 4
INV_BLOCK = 16
SCALAR_LANES = 128


def _split3(x):
    x1 = x.astype(BF16)
    r1 = x - x1.astype(F32)
    x2 = r1.astype(BF16)
    x3 = (r1 - x2.astype(F32)).astype(BF16)
    return jnp.concatenate([x1, x2, x3], axis=1)


def _sum3(y):
    w = y.shape[1] // 3
    return y[:, :w] + y[:, w:2 * w] + y[:, 2 * w:]


def _dn_kernel(dq_ref, dk_ref, dv_ref, wq_ref, wk_ref, wv_ref, bg_ref, o_ref, mp_sc, n_sc, r_sc, gam_sc, *,
               n_chunks, n_ctx_chunks):
    c = CHUNK
    rows_b = DN_BLOCK * c
    ln = DN_LANES
    ri = lax.broadcasted_iota(jnp.int32, (rows_b, ln), 0)
    li = lax.broadcasted_iota(jnp.int32, (rows_b, ln), 1)
    same = (ri // c) == (li // c)
    i_in, j_in = ri % c, li % c
    eye_rc = i_in == j_in
    near = (i_in // INV_BLOCK) == (j_in // INV_BLOCK)
    e_row = lax.broadcasted_iota(jnp.int32, (3 * SCALAR_LANES, 2 * ln), 0) % SCALAR_LANES
    e_blk = lax.broadcasted_iota(jnp.int32, (3 * SCALAR_LANES, 2 * ln), 1) // c
    zl = lax.broadcasted_iota(jnp.int32, (rows_b, SCALAR_LANES), 1)
    nt_dims = (((1,), (1,)), ((), ()))
    tn_dims = (((0,), (0,)), ((), ()))

    def bdiag(x):
        return jnp.where(same, jnp.concatenate([x] * DN_GROUP, axis=0), jnp.zeros((), x.dtype))

    def fold(x):
        x = jnp.where(same, x, 0.0)
        return x[0:c] + x[c:2 * c] + x[2 * c:3 * c] + x[3 * c:4 * c]

    dirs = range(2)
    incl = [i_in >= j_in, i_in <= j_in]
    strict = [i_in > j_in, i_in < j_in]
    tri_bd = [jnp.logical_and(same, incl[d]).astype(BF16) for d in dirs]
    sel_last = [jnp.logical_and(same, j_in == (c - 1, 0)[d]).astype(BF16) for d in dirs]
    same_b = same.astype(BF16)
    col0 = [d * 2 * DN_GROUP for d in dirs]
    expand = [(e_row == e_blk + col0[d]).astype(BF16) for d in dirs]
    is_beta = [jnp.logical_and(zl >= col0[d], zl < col0[d] + DN_GROUP) for d in dirs]
    eye_f = eye_rc.astype(F32)
    chunks = range(DN_BLOCK)
    sls = [slice(ch * c, (ch + 1) * c) for ch in chunks]
    mm = lambda a, bmat: jnp.dot(a.astype(BF16), bmat, preferred_element_type=F32)

    def prepare(blk, carry):
        rs = pl.ds(pl.multiple_of(blk * rows_b, rows_b), rows_b)
        z = bg_ref[0, 0, rs, :]
        r0 = blk * rows_b
        keep_prev = jnp.where(jnp.logical_or(blk == 0, blk == n_ctx_chunks // DN_BLOCK), 0.0, 1.0)
        keep_next = jnp.where(jnp.logical_or(blk == n_ctx_chunks // DN_BLOCK - 1,
                                             blk == n_chunks // DN_BLOCK - 1), 0.0, 1.0)
        prev_rows = pl.ds(pl.multiple_of(jnp.maximum(r0 - HALO, 0), HALO), HALO)
        next_rows = pl.ds(pl.multiple_of(jnp.minimum(r0 + rows_b, n_chunks * c - HALO), HALO), HALO)

        def conv_silu(x_ref, w_ref):
            ext = jnp.concatenate([x_ref[0, prev_rows, :].astype(F32)[HALO - SUBLANES:] * keep_prev,
                                   x_ref[0, rs, :].astype(F32),
                                   x_ref[0, next_rows, :].astype(F32)[:SUBLANES] * keep_next], axis=0)
            y = None
            for j in range(CONV_K):
                shift = (CONV_K // 2 - j) % (rows_b + 2 * SUBLANES)
                tap = ext if shift == 0 else pltpu.roll(ext, shift, 0)
                term = tap[SUBLANES:SUBLANES + rows_b] * w_ref[j:j + 1, :]
                y = term if y is None else y + term
            return _silu(y)

        def l2n(x):
            return x * lax.rsqrt(jnp.dot((x * x).astype(BF16), same_b, preferred_element_type=F32) + EPS)

        q = l2n(conv_silu(dq_ref, wq_ref)) * DN_SCALE
        k = l2n(conv_silu(dk_ref, wk_ref))
        v = conv_silu(dv_ref, wv_ref)
        qbf = q.astype(BF16)
        kbf = k.astype(BF16)
        sc = [lax.dot_general(jnp.concatenate([kbf[sl], qbf[sl]], axis=0), bdiag(kbf[sl]), nt_dims,
                              preferred_element_type=F32) for sl in sls]
        lfull, intra, vb, kbe, q_dec, k_dec, gamma = [], [], [], [], [], [], []
        for d in dirs:
            cum = _sum3(jnp.dot(tri_bd[d], _split3(z), preferred_element_type=F32))
            ex = jnp.dot(_split3(jnp.where(is_beta[d], z, cum)), expand[d], preferred_element_type=F32)
            beta, gce = ex[:, :ln], ex[:, ln:]
            g_row = _sum3(jnp.dot(same_b, _split3(jnp.where(eye_rc, gce, 0.0)), preferred_element_type=F32))
            g_last = _sum3(jnp.dot(sel_last[d], _split3(gce), preferred_element_type=F32))
            decay = jnp.where(incl[d], jnp.exp(jnp.where(incl[d], gce - g_row, 0.0)), 0.0)
            bdecay = jnp.where(strict[d], beta * decay, 0.0)
            e_g = jnp.exp(gce)
            kb = k * beta
            kbe_d = (kb * e_g).astype(BF16)
            vb_d = (v * beta).astype(BF16)
            q_dec_d = q * e_g
            k_dec_d = (k * jnp.exp(g_last - gce)).astype(BF16)
            gamma_d = jnp.exp(g_last)
            lfull += [s[:c] * bdecay[sl] for s, sl in zip(sc, sls)]
            intra += [(s[c:] * decay[sl]).astype(BF16) for s, sl in zip(sc, sls)]
            vb += [vb_d[sl] for sl in sls]
            kbe += [kbe_d[sl] for sl in sls]
            q_dec += [q_dec_d[sl] for sl in sls]
            k_dec += [k_dec_d[sl] for sl in sls]
            gamma += [gamma_d[ch * c:ch * c + SUBLANES] for ch in chunks]
        near_c = near[sls[0]]
        p = [jnp.where(near_c, -lf, 0.0) for lf in lfull]
        l_off = [jnp.where(near_c, 0.0, lf).astype(BF16) for lf in lfull]
        tmat = [eye_f[sls[0]] + pi for pi in p]
        p = [mm(pi, bdiag(pi.astype(BF16))) for pi in p]
        for _ in range(int(math.log2(INV_BLOCK)) - 2):
            res = [mm(jnp.concatenate([ti, pi], axis=0), bdiag(pi.astype(BF16))) for ti, pi in zip(tmat, p)]
            tmat = [ti + ri[:c] for ti, ri in zip(tmat, res)]
            p = [ri[c:] for ri in res]
        tmat = [ti + mm(ti, bdiag(pi.astype(BF16))) for ti, pi in zip(tmat, p)]
        nmat = [mm(ti, bdiag(lo)) for ti, lo in zip(tmat, l_off)]
        n2 = [mm(ni, bdiag(ni.astype(BF16))) for ni in nmat]
        tmat = [ti + mm(qi, bdiag(ti.astype(BF16))) for ti, qi in zip(tmat, n2)]
        tmat = [ti - mm(ni, bdiag(ti.astype(BF16))) for ti, ni in zip(tmat, nmat)]
        uw = [mm(ti, jnp.concatenate([bdiag(vi), bdiag(ki)], axis=1)).astype(BF16)
              for ti, vi, ki in zip(tmat, vb, kbe)]
        aw_au = [mm(ai, jnp.concatenate([bdiag(x[:, ln:]), bdiag(x[:, :ln])], axis=1))
                 for ai, x in zip(intra, uw)]
        mn = [lax.dot_general(kd, jnp.concatenate([x[:, ln:], x[:, :ln]], axis=1), tn_dims,
                              preferred_element_type=F32) for kd, x in zip(k_dec, uw)]
        for d in dirs:
            for ch in chunks:
                st = d * DN_BLOCK + ch
                ci = blk * DN_BLOCK + ch
                p_c = q_dec[st] - aw_au[st][:, :ln]
                mp_sc[d, ci] = jnp.concatenate([fold(mn[st][:, :ln]), p_c], axis=0).astype(BF16)
                n_sc[d, ci] = fold(mn[st][:, ln:])
                r_sc[d, ci] = aw_au[st][:, ln:].astype(BF16)
                gam_sc[d, ci] = gamma[st]
        return carry

    lax.fori_loop(0, n_chunks // DN_BLOCK, prepare, 0)

    o_ref[...] = jnp.zeros(o_ref.shape, F32)

    def scan(s, states):
        ci = (s, jnp.where(s < n_ctx_chunks, n_ctx_chunks - 1 - s, n_chunks - 1 + n_ctx_chunks - s))
        res = [jnp.dot(mp_sc[d, ci[d]], bdiag(states[d].astype(BF16)), preferred_element_type=F32)
               for d in dirs]
        for d in dirs:
            rows = pl.ds(pl.multiple_of(ci[d] * c, c), c)
            o_ref[0, rows, :] += res[d][c:] + r_sc[d, ci[d]].astype(F32)
        return tuple(gam_sc[d, ci[d]][0:1] * states[d] - res[d][:c] + n_sc[d, ci[d]] for d in dirs)

    zero = jnp.zeros((c, ln), F32)
    lax.fori_loop(0, n_chunks, scan, (zero, zero))


def _deltanet(dqkv, conv_w, bg, *, n_ctx):
    b, t, _ = dqkv.shape
    n_chunks = t // CHUNK
    block_rows = DN_BLOCK * CHUNK
    assert t % block_rows == 0 and n_ctx % block_rows == 0 and block_rows == DN_LANES == DN_GROUP * DN_DIM
    ng = DN_HEADS // DN_GROUP
    once = dict(pipeline_mode=pl.Buffered(1))
    part = lambda s: pl.BlockSpec((1, t, DN_LANES), lambda bi, gi, s=s: (bi, 0, s * ng + gi), **once)
    wpart = lambda s: pl.BlockSpec((CONV_K, DN_LANES), lambda bi, gi, s=s: (0, s * ng + gi))
    kern = functools.partial(_dn_kernel, n_chunks=n_chunks, n_ctx_chunks=n_ctx // CHUNK)
    return pl.pallas_call(
        kern, grid=(b, ng),
        in_specs=[part(0), part(1), part(2), wpart(0), wpart(1), wpart(2),
                  pl.BlockSpec((1, 1, t, SCALAR_LANES), lambda bi, gi: (bi, gi, 0, 0), **once)],
        out_specs=pl.BlockSpec((1, t, DN_LANES), lambda bi, gi: (bi, 0, gi)),
        out_shape=jax.ShapeDtypeStruct((b, t, DN_WIDTH), F32),
        scratch_shapes=[pltpu.VMEM((2, n_chunks, 2 * CHUNK, DN_LANES), BF16),
                        pltpu.VMEM((2, n_chunks, CHUNK, DN_LANES), F32),
                        pltpu.VMEM((2, n_chunks, CHUNK, DN_LANES), BF16),
                        pltpu.VMEM((2, n_chunks, SUBLANES, DN_LANES), F32)],
        compiler_params=_cparams("parallel", "parallel"),
        name="gated_deltanet",
    )(dqkv, dqkv, dqkv, conv_w, conv_w, conv_w, bg)


def _outproj_kernel(a_ref, d_ref, gate_ref, x_ref, mod_ref, dnw_ref, bd_ref, woa_ref, wod_ref, nfw_ref,
                    wrt_ref, br_ref, xo_ref, hf_ref, route_ref, grp_ref, cnt_ref):
    dd = d_ref[0]
    gate = gate_ref[0].astype(F32)
    dn = dd * lax.rsqrt(_group_mean_sq(dd, bd_ref[...], DN_DIM) + EPS) * dnw_ref[...] * _silu(gate)
    y = (jnp.dot(a_ref[0], woa_ref[...], preferred_element_type=F32)
         + jnp.dot(dn.astype(BF16), wod_ref[...], preferred_element_type=F32))
    x = x_ref[0] + mod_ref[0, 2:3, :] * y
    xo_ref[0] = x
    h = x * lax.rsqrt(jnp.mean(x * x, axis=-1, keepdims=True) + EPS) * nfw_ref[...]
    h = h * (1.0 + mod_ref[0, 4:5, :]) + mod_ref[0, 3:4, :]
    hb = h.astype(BF16)
    hf_ref[0] = hb
    logits = lax.dot_general(wrt_ref[...], hb, (((1,), (1,)), ((), ())), preferred_element_type=F32) + br_ref[...]
    row = lax.broadcasted_iota(jnp.int32, logits.shape, 0)
    neg = jnp.float32(-jnp.inf)
    big = jnp.int32(ROUTE_LANES)

    def first_argmax(vals, vmax):
        return jnp.min(jnp.where(vals == vmax, row, big), axis=0, keepdims=True)

    is_g = jnp.logical_and(row >= N_EXPERTS, row < N_EXPERTS + N_GROUPS)
    gl = jnp.where(is_g, logits, neg)
    g_max = jnp.max(gl, axis=0, keepdims=True)
    g_sel = first_argmax(gl, g_max) - N_EXPERTS
    p_g = 1.0 / jnp.sum(jnp.exp(gl - g_max), axis=0, keepdims=True)
    e_lo = g_sel * EXPERTS_PER_GROUP
    in_grp = jnp.logical_and(row >= e_lo, row < e_lo + EXPERTS_PER_GROUP)
    el = jnp.where(in_grp, logits, neg)
    e_max = jnp.max(el, axis=0, keepdims=True)
    i1 = first_argmax(el, e_max)
    el2 = jnp.where(row == i1, neg, el)
    e_max2 = jnp.max(el2, axis=0, keepdims=True)
    i2 = first_argmax(el2, e_max2)
    p2 = jnp.exp(e_max2 - e_max)
    w1 = p_g / (1.0 + p2)
    w2 = p_g * p2 / (1.0 + p2)
    route_ref[...] = jnp.where(row == i1, w1, jnp.where(row == i2, w2, 0.0)).astype(BF16)
    grp_ref[...] = jnp.broadcast_to(g_sel.astype(F32), grp_ref.shape)
    in_g = jnp.where(lax.broadcasted_iota(jnp.int32, grp_ref.shape, 0) == g_sel, 1.0, 0.0)
    cnt_ref[0, 0] = jnp.broadcast_to(jnp.sum(in_g, axis=1, keepdims=True), (SUBLANES, ROUTE_LANES))


def _outproj(a, dsum, gate, xs, mods, dnw, bd, woa, wod, nfw, wrt, br, *, n_ctx_tiles):
    b, t, d = xs.shape
    tm = TOKEN_TILE
    tok = lambda i, j: (j, i, 0)
    const = lambda i, j: (0, 0)
    flat = lambda i, j: (0, j * (t // tm) + i)
    modi = lambda i, j: (jnp.where(i < n_ctx_tiles, b, j), 0, 0)
    return pl.pallas_call(
        _outproj_kernel, grid=(t // tm, b),
        in_specs=[pl.BlockSpec((1, tm, ATTN_WIDTH), tok), pl.BlockSpec((1, tm, DN_WIDTH), tok),
                  pl.BlockSpec((1, tm, DN_WIDTH), tok), pl.BlockSpec((1, tm, d), tok),
                  pl.BlockSpec((1, 6, d), modi), pl.BlockSpec((1, DN_WIDTH), const),
                  pl.BlockSpec((DN_WIDTH, DN_WIDTH), const), pl.BlockSpec((ATTN_WIDTH, d), const),
                  pl.BlockSpec((DN_WIDTH, d), const), pl.BlockSpec((1, d), const),
                  pl.BlockSpec((ROUTE_LANES, d), const), pl.BlockSpec((ROUTE_LANES, 1), const)],
        out_specs=[pl.BlockSpec((1, tm, d), tok), pl.BlockSpec((1, tm, d), tok),
                   pl.BlockSpec((ROUTE_LANES, tm), flat), pl.BlockSpec((SUBLANES, tm), flat),
                   pl.BlockSpec((1, 1, SUBLANES, ROUTE_LANES), lambda i, j: (j, i, 0, 0))],
        out_shape=[jax.ShapeDtypeStruct((b, t, d), F32), jax.ShapeDtypeStruct((b, t, d), BF16),
                   jax.ShapeDtypeStruct((ROUTE_LANES, b * t), BF16),
                   jax.ShapeDtypeStruct((SUBLANES, b * t), F32),
                   jax.ShapeDtypeStruct((b, t // tm, SUBLANES, ROUTE_LANES), F32)],
        input_output_aliases={3: 0},
        compiler_params=_cparams("parallel", "parallel"),
        name="outproj_router",
    )(a, dsum, gate, xs, mods, dnw, bd, woa, wod, nfw, wrt, br)


def _moe_kernel(cnt_ref, h_ref, r_ref, grp_ref, tri_ref, w1_ref, w3_ref, w2_ref, f_ref, acc_sc, pos_sc,
                route_sc):
    m, g = pl.program_id(0), pl.program_id(1)
    tm = h_ref.shape[0]
    ff = w1_ref.shape[2]

    @pl.when(g == 0)
    def _():
        acc_sc[...] = jnp.zeros(acc_sc.shape, F32)
        route_sc[...] = r_ref[...].astype(F32).T.astype(BF16)
        grp = jnp.broadcast_to(grp_ref[0:1, :], (POS_ROWS * N_GROUPS, tm))
        row_g = lax.broadcasted_iota(jnp.int32, grp.shape, 0) // POS_ROWS
        member = grp == row_g.astype(F32)
        prefix = jnp.dot(member.astype(BF16), tri_ref[...], preferred_element_type=F32)
        pos_sc[...] = jnp.where(member, prefix - 1.0, -1.0)

    pos = pos_sc[pl.ds(pl.multiple_of(g * POS_ROWS, POS_ROWS), 1), :]
    e_row = lax.broadcasted_iota(jnp.int32, (ROUTE_LANES, EXPERTS_PER_GROUP * ff), 0)
    e_col = lax.broadcasted_iota(jnp.int32, (ROUTE_LANES, EXPERTS_PER_GROUP * ff), 1) // ff
    expand = (e_row == g * EXPERTS_PER_GROUP + e_col).astype(BF16)

    def sub_block(first_slot, rows):
        slot = lax.broadcasted_iota(jnp.int32, (rows, tm), 0) + first_slot
        sel = (pos == slot.astype(F32)).astype(BF16)
        xs = jnp.dot(sel, h_ref[...], preferred_element_type=F32).astype(BF16)
        wt = jnp.dot(sel, route_sc[...], preferred_element_type=F32).astype(BF16)
        wexp = jnp.dot(wt, expand, preferred_element_type=F32)
        y = None
        for e in range(EXPERTS_PER_GROUP):
            a = jnp.dot(xs, w1_ref[e], preferred_element_type=F32)
            gate = jnp.dot(xs, w3_ref[e], preferred_element_type=F32)
            mid = (_silu(a) * gate * wexp[:, e * ff:(e + 1) * ff]).astype(BF16)
            ye = jnp.dot(mid, w2_ref[e], preferred_element_type=F32)
            y = ye if y is None else y + ye
        acc_sc[...] += lax.dot_general(sel, y.astype(BF16), (((0,), (0,)), ((), ())),
                                       preferred_element_type=F32)

    count = cnt_ref[m * N_GROUPS + g]
    wide = jnp.logical_and(count > MOE_SUB, count <= MOE_WIDE[-1])
    n_full = jnp.where(wide, 0, count // MOE_SUB)
    rem = jnp.where(wide, 0, count - n_full * MOE_SUB)

    def full_block(sb, carry):
        sub_block(sb * MOE_SUB, MOE_SUB)
        return carry

    lax.fori_loop(0, n_full, full_block, 0)
    lo = 0
    for rows in MOE_TAILS:
        @pl.when(jnp.logical_and(rem > lo, rem <= rows))
        def _(rows=rows):
            sub_block(n_full * MOE_SUB, rows)
        lo = rows
    lo = MOE_SUB
    for rows in MOE_WIDE:
        @pl.when(jnp.logical_and(count > lo, count <= rows))
        def _(rows=rows):
            sub_block(jnp.int32(0), rows)
        lo = rows

    @pl.when(g == N_GROUPS - 1)
    def _():
        f_ref[...] = acc_sc[...].astype(f_ref.dtype)


def _moe(hf, route, grp, counts, tri, w1, w3, w2):
    n, d = hf.shape
    _, _, ff = w1.shape
    tm = tri.shape[0]
    epg = EXPERTS_PER_GROUP
    grid_spec = pltpu.PrefetchScalarGridSpec(
        num_scalar_prefetch=1, grid=(n // tm, N_GROUPS),
        in_specs=[pl.BlockSpec((tm, d), lambda i, g, c: (i, 0)),
                  pl.BlockSpec((ROUTE_LANES, tm), lambda i, g, c: (0, i)),
                  pl.BlockSpec((SUBLANES, tm), lambda i, g, c: (0, i)),
                  pl.BlockSpec((tm, tm), lambda i, g, c: (0, 0)),
                  pl.BlockSpec((epg, d, ff), lambda i, g, c: (g, 0, 0)),
                  pl.BlockSpec((epg, d, ff), lambda i, g, c: (g, 0, 0)),
                  pl.BlockSpec((epg, ff, d), lambda i, g, c: (g, 0, 0))],
        out_specs=pl.BlockSpec((tm, d), lambda i, g, c: (i, 0)),
        scratch_shapes=[pltpu.VMEM((tm, d), F32), pltpu.VMEM((POS_ROWS * N_GROUPS, tm), F32),
                        pltpu.VMEM((tm, ROUTE_LANES), BF16)])
    return pl.pallas_call(
        _moe_kernel, grid_spec=grid_spec,
        out_shape=jax.ShapeDtypeStruct((n, d), BF16),
        compiler_params=_cparams("parallel", "arbitrary"),
        name="moe_experts",
    )(counts, hf, route, grp, tri, w1, w3, w2)


def _final_kernel(x_ref, f_ref, mod_ref, w_ref, o_ref):
    x = x_ref[0] + mod_ref[0, 5:6, :] * f_ref[0].astype(F32)
    o_ref[0] = x * lax.rsqrt(jnp.mean(x * x, axis=-1, keepdims=True) + EPS) * w_ref[...]


def _final_norm(xs, f_prev, mods, w, *, n_ctx_tiles):
    b, t, d = xs.shape
    tm = TOKEN_TILE
    n_lat = t // tm - n_ctx_tiles
    lat = lambda i, j: (j, i + n_ctx_tiles, 0)
    return pl.pallas_call(
        _final_kernel, grid=(n_lat, b),
        in_specs=[pl.BlockSpec((1, tm, d), lat), pl.BlockSpec((1, tm, d), lat),
                  pl.BlockSpec((1, 6, d), lambda i, j: (j, 0, 0)), pl.BlockSpec((1, d), lambda i, j: (0, 0))],
        out_specs=pl.BlockSpec((1, tm, d), lambda i, j: (j, i, 0)),
        out_shape=jax.ShapeDtypeStruct((b, n_lat * tm, d), F32),
        compiler_params=_cparams("parallel", "parallel"),
        name="final_norm",
    )(xs, f_prev, mods, w)


def _rope_tables(n_ctx, n_lat):
    pos = jnp.arange(n_lat, dtype=jnp.int32)
    inv = ROPE_THETA ** (-jnp.arange(ROPE_NF, dtype=F32) / ROPE_NF)
    ang_r = (pos // GRID_W).astype(F32)[:, None] * inv
    ang_c = (pos % GRID_W).astype(F32)[:, None] * inv
    cos = jnp.concatenate([jnp.cos(ang_r)] * 2 + [jnp.cos(ang_c)] * 2, axis=-1)
    sin = jnp.concatenate([-jnp.sin(ang_r), jnp.sin(ang_r), -jnp.sin(ang_c), jnp.sin(ang_c)], axis=-1)
    cos = jnp.concatenate([jnp.ones((n_ctx, HEAD_DIM), F32), cos], axis=0)
    sin = jnp.concatenate([jnp.zeros((n_ctx, HEAD_DIM), F32), sin], axis=0)
    reps = QK_WIDTH // HEAD_DIM
    return jnp.tile(cos, (1, reps)), jnp.tile(sin, (1, reps))


def kernel(x, c, ctx, c_ctx, ada_w, ada_b, norm_mix_w, norm_ffn_w, w_in, q_norm_w, k_norm_w, conv_w,
           dn_A_log, dn_dt_bias, dn_norm_w, w_out, rg_w, rg_b, re_w, re_b, w1, w3, w2, final_norm_w):
    b, s, d = x.shape
    n_ctx = ctx.shape[1]
    depth = w_in.shape[0]
    t = n_ctx + s
    assert n_ctx % TOKEN_TILE == 0 and s % TOKEN_TILE == 0 and s % GRID_W == 0
    n_ctx_tiles = n_ctx // TOKEN_TILE

    xs = jnp.concatenate([ctx, x], axis=1)
    mod_rows = -(-(b + BATCH_PER_STEP) // SUBLANES) * SUBLANES
    cs = jnp.zeros((mod_rows, d), F32).at[:b].set(c).at[b:b + BATCH_PER_STEP].set(c_ctx)
    mods = _ada_mods(cs, ada_w, ada_b).reshape(depth, mod_rows, 6, d)

    cos_t, sin_t = _rope_tables(n_ctx, s)
    bd_qk = _block_ones(QK_WIDTH, HEAD_DIM)
    bd_dn = _block_ones(DN_WIDTH, DN_DIM)
    zeros16 = jnp.zeros((2 * DN_HEADS,), F32)
    n_main = w_in.shape[2] - 4 * DN_HEADS
    lane_i = jnp.arange((DN_HEADS // DN_GROUP) * SCALAR_LANES)
    gi, li = lane_i // SCALAR_LANES, lane_i % SCALAR_LANES
    di, ki, hi = li // (2 * DN_GROUP), (li // DN_GROUP) % 2, li % DN_GROUP
    bg_used = li < 4 * DN_GROUP
    bg_src = jnp.where(bg_used, ki * 2 * DN_HEADS + di * DN_HEADS + gi * DN_GROUP + hi, 0)
    moe_tile = next(c for c in MOE_TILES if (b * t) % c == 0)
    tri = (lax.broadcasted_iota(jnp.int32, (moe_tile, moe_tile), 0)
           <= lax.broadcasted_iota(jnp.int32, (moe_tile, moe_tile), 1)).astype(BF16)

    f_prev = None
    for l in range(depth):
        qkw = jnp.concatenate([jnp.tile(q_norm_w[l], ATTN_HEADS), jnp.tile(k_norm_w[l], KV_HEADS)])[None]
        dnp = jnp.stack([jnp.concatenate([zeros16, dn_A_log[l].reshape(-1)]),
                         jnp.concatenate([zeros16, dn_dt_bias[l].reshape(-1)])])
        dnp = jnp.where(bg_used, jnp.take(dnp, bg_src, axis=1), 0.0)
        w_tail = jnp.where(bg_used, jnp.take(w_in[l][:, n_main:], bg_src, axis=1), 0.0)
        w_in_l = jnp.concatenate([w_in[l][:, :n_main], w_tail], axis=1).astype(BF16)
        outs = _inproj(xs, f_prev, mods[l - 1] if l else None, mods[l], norm_mix_w[l][None],
                       w_in_l, qkw, cos_t, sin_t, bd_qk, dnp, n_ctx_tiles=n_ctx_tiles)
        qt, k, vt, dqkv, gate, bg = outs[:6]
        if l:
            xs = outs[6]
        a = _attention(qt, k, vt, n_ctx=n_ctx)
        dsum = _deltanet(dqkv, conv_w[l], bg, n_ctx=n_ctx)
        wo = w_out[l].astype(BF16)
        wrt = jnp.zeros((ROUTE_LANES, d), F32).at[:N_EXPERTS].set(re_w[l].T).at[
            N_EXPERTS:N_EXPERTS + N_GROUPS].set(rg_w[l].T).astype(BF16)
        br = jnp.zeros((ROUTE_LANES, 1), F32).at[:N_EXPERTS, 0].set(re_b[l]).at[
            N_EXPERTS:N_EXPERTS + N_GROUPS, 0].set(rg_b[l])
        xs, hf, route, grp, cnt = _outproj(
            a, dsum, gate, xs, mods[l], jnp.tile(dn_norm_w[l], DN_HEADS)[None], bd_dn,
            wo[:ATTN_WIDTH], wo[ATTN_WIDTH:], norm_ffn_w[l][None], wrt, br, n_ctx_tiles=n_ctx_tiles)
        counts = cnt[:, :, :N_GROUPS, 0].reshape(-1, moe_tile // TOKEN_TILE, N_GROUPS).sum(axis=1)
        f_prev = _moe(hf.reshape(b * t, d), route, grp,
                      counts.astype(jnp.int32).reshape(-1), tri,
                      w1[l].astype(BF16), w3[l].astype(BF16), w2[l].astype(BF16)).reshape(b, t, d)
    return _final_norm(xs, f_prev, mods[depth - 1], final_norm_w[None], n_ctx_tiles=n_ctx_tiles)
```

```python
import functools
import math

import jax
import jax.numpy as jnp
from jax import lax
from jax.experimental import pallas as pl
from jax.experimental.pallas import tpu as pltpu

F32 = jnp.float32
BF16 = jnp.bfloat16
HIGHEST = lax.Precision.HIGHEST

GRID_W = 64
HEAD_DIM = 64
ATTN_HEADS = 8
KV_HEADS = 2
GQA_GROUP = ATTN_HEADS // KV_HEADS
ATTN_WIDTH = ATTN_HEADS * HEAD_DIM
KV_WIDTH = KV_HEADS * HEAD_DIM
ATTN_SCALE = HEAD_DIM ** -0.5
LOG2_E = math.log2(math.e)
ROPE_THETA = 10000.0
ROPE_NF = HEAD_DIM // 4
DN_HEADS = 8
DN_DIM = 64
DN_WIDTH = DN_HEADS * DN_DIM
DN_SCALE = DN_DIM ** -0.5
DN_GROUP = 4
CONV_K = 5
CHUNK = 64
N_GROUPS = 4
EXPERTS_PER_GROUP = 8
N_EXPERTS = N_GROUPS * EXPERTS_PER_GROUP
EPS = 1e-6
QK_WIDTH = ATTN_WIDTH + KV_WIDTH
SUBLANES = 8
TOKEN_TILE = 256
BATCH_PER_STEP = 2
ATTN_KEY_TILE = 256
ONES_ROWS = 16
HEADS_PER_PASS = 8
HALO = 16
ROUTE_LANES = 128
MOE_TILES = (1024, 512, 256)
MOE_SUB = 256
MOE_TAILS = (64, 128, 256)
MOE_WIDE = (320, 384)
POS_ROWS = 16
VMEM_LIMIT = 56 * 1024 * 1024


def _cparams(*sem):
    return pltpu.CompilerParams(dimension_semantics=sem, vmem_limit_bytes=VMEM_LIMIT)


def _silu(x):
    return x * jax.nn.sigmoid(x)


def _block_ones(n, blk):
    i = lax.broadcasted_iota(jnp.int32, (n, n), 0) // blk
    j = lax.broadcasted_iota(jnp.int32, (n, n), 1) // blk
    return (i == j).astype(BF16)


def _group_mean_sq(x, ones_bd, width):
    return jnp.dot((x * x).astype(BF16), ones_bd, preferred_element_type=F32) * (1.0 / width)


def _ada_kernel(cs_ref, w_ref, b_ref, o_ref):
    o_ref[0] = jnp.dot(_silu(cs_ref[...]), w_ref[0], preferred_element_type=F32,
                       precision=HIGHEST) + b_ref[0]


def _ada_mods(cs, ada_w, ada_b):
    depth, d, n6 = ada_w.shape
    rows = cs.shape[0]
    tn = 1536
    return pl.pallas_call(
        _ada_kernel,
        grid=(depth, n6 // tn),
        in_specs=[pl.BlockSpec((rows, d), lambda l, j: (0, 0)),
                  pl.BlockSpec((1, d, tn), lambda l, j: (l, 0, j)),
                  pl.BlockSpec((1, 1, tn), lambda l, j: (l, 0, j))],
        out_specs=pl.BlockSpec((1, rows, tn), lambda l, j: (l, 0, j)),
        out_shape=jax.ShapeDtypeStruct((depth, rows, n6), F32),
        compiler_params=_cparams("parallel", "parallel"),
        name="ada_mods",
    )(cs, ada_w, ada_b.reshape(depth, 1, n6))


def _inproj_kernel(*refs, has_prev):
    if has_prev:
        (x_ref, f_ref, modp_ref, mod_ref, nw_ref, win_ref, qkw_ref, cos_ref, sin_ref, bd_ref, dnp_ref,
         qt_ref, k_ref, vt_ref, dqkv_ref, gate_ref, bg_ref, xo_ref) = refs
    else:
        (x_ref, mod_ref, nw_ref, win_ref, qkw_ref, cos_ref, sin_ref, bd_ref, dnp_ref,
         qt_ref, k_ref, vt_ref, dqkv_ref, gate_ref, bg_ref) = refs
    nb, tm = x_ref.shape[0], x_ref.shape[1]
    hs = []
    for r in range(nb):
        x = x_ref[r]
        if has_prev:
            x = x + modp_ref[r, 5:6, :] * f_ref[r].astype(F32)
            xo_ref[r] = x
        h = x * lax.rsqrt(jnp.mean(x * x, axis=-1, keepdims=True) + EPS) * nw_ref[...]
        hs.append((h * (1.0 + mod_ref[r, 1:2, :]) + mod_ref[r, 0:1, :]).astype(BF16))
    acc = jnp.dot(jnp.concatenate(hs, axis=0), win_ref[...], preferred_element_type=F32)
    qk = acc[:, :QK_WIDTH]
    qn = qk * lax.rsqrt(_group_mean_sq(qk, bd_ref[...], HEAD_DIM) + EPS) * qkw_ref[...]
    lane = lax.broadcasted_iota(jnp.int32, qn.shape, 1)
    partner = jnp.where(lane % (2 * ROPE_NF) < ROPE_NF,
                        pltpu.roll(qn, QK_WIDTH - ROPE_NF, 1), pltpu.roll(qn, ROPE_NF, 1))
    qr = (qn * jnp.concatenate([cos_ref[...]] * nb, axis=0)
          + partner * jnp.concatenate([sin_ref[...]] * nb, axis=0))
    c_v = QK_WIDTH
    c_dn = c_v + KV_WIDTH
    c_gate = c_dn + 3 * DN_WIDTH
    c_bg = c_gate + DN_WIDTH
    z = acc[:, c_bg:]
    zb = z + dnp_ref[1:2, :]
    softplus = jnp.maximum(zb, 0.0) + jnp.log1p(jnp.exp(-jnp.abs(zb)))
    lane_z = lax.broadcasted_iota(jnp.int32, z.shape, 1)
    bg = jnp.where(lane_z % (2 * DN_GROUP) < DN_GROUP, jax.nn.sigmoid(z), -jnp.exp(dnp_ref[0:1, :]) * softplus)
    for r in range(nb):
        rs = slice(r * tm, (r + 1) * tm)
        qt_ref[r] = (qr[rs, :ATTN_WIDTH] * (ATTN_SCALE * LOG2_E)).T.astype(BF16)
        k_ref[r] = qr[rs, ATTN_WIDTH:].astype(BF16)
        vt_ref[r] = acc[rs, c_v:c_dn].T.astype(BF16)
        dqkv_ref[r] = acc[rs, c_dn:c_gate].astype(BF16)
        gate_ref[r] = acc[rs, c_gate:c_bg].astype(BF16)
        for gi in range(DN_HEADS // DN_GROUP):
            bg_ref[r, gi] = bg[rs, gi * SCALAR_LANES:(gi + 1) * SCALAR_LANES]


def _inproj(xs, f_prev, mods_prev, mods, nw, win, qkw, cos_t, sin_t, bd, dnp, *, n_ctx_tiles):
    b, t, d = xs.shape
    tm = TOKEN_TILE
    nb = BATCH_PER_STEP
    assert b % nb == 0
    nt = t // tm
    proj = win.shape[1]
    has_prev = f_prev is not None
    tok = lambda i, j: (j, i, 0)
    modi = lambda i, j: (jnp.where(i < n_ctx_tiles, b // nb, j), 0, 0)
    const = lambda i, j: (0, 0)
    in_specs = [pl.BlockSpec((nb, tm, d), tok)]
    args = [xs]
    if has_prev:
        in_specs += [pl.BlockSpec((nb, tm, d), tok), pl.BlockSpec((nb, 6, d), modi)]
        args += [f_prev, mods_prev]
    in_specs += [pl.BlockSpec((nb, 6, d), modi), pl.BlockSpec((1, d), const),
                 pl.BlockSpec((d, proj), const), pl.BlockSpec((1, QK_WIDTH), const),
                 pl.BlockSpec((tm, QK_WIDTH), lambda i, j: (i, 0)),
                 pl.BlockSpec((tm, QK_WIDTH), lambda i, j: (i, 0)),
                 pl.BlockSpec((QK_WIDTH, QK_WIDTH), const), pl.BlockSpec((2, dnp.shape[1]), const)]
    args += [mods, nw, win, qkw, cos_t, sin_t, bd, dnp]
    seq_out = lambda w, dt: (pl.BlockSpec((nb, tm, w), tok), jax.ShapeDtypeStruct((b, t, w), dt))
    tr_out = lambda w: (pl.BlockSpec((nb, w, tm), lambda i, j: (j, 0, i)), jax.ShapeDtypeStruct((b, w, t), BF16))
    ng = DN_HEADS // DN_GROUP
    outs = [tr_out(ATTN_WIDTH), seq_out(KV_WIDTH, BF16), tr_out(KV_WIDTH),
            seq_out(3 * DN_WIDTH, BF16), seq_out(DN_WIDTH, BF16),
            (pl.BlockSpec((nb, ng, tm, SCALAR_LANES), lambda i, j: (j, 0, i, 0)),
             jax.ShapeDtypeStruct((b, ng, t, SCALAR_LANES), F32))]
    out_specs = [o[0] for o in outs]
    out_shape = [o[1] for o in outs]
    aliases = {}
    if has_prev:
        out_specs.append(pl.BlockSpec((nb, tm, d), tok))
        out_shape.append(jax.ShapeDtypeStruct((b, t, d), F32))
        aliases = {0: len(out_shape) - 1}
    return pl.pallas_call(
        functools.partial(_inproj_kernel, has_prev=has_prev),
        grid=(nt, b // nb), in_specs=in_specs, out_specs=out_specs, out_shape=out_shape,
        input_output_aliases=aliases,
        compiler_params=_cparams("parallel", "parallel"),
        name="inproj",
    )(*args)


def _attn_kernel(qt_ref, k_ref, vt_ref, o_ref, acc_sc, sa_sc, sb_sc, *, tk, n_ctx_q, n_ctx_k, n_all_k):
    tq = qt_ref.shape[2]
    n_kv = jnp.where(pl.program_id(1) < n_ctx_q, n_ctx_k, n_all_k)
    nh = HEADS_PER_PASS
    local = range(nh)
    last = n_kv - 1

    def tile_rows(i):
        return pl.ds(pl.multiple_of(i * tk, tk), tk)

    for h0 in range(0, ATTN_HEADS, nh):
        kv = [(h0 + j) // GQA_GROUP for j in local]
        acc_sc[...] = jnp.zeros(acc_sc.shape, F32)

        def scores(i, s_sc, h0=h0, kv=kv):
            rows = tile_rows(i)
            for j in local:
                s_sc[j] = jnp.dot(k_ref[0, rows, kv[j] * HEAD_DIM:(kv[j] + 1) * HEAD_DIM],
                                  qt_ref[0, (h0 + j) * HEAD_DIM:(h0 + j + 1) * HEAD_DIM, :],
                                  preferred_element_type=F32)

        def softmax_pv(i, s_sc, carry, kv=kv):
            rows = tile_rows(i)
            m_prev, l_prev = carry[:nh], carry[nh:]
            s = [s_sc[j] for j in local]
            m_new = [jnp.maximum(m_prev[j], jnp.max(s[j], axis=0, keepdims=True)) for j in local]
            p = [jnp.exp2(s[j] - m_new[j]) for j in local]
            alpha = [jnp.exp2(m_prev[j] - m_new[j]) for j in local]
            ones = jnp.ones((ONES_ROWS, tk), BF16)
            vt1 = {g: jnp.concatenate([vt_ref[0, g * HEAD_DIM:(g + 1) * HEAD_DIM, rows], ones], axis=0)
                   for g in sorted(set(kv))}
            pv = [jnp.dot(vt1[kv[j]], p[j].astype(BF16), preferred_element_type=F32) for j in local]
            for j in local:
                acc_sc[j] = alpha[j] * acc_sc[j] + pv[j][:HEAD_DIM]
            l_new = [alpha[j] * l_prev[j] + pv[j][HEAD_DIM:HEAD_DIM + 1] for j in local]
            return tuple(m_new) + tuple(l_new)

        init = (jnp.full((1, tq), -jnp.inf, F32),) * nh + (jnp.zeros((1, tq), F32),) * nh
        scores(0, sa_sc)
        scores(jnp.minimum(1, last), sb_sc)
        stats = softmax_pv(0, sa_sc, init)

        def pair(j, carry, scores=scores, softmax_pv=softmax_pv):
            a = 2 * j + 1
            scores(a + 1, sa_sc)
            carry = softmax_pv(a, sb_sc, carry)
            scores(jnp.minimum(a + 2, last), sb_sc)
            return softmax_pv(a + 1, sa_sc, carry)

        stats = lax.fori_loop(0, last // 2, pair, stats)
        og = jnp.concatenate([acc_sc[j] / stats[nh + j] for j in local], axis=0)
        o_ref[0, :, h0 * HEAD_DIM:(h0 + nh) * HEAD_DIM] = og.T.astype(BF16)


def _attention(qt, k, vt, *, n_ctx):
    b, _, t = qt.shape
    tq, tk = TOKEN_TILE, ATTN_KEY_TILE
    assert (n_ctx // tk) % 2 == 1 and (t // tk) % 2 == 1
    kern = functools.partial(_attn_kernel, tk=tk, n_ctx_q=n_ctx // tq, n_ctx_k=n_ctx // tk, n_all_k=t // tk)
    s_buf = pltpu.VMEM((HEADS_PER_PASS, tk, tq), F32)
    return pl.pallas_call(
        kern, grid=(b, t // tq),
        in_specs=[pl.BlockSpec((1, ATTN_WIDTH, tq), lambda bi, i: (bi, 0, i)),
                  pl.BlockSpec((1, t, KV_WIDTH), lambda bi, i: (bi, 0, 0)),
                  pl.BlockSpec((1, KV_WIDTH, t), lambda bi, i: (bi, 0, 0))],
        out_specs=pl.BlockSpec((1, tq, ATTN_WIDTH), lambda bi, i: (bi, i, 0)),
        out_shape=jax.ShapeDtypeStruct((b, t, ATTN_WIDTH), BF16),
        scratch_shapes=[pltpu.VMEM((HEADS_PER_PASS, HEAD_DIM, tq), F32), s_buf, s_buf],
        compiler_params=_cparams("parallel", "parallel"),
        name="gqa_attention",
    )(qt, k, vt)


DN_LANES = DN_GROUP * DN_DIM
DN_BLOCK = 4
INV_BLOCK = 16
SCALAR_LANES = 128


def _split3(x):
    x1 = x.astype(BF16)
    r1 = x - x1.astype(F32)
    x2 = r1.astype(BF16)
    x3 = (r1 - x2.astype(F32)).astype(BF16)
    return jnp.concatenate([x1, x2, x3], axis=1)


def _sum3(y):
    w = y.shape[1] // 3
    return y[:, :w] + y[:, w:2 * w] + y[:, 2 * w:]


def _dn_kernel(dq_ref, dk_ref, dv_ref, wq_ref, wk_ref, wv_ref, bg_ref, o_ref, mp_sc, n_sc, r_sc, gam_sc, *,
               n_chunks, n_ctx_chunks):
    c = CHUNK
    rows_b = DN_BLOCK * c
    ln = DN_LANES
    ri = lax.broadcasted_iota(jnp.int32, (rows_b, ln), 0)
    li = lax.broadcasted_iota(jnp.int32, (rows_b, ln), 1)
    same = (ri // c) == (li // c)
    i_in, j_in = ri % c, li % c
    eye_rc = i_in == j_in
    near = (i_in // INV_BLOCK) == (j_in // INV_BLOCK)
    e_row = lax.broadcasted_iota(jnp.int32, (3 * SCALAR_LANES, 2 * ln), 0) % SCALAR_LANES
    e_blk = lax.broadcasted_iota(jnp.int32, (3 * SCALAR_LANES, 2 * ln), 1) // c
    zl = lax.broadcasted_iota(jnp.int32, (rows_b, SCALAR_LANES), 1)
    nt_dims = (((1,), (1,)), ((), ()))
    tn_dims = (((0,), (0,)), ((), ()))

    def bdiag(x):
        return jnp.where(same, jnp.concatenate([x] * DN_GROUP, axis=0), jnp.zeros((), x.dtype))

    def fold(x):
        x = jnp.where(same, x, 0.0)
        return x[0:c] + x[c:2 * c] + x[2 * c:3 * c] + x[3 * c:4 * c]

    dirs = range(2)
    incl = [i_in >= j_in, i_in <= j_in]
    strict = [i_in > j_in, i_in < j_in]
    tri_bd = [jnp.logical_and(same, incl[d]).astype(BF16) for d in dirs]
    sel_last = [jnp.logical_and(same, j_in == (c - 1, 0)[d]).astype(BF16) for d in dirs]
    same_b = same.astype(BF16)
    col0 = [d * 2 * DN_GROUP for d in dirs]
    expand = [(e_row == e_blk + col0[d]).astype(BF16) for d in dirs]
    is_beta = [jnp.logical_and(zl >= col0[d], zl < col0[d] + DN_GROUP) for d in dirs]
    eye_f = eye_rc.astype(F32)
    chunks = range(DN_BLOCK)
    sls = [slice(ch * c, (ch + 1) * c) for ch in chunks]
    mm = lambda a, bmat: jnp.dot(a.astype(BF16), bmat, preferred_element_type=F32)

    def prepare(blk, carry):
        rs = pl.ds(pl.multiple_of(blk * rows_b, rows_b), rows_b)
        z = bg_ref[0, 0, rs, :]
        r0 = blk * rows_b
        keep_prev = jnp.where(jnp.logical_or(blk == 0, blk == n_ctx_chunks // DN_BLOCK), 0.0, 1.0)
        keep_next = jnp.where(jnp.logical_or(blk == n_ctx_chunks // DN_BLOCK - 1,
                                             blk == n_chunks // DN_BLOCK - 1), 0.0, 1.0)
        prev_rows = pl.ds(pl.multiple_of(jnp.maximum(r0 - HALO, 0), HALO), HALO)
        next_rows = pl.ds(pl.multiple_of(jnp.minimum(r0 + rows_b, n_chunks * c - HALO), HALO), HALO)

        def conv_silu(x_ref, w_ref):
            ext = jnp.concatenate([x_ref[0, prev_rows, :].astype(F32)[HALO - SUBLANES:] * keep_prev,
                                   x_ref[0, rs, :].astype(F32),
                                   x_ref[0, next_rows, :].astype(F32)[:SUBLANES] * keep_next], axis=0)
            y = None
            for j in range(CONV_K):
                shift = (CONV_K // 2 - j) % (rows_b + 2 * SUBLANES)
                tap = ext if shift == 0 else pltpu.roll(ext, shift, 0)
                term = tap[SUBLANES:SUBLANES + rows_b] * w_ref[j:j + 1, :]
                y = term if y is None else y + term
            return _silu(y)

        def l2n(x):
            return x * lax.rsqrt(jnp.dot((x * x).astype(BF16), same_b, preferred_element_type=F32) + EPS)

        q = l2n(conv_silu(dq_ref, wq_ref)) * DN_SCALE
        k = l2n(conv_silu(dk_ref, wk_ref))
        v = conv_silu(dv_ref, wv_ref)
        qbf = q.astype(BF16)
        kbf = k.astype(BF16)
        sc = [lax.dot_general(jnp.concatenate([kbf[sl], qbf[sl]], axis=0), bdiag(kbf[sl]), nt_dims,
                              preferred_element_type=F32) for sl in sls]
        lfull, intra, vb, kbe, q_dec, k_dec, gamma = [], [], [], [], [], [], []
        for d in dirs:
            cum = _sum3(jnp.dot(tri_bd[d], _split3(z), preferred_element_type=F32))
            ex = jnp.dot(_split3(jnp.where(is_beta[d], z, cum)), expand[d], preferred_element_type=F32)
            beta, gce = ex[:, :ln], ex[:, ln:]
            g_row = _sum3(jnp.dot(same_b, _split3(jnp.where(eye_rc, gce, 0.0)), preferred_element_type=F32))
            g_last = _sum3(jnp.dot(sel_last[d], _split3(gce), preferred_element_type=F32))
            decay = jnp.where(incl[d], jnp.exp(jnp.where(incl[d], gce - g_row, 0.0)), 0.0)
            bdecay = jnp.where(strict[d], beta * decay, 0.0)
            e_g = jnp.exp(gce)
            kb = k * beta
            kbe_d = (kb * e_g).astype(BF16)
            vb_d = (v * beta).astype(BF16)
            q_dec_d = q * e_g
            k_dec_d = (k * jnp.exp(g_last - gce)).astype(BF16)
            gamma_d = jnp.exp(g_last)
            lfull += [s[:c] * bdecay[sl] for s, sl in zip(sc, sls)]
            intra += [(s[c:] * decay[sl]).astype(BF16) for s, sl in zip(sc, sls)]
            vb += [vb_d[sl] for sl in sls]
            kbe += [kbe_d[sl] for sl in sls]
            q_dec += [q_dec_d[sl] for sl in sls]
            k_dec += [k_dec_d[sl] for sl in sls]
            gamma += [gamma_d[ch * c:ch * c + SUBLANES] for ch in chunks]
        near_c = near[sls[0]]
        p = [jnp.where(near_c, -lf, 0.0) for lf in lfull]
        l_off = [jnp.where(near_c, 0.0, lf).astype(BF16) for lf in lfull]
        tmat = [eye_f[sls[0]] + pi for pi in p]
        p = [mm(pi, bdiag(pi.astype(BF16))) for pi in p]
        for _ in range(int(math.log2(INV_BLOCK)) - 2):
            res = [mm(jnp.concatenate([ti, pi], axis=0), bdiag(pi.astype(BF16))) for ti, pi in zip(tmat, p)]
            tmat = [ti + ri[:c] for ti, ri in zip(tmat, res)]
            p = [ri[c:] for ri in res]
        tmat = [ti + mm(ti, bdiag(pi.astype(BF16))) for ti, pi in zip(tmat, p)]
        nmat = [mm(ti, bdiag(lo)) for ti, lo in zip(tmat, l_off)]
        n2 = [mm(ni, bdiag(ni.astype(BF16))) for ni in nmat]
        tmat = [ti + mm(qi, bdiag(ti.astype(BF16))) for ti, qi in zip(tmat, n2)]
        tmat = [ti - mm(ni, bdiag(ti.astype(BF16))) for ti, ni in zip(tmat, nmat)]
        uw = [mm(ti, jnp.concatenate([bdiag(vi), bdiag(ki)], axis=1)).astype(BF16)
              for ti, vi, ki in zip(tmat, vb, kbe)]
        aw_au = [mm(ai, jnp.concatenate([bdiag(x[:, ln:]), bdiag(x[:, :ln])], axis=1))
                 for ai, x in zip(intra, uw)]
        mn = [lax.dot_general(kd, jnp.concatenate([x[:, ln:], x[:, :ln]], axis=1), tn_dims,
                              preferred_element_type=F32) for kd, x in zip(k_dec, uw)]
        for d in dirs:
            for ch in chunks:
                st = d * DN_BLOCK + ch
                ci = blk * DN_BLOCK + ch
                p_c = q_dec[st] - aw_au[st][:, :ln]
                mp_sc[d, ci] = jnp.concatenate([fold(mn[st][:, :ln]), p_c], axis=0).astype(BF16)
                n_sc[d, ci] = fold(mn[st][:, ln:])
                r_sc[d, ci] = aw_au[st][:, ln:].astype(BF16)
                gam_sc[d, ci] = gamma[st]
        return carry

    lax.fori_loop(0, n_chunks // DN_BLOCK, prepare, 0)

    o_ref[...] = jnp.zeros(o_ref.shape, F32)

    def scan(s, states):
        ci = (s, jnp.where(s < n_ctx_chunks, n_ctx_chunks - 1 - s, n_chunks - 1 + n_ctx_chunks - s))
        res = [jnp.dot(mp_sc[d, ci[d]], bdiag(states[d].astype(BF16)), preferred_element_type=F32)
               for d in dirs]
        for d in dirs:
            rows = pl.ds(pl.multiple_of(ci[d] * c, c), c)
            o_ref[0, rows, :] += res[d][c:] + r_sc[d, ci[d]].astype(F32)
        return tuple(gam_sc[d, ci[d]][0:1] * states[d] - res[d][:c] + n_sc[d, ci[d]] for d in dirs)

    zero = jnp.zeros((c, ln), F32)
    lax.fori_loop(0, n_chunks, scan, (zero, zero), unroll=DN_BLOCK)


def _deltanet(dqkv, conv_w, bg, *, n_ctx):
    b, t, _ = dqkv.shape
    n_chunks = t // CHUNK
    block_rows = DN_BLOCK * CHUNK
    assert t % block_rows == 0 and n_ctx % block_rows == 0 and block_rows == DN_LANES == DN_GROUP * DN_DIM
    ng = DN_HEADS // DN_GROUP
    once = dict(pipeline_mode=pl.Buffered(1))
    part = lambda s: pl.BlockSpec((1, t, DN_LANES), lambda bi, gi, s=s: (bi, 0, s * ng + gi), **once)
    wpart = lambda s: pl.BlockSpec((CONV_K, DN_LANES), lambda bi, gi, s=s: (0, s * ng + gi))
    kern = functools.partial(_dn_kernel, n_chunks=n_chunks, n_ctx_chunks=n_ctx // CHUNK)
    return pl.pallas_call(
        kern, grid=(b, ng),
        in_specs=[part(0), part(1), part(2), wpart(0), wpart(1), wpart(2),
                  pl.BlockSpec((1, 1, t, SCALAR_LANES), lambda bi, gi: (bi, gi, 0, 0), **once)],
        out_specs=pl.BlockSpec((1, t, DN_LANES), lambda bi, gi: (bi, 0, gi)),
        out_shape=jax.ShapeDtypeStruct((b, t, DN_WIDTH), F32),
        scratch_shapes=[pltpu.VMEM((2, n_chunks, 2 * CHUNK, DN_LANES), BF16),
                        pltpu.VMEM((2, n_chunks, CHUNK, DN_LANES), F32),
                        pltpu.VMEM((2, n_chunks, CHUNK, DN_LANES), BF16),
                        pltpu.VMEM((2, n_chunks, SUBLANES, DN_LANES), F32)],
        compiler_params=_cparams("parallel", "parallel"),
        name="gated_deltanet",
    )(dqkv, dqkv, dqkv, conv_w, conv_w, conv_w, bg)


def _outproj_kernel(a_ref, d_ref, gate_ref, x_ref, mod_ref, dnw_ref, bd_ref, woa_ref, wod_ref, nfw_ref,
                    wrt_ref, br_ref, xo_ref, hf_ref, route_ref, grp_ref, cnt_ref):
    dd = d_ref[0]
    gate = gate_ref[0].astype(F32)
    dn = dd * lax.rsqrt(_group_mean_sq(dd, bd_ref[...], DN_DIM) + EPS) * dnw_ref[...] * _silu(gate)
    y = (jnp.dot(a_ref[0], woa_ref[...], preferred_element_type=F32)
         + jnp.dot(dn.astype(BF16), wod_ref[...], preferred_element_type=F32))
    x = x_ref[0] + mod_ref[0, 2:3, :] * y
    xo_ref[0] = x
    h = x * lax.rsqrt(jnp.mean(x * x, axis=-1, keepdims=True) + EPS) * nfw_ref[...]
    h = h * (1.0 + mod_ref[0, 4:5, :]) + mod_ref[0, 3:4, :]
    hb = h.astype(BF16)
    hf_ref[0] = hb
    logits = lax.dot_general(wrt_ref[...], hb, (((1,), (1,)), ((), ())), preferred_element_type=F32) + br_ref[...]
    row = lax.broadcasted_iota(jnp.int32, logits.shape, 0)
    neg = jnp.float32(-jnp.inf)
    big = jnp.int32(ROUTE_LANES)

    def first_argmax(vals, vmax):
        return jnp.min(jnp.where(vals == vmax, row, big), axis=0, keepdims=True)

    is_g = jnp.logical_and(row >= N_EXPERTS, row < N_EXPERTS + N_GROUPS)
    gl = jnp.where(is_g, logits, neg)
    g_max = jnp.max(gl, axis=0, keepdims=True)
    g_sel = first_argmax(gl, g_max) - N_EXPERTS
    p_g = 1.0 / jnp.sum(jnp.exp(gl - g_max), axis=0, keepdims=True)
    e_lo = g_sel * EXPERTS_PER_GROUP
    in_grp = jnp.logical_and(row >= e_lo, row < e_lo + EXPERTS_PER_GROUP)
    el = jnp.where(in_grp, logits, neg)
    e_max = jnp.max(el, axis=0, keepdims=True)
    i1 = first_argmax(el, e_max)
    el2 = jnp.where(row == i1, neg, el)
    e_max2 = jnp.max(el2, axis=0, keepdims=True)
    i2 = first_argmax(el2, e_max2)
    p2 = jnp.exp(e_max2 - e_max)
    w1 = p_g / (1.0 + p2)
    w2 = p_g * p2 / (1.0 + p2)
    route_ref[...] = jnp.where(row == i1, w1, jnp.where(row == i2, w2, 0.0)).astype(BF16)
    grp_ref[...] = jnp.broadcast_to(g_sel.astype(F32), grp_ref.shape)
    in_g = jnp.where(lax.broadcasted_iota(jnp.int32, grp_ref.shape, 0) == g_sel, 1.0, 0.0)
    cnt_ref[0, 0] = jnp.broadcast_to(jnp.sum(in_g, axis=1, keepdims=True), (SUBLANES, ROUTE_LANES))


def _outproj(a, dsum, gate, xs, mods, dnw, bd, woa, wod, nfw, wrt, br, *, n_ctx_tiles):
    b, t, d = xs.shape
    tm = TOKEN_TILE
    tok = lambda i, j: (j, i, 0)
    const = lambda i, j: (0, 0)
    flat = lambda i, j: (0, j * (t // tm) + i)
    modi = lambda i, j: (jnp.where(i < n_ctx_tiles, b, j), 0, 0)
    return pl.pallas_call(
        _outproj_kernel, grid=(t // tm, b),
        in_specs=[pl.BlockSpec((1, tm, ATTN_WIDTH), tok), pl.BlockSpec((1, tm, DN_WIDTH), tok),
                  pl.BlockSpec((1, tm, DN_WIDTH), tok), pl.BlockSpec((1, tm, d), tok),
                  pl.BlockSpec((1, 6, d), modi), pl.BlockSpec((1, DN_WIDTH), const),
                  pl.BlockSpec((DN_WIDTH, DN_WIDTH), const), pl.BlockSpec((ATTN_WIDTH, d), const),
                  pl.BlockSpec((DN_WIDTH, d), const), pl.BlockSpec((1, d), const),
                  pl.BlockSpec((ROUTE_LANES, d), const), pl.BlockSpec((ROUTE_LANES, 1), const)],
        out_specs=[pl.BlockSpec((1, tm, d), tok), pl.BlockSpec((1, tm, d), tok),
                   pl.BlockSpec((ROUTE_LANES, tm), flat), pl.BlockSpec((SUBLANES, tm), flat),
                   pl.BlockSpec((1, 1, SUBLANES, ROUTE_LANES), lambda i, j: (j, i, 0, 0))],
        out_shape=[jax.ShapeDtypeStruct((b, t, d), F32), jax.ShapeDtypeStruct((b, t, d), BF16),
                   jax.ShapeDtypeStruct((ROUTE_LANES, b * t), BF16),
                   jax.ShapeDtypeStruct((SUBLANES, b * t), F32),
                   jax.ShapeDtypeStruct((b, t // tm, SUBLANES, ROUTE_LANES), F32)],
        input_output_aliases={3: 0},
        compiler_params=_cparams("parallel", "parallel"),
        name="outproj_router",
    )(a, dsum, gate, xs, mods, dnw, bd, woa, wod, nfw, wrt, br)


def _moe_kernel(cnt_ref, h_ref, r_ref, grp_ref, tri_ref, w1_ref, w3_ref, w2_ref, f_ref, acc_sc, pos_sc,
                route_sc):
    m, g = pl.program_id(0), pl.program_id(1)
    tm = h_ref.shape[0]
    ff = w1_ref.shape[2]

    @pl.when(g == 0)
    def _():
        acc_sc[...] = jnp.zeros(acc_sc.shape, F32)
        route_sc[...] = r_ref[...].astype(F32).T.astype(BF16)
        grp = jnp.broadcast_to(grp_ref[0:1, :], (POS_ROWS * N_GROUPS, tm))
        row_g = lax.broadcasted_iota(jnp.int32, grp.shape, 0) // POS_ROWS
        member = grp == row_g.astype(F32)
        prefix = jnp.dot(member.astype(BF16), tri_ref[...], preferred_element_type=F32)
        pos_sc[...] = jnp.where(member, prefix - 1.0, -1.0)

    pos = pos_sc[pl.ds(pl.multiple_of(g * POS_ROWS, POS_ROWS), 1), :]
    e_row = lax.broadcasted_iota(jnp.int32, (ROUTE_LANES, EXPERTS_PER_GROUP * ff), 0)
    e_col = lax.broadcasted_iota(jnp.int32, (ROUTE_LANES, EXPERTS_PER_GROUP * ff), 1) // ff
    expand = (e_row == g * EXPERTS_PER_GROUP + e_col).astype(BF16)

    def sub_block(first_slot, rows):
        slot = lax.broadcasted_iota(jnp.int32, (rows, tm), 0) + first_slot
        sel = (pos == slot.astype(F32)).astype(BF16)
        xs = jnp.dot(sel, h_ref[...], preferred_element_type=F32).astype(BF16)
        wt = jnp.dot(sel, route_sc[...], preferred_element_type=F32).astype(BF16)
        wexp = jnp.dot(wt, expand, preferred_element_type=F32)
        y = None
        for e in range(EXPERTS_PER_GROUP):
            a = jnp.dot(xs, w1_ref[e], preferred_element_type=F32)
            gate = jnp.dot(xs, w3_ref[e], preferred_element_type=F32)
            mid = (_silu(a) * gate * wexp[:, e * ff:(e + 1) * ff]).astype(BF16)
            ye = jnp.dot(mid, w2_ref[e], preferred_element_type=F32)
            y = ye if y is None else y + ye
        acc_sc[...] += lax.dot_general(sel, y.astype(BF16), (((0,), (0,)), ((), ())),
                                       preferred_element_type=F32)

    count = cnt_ref[m * N_GROUPS + g]
    wide = jnp.logical_and(count > MOE_SUB, count <= MOE_WIDE[-1])
    n_full = jnp.where(wide, 0, count // MOE_SUB)
    rem = jnp.where(wide, 0, count - n_full * MOE_SUB)

    def full_block(sb, carry):
        sub_block(sb * MOE_SUB, MOE_SUB)
        return carry

    lax.fori_loop(0, n_full, full_block, 0)
    lo = 0
    for rows in MOE_TAILS:
        @pl.when(jnp.logical_and(rem > lo, rem <= rows))
        def _(rows=rows):
            sub_block(n_full * MOE_SUB, rows)
        lo = rows
    lo = MOE_SUB
    for rows in MOE_WIDE:
        @pl.when(jnp.logical_and(count > lo, count <= rows))
        def _(rows=rows):
            sub_block(jnp.int32(0), rows)
        lo = rows

    @pl.when(g == N_GROUPS - 1)
    def _():
        f_ref[...] = acc_sc[...].astype(f_ref.dtype)


def _moe(hf, route, grp, counts, tri, w1, w3, w2):
    n, d = hf.shape
    _, _, ff = w1.shape
    tm = tri.shape[0]
    epg = EXPERTS_PER_GROUP
    grid_spec = pltpu.PrefetchScalarGridSpec(
        num_scalar_prefetch=1, grid=(n // tm, N_GROUPS),
        in_specs=[pl.BlockSpec((tm, d), lambda i, g, c: (i, 0)),
                  pl.BlockSpec((ROUTE_LANES, tm), lambda i, g, c: (0, i)),
                  pl.BlockSpec((SUBLANES, tm), lambda i, g, c: (0, i)),
                  pl.BlockSpec((tm, tm), lambda i, g, c: (0, 0)),
                  pl.BlockSpec((epg, d, ff), lambda i, g, c: (g, 0, 0)),
                  pl.BlockSpec((epg, d, ff), lambda i, g, c: (g, 0, 0)),
                  pl.BlockSpec((epg, ff, d), lambda i, g, c: (g, 0, 0))],
        out_specs=pl.BlockSpec((tm, d), lambda i, g, c: (i, 0)),
        scratch_shapes=[pltpu.VMEM((tm, d), F32), pltpu.VMEM((POS_ROWS * N_GROUPS, tm), F32),
                        pltpu.VMEM((tm, ROUTE_LANES), BF16)])
    return pl.pallas_call(
        _moe_kernel, grid_spec=grid_spec,
        out_shape=jax.ShapeDtypeStruct((n, d), BF16),
        compiler_params=_cparams("parallel", "arbitrary"),
        name="moe_experts",
    )(counts, hf, route, grp, tri, w1, w3, w2)


def _final_kernel(x_ref, f_ref, mod_ref, w_ref, o_ref):
    x = x_ref[0] + mod_ref[0, 5:6, :] * f_ref[0].astype(F32)
    o_ref[0] = x * lax.rsqrt(jnp.mean(x * x, axis=-1, keepdims=True) + EPS) * w_ref[...]


def _final_norm(xs, f_prev, mods, w, *, n_ctx_tiles):
    b, t, d = xs.shape
    tm = TOKEN_TILE
    n_lat = t // tm - n_ctx_tiles
    lat = lambda i, j: (j, i + n_ctx_tiles, 0)
    return pl.pallas_call(
        _final_kernel, grid=(n_lat, b),
        in_specs=[pl.BlockSpec((1, tm, d), lat), pl.BlockSpec((1, tm, d), lat),
                  pl.BlockSpec((1, 6, d), lambda i, j: (j, 0, 0)), pl.BlockSpec((1, d), lambda i, j: (0, 0))],
        out_specs=pl.BlockSpec((1, tm, d), lambda i, j: (j, i, 0)),
        out_shape=jax.ShapeDtypeStruct((b, n_lat * tm, d), F32),
        compiler_params=_cparams("parallel", "parallel"),
        name="final_norm",
    )(xs, f_prev, mods, w)


def _rope_tables(n_ctx, n_lat):
    pos = jnp.arange(n_lat, dtype=jnp.int32)
    inv = ROPE_THETA ** (-jnp.arange(ROPE_NF, dtype=F32) / ROPE_NF)
    ang_r = (pos // GRID_W).astype(F32)[:, None] * inv
    ang_c = (pos % GRID_W).astype(F32)[:, None] * inv
    cos = jnp.concatenate([jnp.cos(ang_r)] * 2 + [jnp.cos(ang_c)] * 2, axis=-1)
    sin = jnp.concatenate([-jnp.sin(ang_r), jnp.sin(ang_r), -jnp.sin(ang_c), jnp.sin(ang_c)], axis=-1)
    cos = jnp.concatenate([jnp.ones((n_ctx, HEAD_DIM), F32), cos], axis=0)
    sin = jnp.concatenate([jnp.zeros((n_ctx, HEAD_DIM), F32), sin], axis=0)
    reps = QK_WIDTH // HEAD_DIM
    return jnp.tile(cos, (1, reps)), jnp.tile(sin, (1, reps))


def kernel(x, c, ctx, c_ctx, ada_w, ada_b, norm_mix_w, norm_ffn_w, w_in, q_norm_w, k_norm_w, conv_w,
           dn_A_log, dn_dt_bias, dn_norm_w, w_out, rg_w, rg_b, re_w, re_b, w1, w3, w2, final_norm_w):
    b, s, d = x.shape
    n_ctx = ctx.shape[1]
    depth = w_in.shape[0]
    t = n_ctx + s
    assert n_ctx % TOKEN_TILE == 0 and s % TOKEN_TILE == 0 and s % GRID_W == 0
    n_ctx_tiles = n_ctx // TOKEN_TILE

    xs = jnp.concatenate([ctx, x], axis=1)
    mod_rows = -(-(b + BATCH_PER_STEP) // SUBLANES) * SUBLANES
    cs = jnp.zeros((mod_rows, d), F32).at[:b].set(c).at[b:b + BATCH_PER_STEP].set(c_ctx)
    mods = _ada_mods(cs, ada_w, ada_b).reshape(depth, mod_rows, 6, d)

    cos_t, sin_t = _rope_tables(n_ctx, s)
    bd_qk = _block_ones(QK_WIDTH, HEAD_DIM)
    bd_dn = _block_ones(DN_WIDTH, DN_DIM)
    zeros16 = jnp.zeros((2 * DN_HEADS,), F32)
    n_main = w_in.shape[2] - 4 * DN_HEADS
    lane_i = jnp.arange((DN_HEADS // DN_GROUP) * SCALAR_LANES)
    gi, li = lane_i // SCALAR_LANES, lane_i % SCALAR_LANES
    di, ki, hi = li // (2 * DN_GROUP), (li // DN_GROUP) % 2, li % DN_GROUP
    bg_used = li < 4 * DN_GROUP
    bg_src = jnp.where(bg_used, ki * 2 * DN_HEADS + di * DN_HEADS + gi * DN_GROUP + hi, 0)
    moe_tile = next(c for c in MOE_TILES if (b * t) % c == 0)
    tri = (lax.broadcasted_iota(jnp.int32, (moe_tile, moe_tile), 0)
           <= lax.broadcasted_iota(jnp.int32, (moe_tile, moe_tile), 1)).astype(BF16)

    f_prev = None
    for l in range(depth):
        qkw = jnp.concatenate([jnp.tile(q_norm_w[l], ATTN_HEADS), jnp.tile(k_norm_w[l], KV_HEADS)])[None]
        dnp = jnp.stack([jnp.concatenate([zeros16, dn_A_log[l].reshape(-1)]),
                         jnp.concatenate([zeros16, dn_dt_bias[l].reshape(-1)])])
        dnp = jnp.where(bg_used, jnp.take(dnp, bg_src, axis=1), 0.0)
        w_tail = jnp.where(bg_used, jnp.take(w_in[l][:, n_main:], bg_src, axis=1), 0.0)
        w_in_l = jnp.concatenate([w_in[l][:, :n_main], w_tail], axis=1).astype(BF16)
        outs = _inproj(xs, f_prev, mods[l - 1] if l else None, mods[l], norm_mix_w[l][None],
                       w_in_l, qkw, cos_t, sin_t, bd_qk, dnp, n_ctx_tiles=n_ctx_tiles)
        qt, k, vt, dqkv, gate, bg = outs[:6]
        if l:
            xs = outs[6]
        a = _attention(qt, k, vt, n_ctx=n_ctx)
        dsum = _deltanet(dqkv, conv_w[l], bg, n_ctx=n_ctx)
        wo = w_out[l].astype(BF16)
        wrt = jnp.zeros((ROUTE_LANES, d), F32).at[:N_EXPERTS].set(re_w[l].T).at[
            N_EXPERTS:N_EXPERTS + N_GROUPS].set(rg_w[l].T).astype(BF16)
        br = jnp.zeros((ROUTE_LANES, 1), F32).at[:N_EXPERTS, 0].set(re_b[l]).at[
            N_EXPERTS:N_EXPERTS + N_GROUPS, 0].set(rg_b[l])
        xs, hf, route, grp, cnt = _outproj(
            a, dsum, gate, xs, mods[l], jnp.tile(dn_norm_w[l], DN_HEADS)[None], bd_dn,
            wo[:ATTN_WIDTH], wo[ATTN_WIDTH:], norm_ffn_w[l][None], wrt, br, n_ctx_tiles=n_ctx_tiles)
        counts = cnt[:, :, :N_GROUPS, 0].reshape(-1, moe_tile // TOKEN_TILE, N_GROUPS).sum(axis=1)
        f_prev = _moe(hf.reshape(b * t, d), route, grp,
                      counts.astype(jnp.int32).reshape(-1), tri,
                      w1[l].astype(BF16), w3[l].astype(BF16), w2[l].astype(BF16)).reshape(b, t, d)
    return _final_norm(xs, f_prev, mods[depth - 1], final_norm_w[None], n_ctx_tiles=n_ctx_tiles)
```

```python
import functools
import math

import jax
import jax.numpy as jnp
from jax import lax
from jax.experimental import pallas as pl
from jax.experimental.pallas import tpu as pltpu

F32 = jnp.float32
BF16 = jnp.bfloat16
HIGHEST = lax.Precision.HIGHEST

GRID_W = 64
HEAD_DIM = 64
ATTN_HEADS = 8
KV_HEADS = 2
GQA_GROUP = ATTN_HEADS // KV_HEADS
ATTN_WIDTH = ATTN_HEADS * HEAD_DIM
KV_WIDTH = KV_HEADS * HEAD_DIM
ATTN_SCALE = HEAD_DIM ** -0.5
LOG2_E = math.log2(math.e)
ROPE_THETA = 10000.0
ROPE_NF = HEAD_DIM // 4
DN_HEADS = 8
DN_DIM = 64
DN_WIDTH = DN_HEADS * DN_DIM
DN_SCALE = DN_DIM ** -0.5
DN_GROUP = 4
CONV_K = 5
CHUNK = 64
N_GROUPS = 4
EXPERTS_PER_GROUP = 8
N_EXPERTS = N_GROUPS * EXPERTS_PER_GROUP
EPS = 1e-6
QK_WIDTH = ATTN_WIDTH + KV_WIDTH
SUBLANES = 8
TOKEN_TILE = 256
BATCH_PER_STEP = 2
ATTN_KEY_TILE = 256
ONES_ROWS = 16
HEADS_PER_PASS = 8
HALO = 16
ROUTE_LANES = 128
MOE_TILES = (1024, 512, 256)
MOE_SUB = 256
MOE_TAILS = (64, 128, 256)
MOE_WIDE = (320, 384)
POS_ROWS = 16
VMEM_LIMIT = 56 * 1024 * 1024


def _cparams(*sem):
    return pltpu.CompilerParams(dimension_semantics=sem, vmem_limit_bytes=VMEM_LIMIT)


def _silu(x):
    return x * jax.nn.sigmoid(x)


def _block_ones(n, blk):
    i = lax.broadcasted_iota(jnp.int32, (n, n), 0) // blk
    j = lax.broadcasted_iota(jnp.int32, (n, n), 1) // blk
    return (i == j).astype(BF16)


def _group_mean_sq(x, ones_bd, width):
    return jnp.dot((x * x).astype(BF16), ones_bd, preferred_element_type=F32) * (1.0 / width)


def _ada_kernel(cs_ref, w_ref, b_ref, o_ref):
    o_ref[0] = jnp.dot(_silu(cs_ref[...]), w_ref[0], preferred_element_type=F32,
                       precision=HIGHEST) + b_ref[0]


def _ada_mods(cs, ada_w, ada_b):
    depth, d, n6 = ada_w.shape
    rows = cs.shape[0]
    tn = 1536
    return pl.pallas_call(
        _ada_kernel,
        grid=(depth, n6 // tn),
        in_specs=[pl.BlockSpec((rows, d), lambda l, j: (0, 0)),
                  pl.BlockSpec((1, d, tn), lambda l, j: (l, 0, j)),
                  pl.BlockSpec((1, 1, tn), lambda l, j: (l, 0, j))],
        out_specs=pl.BlockSpec((1, rows, tn), lambda l, j: (l, 0, j)),
        out_shape=jax.ShapeDtypeStruct((depth, rows, n6), F32),
        compiler_params=_cparams("parallel", "parallel"),
        name="ada_mods",
    )(cs, ada_w, ada_b.reshape(depth, 1, n6))


def _inproj_kernel(*refs, has_prev):
    if has_prev:
        (x_ref, f_ref, modp_ref, mod_ref, nw_ref, win_ref, qkw_ref, cos_ref, sin_ref, bd_ref, dnp_ref,
         qt_ref, k_ref, vt_ref, dqkv_ref, gate_ref, bg_ref, xo_ref) = refs
    else:
        (x_ref, mod_ref, nw_ref, win_ref, qkw_ref, cos_ref, sin_ref, bd_ref, dnp_ref,
         qt_ref, k_ref, vt_ref, dqkv_ref, gate_ref, bg_ref) = refs
    nb, tm = x_ref.shape[0], x_ref.shape[1]
    hs = []
    for r in range(nb):
        x = x_ref[r]
        if has_prev:
            x = x + modp_ref[r, 5:6, :] * f_ref[r].astype(F32)
            xo_ref[r] = x
        h = x * lax.rsqrt(jnp.mean(x * x, axis=-1, keepdims=True) + EPS) * nw_ref[...]
        hs.append((h * (1.0 + mod_ref[r, 1:2, :]) + mod_ref[r, 0:1, :]).astype(BF16))
    acc = jnp.dot(jnp.concatenate(hs, axis=0), win_ref[...], preferred_element_type=F32)
    qk = acc[:, :QK_WIDTH]
    qn = qk * lax.rsqrt(_group_mean_sq(qk, bd_ref[...], HEAD_DIM) + EPS) * qkw_ref[...]
    lane = lax.broadcasted_iota(jnp.int32, qn.shape, 1)
    partner = jnp.where(lane % (2 * ROPE_NF) < ROPE_NF,
                        pltpu.roll(qn, QK_WIDTH - ROPE_NF, 1), pltpu.roll(qn, ROPE_NF, 1))
    qr = (qn * jnp.concatenate([cos_ref[...]] * nb, axis=0)
          + partner * jnp.concatenate([sin_ref[...]] * nb, axis=0))
    c_v = QK_WIDTH
    c_dn = c_v + KV_WIDTH
    c_gate = c_dn + 3 * DN_WIDTH
    c_bg = c_gate + DN_WIDTH
    z = acc[:, c_bg:]
    zb = z + dnp_ref[1:2, :]
    softplus = jnp.maximum(zb, 0.0) + jnp.log1p(jnp.exp(-jnp.abs(zb)))
    lane_z = lax.broadcasted_iota(jnp.int32, z.shape, 1)
    bg = jnp.where(lane_z % (2 * DN_GROUP) < DN_GROUP, jax.nn.sigmoid(z), -jnp.exp(dnp_ref[0:1, :]) * softplus)
    for r in range(nb):
        rs = slice(r * tm, (r + 1) * tm)
        qt_ref[r] = (qr[rs, :ATTN_WIDTH] * (ATTN_SCALE * LOG2_E)).T.astype(BF16)
        k_ref[r] = qr[rs, ATTN_WIDTH:].astype(BF16)
        vt_ref[r] = acc[rs, c_v:c_dn].T.astype(BF16)
        dqkv_ref[r] = acc[rs, c_dn:c_gate].astype(BF16)
        gate_ref[r] = acc[rs, c_gate:c_bg].astype(BF16)
        for gi in range(DN_HEADS // DN_GROUP):
            bg_ref[r, gi] = bg[rs, gi * SCALAR_LANES:(gi + 1) * SCALAR_LANES]


def _inproj(xs, f_prev, mods_prev, mods, nw, win, qkw, cos_t, sin_t, bd, dnp, *, n_ctx_tiles):
    b, t, d = xs.shape
    tm = TOKEN_TILE
    nb = BATCH_PER_STEP
    assert b % nb == 0
    nt = t // tm
    proj = win.shape[1]
    has_prev = f_prev is not None
    tok = lambda i, j: (j, i, 0)
    modi = lambda i, j: (jnp.where(i < n_ctx_tiles, b // nb, j), 0, 0)
    const = lambda i, j: (0, 0)
    in_specs = [pl.BlockSpec((nb, tm, d), tok)]
    args = [xs]
    if has_prev:
        in_specs += [pl.BlockSpec((nb, tm, d), tok), pl.BlockSpec((nb, 6, d), modi)]
        args += [f_prev, mods_prev]
    in_specs += [pl.BlockSpec((nb, 6, d), modi), pl.BlockSpec((1, d), const),
                 pl.BlockSpec((d, proj), const), pl.BlockSpec((1, QK_WIDTH), const),
                 pl.BlockSpec((tm, QK_WIDTH), lambda i, j: (i, 0)),
                 pl.BlockSpec((tm, QK_WIDTH), lambda i, j: (i, 0)),
                 pl.BlockSpec((QK_WIDTH, QK_WIDTH), const), pl.BlockSpec((2, dnp.shape[1]), const)]
    args += [mods, nw, win, qkw, cos_t, sin_t, bd, dnp]
    seq_out = lambda w, dt: (pl.BlockSpec((nb, tm, w), tok), jax.ShapeDtypeStruct((b, t, w), dt))
    tr_out = lambda w: (pl.BlockSpec((nb, w, tm), lambda i, j: (j, 0, i)), jax.ShapeDtypeStruct((b, w, t), BF16))
    ng = DN_HEADS // DN_GROUP
    outs = [tr_out(ATTN_WIDTH), seq_out(KV_WIDTH, BF16), tr_out(KV_WIDTH),
            seq_out(3 * DN_WIDTH, BF16), seq_out(DN_WIDTH, BF16),
            (pl.BlockSpec((nb, ng, tm, SCALAR_LANES), lambda i, j: (j, 0, i, 0)),
             jax.ShapeDtypeStruct((b, ng, t, SCALAR_LANES), F32))]
    out_specs = [o[0] for o in outs]
    out_shape = [o[1] for o in outs]
    aliases = {}
    if has_prev:
        out_specs.append(pl.BlockSpec((nb, tm, d), tok))
        out_shape.append(jax.ShapeDtypeStruct((b, t, d), F32))
        aliases = {0: len(out_shape) - 1}
    return pl.pallas_call(
        functools.partial(_inproj_kernel, has_prev=has_prev),
        grid=(nt, b // nb), in_specs=in_specs, out_specs=out_specs, out_shape=out_shape,
        input_output_aliases=aliases,
        compiler_params=_cparams("parallel", "parallel"),
        name="inproj",
    )(*args)


def _attn_kernel(qt_ref, k_ref, vt_ref, o_ref, acc_sc, sa_sc, sb_sc, *, tk, n_ctx_q, n_ctx_k, n_all_k):
    tq = qt_ref.shape[2]
    n_kv = jnp.where(pl.program_id(1) < n_ctx_q, n_ctx_k, n_all_k)
    nh = HEADS_PER_PASS
    local = range(nh)
    last = n_kv - 1

    def tile_rows(i):
        return pl.ds(pl.multiple_of(i * tk, tk), tk)

    for h0 in range(0, ATTN_HEADS, nh):
        kv = [(h0 + j) // GQA_GROUP for j in local]
        acc_sc[...] = jnp.zeros(acc_sc.shape, F32)

        def scores(i, s_sc, h0=h0, kv=kv):
            rows = tile_rows(i)
            for j in local:
                s_sc[j] = jnp.dot(k_ref[0, rows, kv[j] * HEAD_DIM:(kv[j] + 1) * HEAD_DIM],
                                  qt_ref[0, (h0 + j) * HEAD_DIM:(h0 + j + 1) * HEAD_DIM, :],
                                  preferred_element_type=F32)

        def softmax_pv(i, s_sc, carry, kv=kv):
            rows = tile_rows(i)
            m_prev, l_prev = carry[:nh], carry[nh:]
            s = [s_sc[j] for j in local]
            m_new = [jnp.maximum(m_prev[j], jnp.max(s[j], axis=0, keepdims=True)) for j in local]
            p = [jnp.exp2(s[j] - m_new[j]) for j in local]
            alpha = [jnp.exp2(m_prev[j] - m_new[j]) for j in local]
            ones = jnp.ones((ONES_ROWS, tk), BF16)
            vt1 = {g: jnp.concatenate([vt_ref[0, g * HEAD_DIM:(g + 1) * HEAD_DIM, rows], ones], axis=0)
                   for g in sorted(set(kv))}
            pv = [jnp.dot(vt1[kv[j]], p[j].astype(BF16), preferred_element_type=F32) for j in local]
            for j in local:
                acc_sc[j] = alpha[j] * acc_sc[j] + pv[j][:HEAD_DIM]
            l_new = [alpha[j] * l_prev[j] + pv[j][HEAD_DIM:HEAD_DIM + 1] for j in local]
            return tuple(m_new) + tuple(l_new)

        init = (jnp.full((1, tq), -jnp.inf, F32),) * nh + (jnp.zeros((1, tq), F32),) * nh
        scores(0, sa_sc)
        scores(jnp.minimum(1, last), sb_sc)
        stats = softmax_pv(0, sa_sc, init)

        def pair(j, carry, scores=scores, softmax_pv=softmax_pv):
            a = 2 * j + 1
            scores(a + 1, sa_sc)
            carry = softmax_pv(a, sb_sc, carry)
            scores(jnp.minimum(a + 2, last), sb_sc)
            return softmax_pv(a + 1, sa_sc, carry)

        stats = lax.fori_loop(0, last // 2, pair, stats)
        og = jnp.concatenate([acc_sc[j] / stats[nh + j] for j in local], axis=0)
        o_ref[0, :, h0 * HEAD_DIM:(h0 + nh) * HEAD_DIM] = og.T.astype(BF16)


def _attention(qt, k, vt, *, n_ctx):
    b, _, t = qt.shape
    tq, tk = TOKEN_TILE, ATTN_KEY_TILE
    assert (n_ctx // tk) % 2 == 1 and (t // tk) % 2 == 1
    kern = functools.partial(_attn_kernel, tk=tk, n_ctx_q=n_ctx // tq, n_ctx_k=n_ctx // tk, n_all_k=t // tk)
    s_buf = pltpu.VMEM((HEADS_PER_PASS, tk, tq), F32)
    return pl.pallas_call(
        kern, grid=(b, t // tq),
        in_specs=[pl.BlockSpec((1, ATTN_WIDTH, tq), lambda bi, i: (bi, 0, i)),
                  pl.BlockSpec((1, t, KV_WIDTH), lambda bi, i: (bi, 0, 0)),
                  pl.BlockSpec((1, KV_WIDTH, t), lambda bi, i: (bi, 0, 0))],
        out_specs=pl.BlockSpec((1, tq, ATTN_WIDTH), lambda bi, i: (bi, i, 0)),
        out_shape=jax.ShapeDtypeStruct((b, t, ATTN_WIDTH), BF16),
        scratch_shapes=[pltpu.VMEM((HEADS_PER_PASS, HEAD_DIM, tq), F32), s_buf, s_buf],
        compiler_params=_cparams("parallel", "parallel"),
        name="gqa_attention",
    )(qt, k, vt)


DN_LANES = DN_GROUP * DN_DIM
DN_BLOCK = 4
INV_BLOCK = 16
SCALAR_LANES = 128


def _split3(x):
    x1 = x.astype(BF16)
    r1 = x - x1.astype(F32)
    x2 = r1.astype(BF16)
    x3 = (r1 - x2.astype(F32)).astype(BF16)
    return jnp.concatenate([x1, x2, x3], axis=1)


def _sum3(y):
    w = y.shape[1] // 3
    return y[:, :w] + y[:, w:2 * w] + y[:, 2 * w:]


def _dn_kernel(dq_ref, dk_ref, dv_ref, wq_ref, wk_ref, wv_ref, bg_ref, o_ref, mp_sc, n_sc, r_sc, gam_sc, *,
               n_chunks, n_ctx_chunks):
    c = CHUNK
    rows_b = DN_BLOCK * c
    ln = DN_LANES
    ri = lax.broadcasted_iota(jnp.int32, (rows_b, ln), 0)
    li = lax.broadcasted_iota(jnp.int32, (rows_b, ln), 1)
    same = (ri // c) == (li // c)
    i_in, j_in = ri % c, li % c
    eye_rc = i_in == j_in
    near = (i_in // INV_BLOCK) == (j_in // INV_BLOCK)
    e_row = lax.broadcasted_iota(jnp.int32, (3 * SCALAR_LANES, 2 * ln), 0) % SCALAR_LANES
    e_blk = lax.broadcasted_iota(jnp.int32, (3 * SCALAR_LANES, 2 * ln), 1) // c
    zl = lax.broadcasted_iota(jnp.int32, (rows_b, SCALAR_LANES), 1)
    nt_dims = (((1,), (1,)), ((), ()))
    tn_dims = (((0,), (0,)), ((), ()))

    def bdiag(x):
        return jnp.where(same, jnp.concatenate([x] * DN_GROUP, axis=0), jnp.zeros((), x.dtype))

    def fold(x):
        x = jnp.where(same, x, 0.0)
        return x[0:c] + x[c:2 * c] + x[2 * c:3 * c] + x[3 * c:4 * c]

    dirs = range(2)
    incl = [i_in >= j_in, i_in <= j_in]
    strict = [i_in > j_in, i_in < j_in]
    tri_bd = [jnp.logical_and(same, incl[d]).astype(BF16) for d in dirs]
    sel_last = [jnp.logical_and(same, j_in == (c - 1, 0)[d]).astype(BF16) for d in dirs]
    same_b = same.astype(BF16)
    col0 = [d * 2 * DN_GROUP for d in dirs]
    expand = [(e_row == e_blk + col0[d]).astype(BF16) for d in dirs]
    is_beta = [jnp.logical_and(zl >= col0[d], zl < col0[d] + DN_GROUP) for d in dirs]
    eye_f = eye_rc.astype(F32)
    chunks = range(DN_BLOCK)
    sls = [slice(ch * c, (ch + 1) * c) for ch in chunks]
    mm = lambda a, bmat: jnp.dot(a.astype(BF16), bmat, preferred_element_type=F32)

    def prepare(blk, carry):
        rs = pl.ds(pl.multiple_of(blk * rows_b, rows_b), rows_b)
        z = bg_ref[0, 0, rs, :]
        r0 = blk * rows_b
        keep_prev = jnp.where(jnp.logical_or(blk == 0, blk == n_ctx_chunks // DN_BLOCK), 0.0, 1.0)
        keep_next = jnp.where(jnp.logical_or(blk == n_ctx_chunks // DN_BLOCK - 1,
                                             blk == n_chunks // DN_BLOCK - 1), 0.0, 1.0)
        prev_rows = pl.ds(pl.multiple_of(jnp.maximum(r0 - HALO, 0), HALO), HALO)
        next_rows = pl.ds(pl.multiple_of(jnp.minimum(r0 + rows_b, n_chunks * c - HALO), HALO), HALO)

        def conv_silu(x_ref, w_ref):
            ext = jnp.concatenate([x_ref[0, prev_rows, :].astype(F32)[HALO - SUBLANES:] * keep_prev,
                                   x_ref[0, rs, :].astype(F32),
                                   x_ref[0, next_rows, :].astype(F32)[:SUBLANES] * keep_next], axis=0)
            y = None
            for j in range(CONV_K):
                shift = (CONV_K // 2 - j) % (rows_b + 2 * SUBLANES)
                tap = ext if shift == 0 else pltpu.roll(ext, shift, 0)
                term = tap[SUBLANES:SUBLANES + rows_b] * w_ref[j:j + 1, :]
                y = term if y is None else y + term
            return _silu(y)

        def l2n(x):
            return x * lax.rsqrt(jnp.dot((x * x).astype(BF16), same_b, preferred_element_type=F32) + EPS)

        q = l2n(conv_silu(dq_ref, wq_ref)) * DN_SCALE
        k = l2n(conv_silu(dk_ref, wk_ref))
        v = conv_silu(dv_ref, wv_ref)
        qbf = q.astype(BF16)
        kbf = k.astype(BF16)
        sc = [lax.dot_general(jnp.concatenate([kbf[sl], qbf[sl]], axis=0), bdiag(kbf[sl]), nt_dims,
                              preferred_element_type=F32) for sl in sls]
        lfull, intra, vb, kbe, q_dec, k_dec, gamma = [], [], [], [], [], [], []
        for d in dirs:
            cum = _sum3(jnp.dot(tri_bd[d], _split3(z), preferred_element_type=F32))
            ex = jnp.dot(_split3(jnp.where(is_beta[d], z, cum)), expand[d], preferred_element_type=F32)
            beta, gce = ex[:, :ln], ex[:, ln:]
            g_row = _sum3(jnp.dot(same_b, _split3(jnp.where(eye_rc, gce, 0.0)), preferred_element_type=F32))
            g_last = _sum3(jnp.dot(sel_last[d], _split3(gce), preferred_element_type=F32))
            decay = jnp.where(incl[d], jnp.exp(jnp.where(incl[d], gce - g_row, 0.0)), 0.0)
            bdecay = jnp.where(strict[d], beta * decay, 0.0)
            e_g = jnp.exp(gce)
            kb = k * beta
            kbe_d = (kb * e_g).astype(BF16)
            vb_d = (v * beta).astype(BF16)
            q_dec_d = q * e_g
            k_dec_d = (k * jnp.exp(g_last - gce)).astype(BF16)
            gamma_d = jnp.exp(g_last)
            lfull += [s[:c] * bdecay[sl] for s, sl in zip(sc, sls)]
            intra += [(s[c:] * decay[sl]).astype(BF16) for s, sl in zip(sc, sls)]
            vb += [vb_d[sl] for sl in sls]
            kbe += [kbe_d[sl] for sl in sls]
            q_dec += [q_dec_d[sl] for sl in sls]
            k_dec += [k_dec_d[sl] for sl in sls]
            gamma += [gamma_d[ch * c:ch * c + SUBLANES] for ch in chunks]
        near_c = near[sls[0]]
        p = [jnp.where(near_c, -lf, 0.0) for lf in lfull]
        l_off = [jnp.where(near_c, 0.0, lf).astype(BF16) for lf in lfull]
        tmat = [eye_f[sls[0]] + pi for pi in p]
        p = [mm(pi, bdiag(pi.astype(BF16))) for pi in p]
        for _ in range(int(math.log2(INV_BLOCK)) - 2):
            res = [mm(jnp.concatenate([ti, pi], axis=0), bdiag(pi.astype(BF16))) for ti, pi in zip(tmat, p)]
            tmat = [ti + ri[:c] for ti, ri in zip(tmat, res)]
            p = [ri[c:] for ri in res]
        tmat = [ti + mm(ti, bdiag(pi.astype(BF16))) for ti, pi in zip(tmat, p)]
        nmat = [mm(ti, bdiag(lo)) for ti, lo in zip(tmat, l_off)]
        n2 = [mm(ni, bdiag(ni.astype(BF16))) for ni in nmat]
        tmat = [ti + mm(qi, bdiag(ti.astype(BF16))) for ti, qi in zip(tmat, n2)]
        tmat = [ti - mm(ni, bdiag(ti.astype(BF16))) for ti, ni in zip(tmat, nmat)]
        uw = [mm(ti, jnp.concatenate([bdiag(vi), bdiag(ki)], axis=1)).astype(BF16)
              for ti, vi, ki in zip(tmat, vb, kbe)]
        aw_au = [mm(ai, jnp.concatenate([bdiag(x[:, ln:]), bdiag(x[:, :ln])], axis=1))
                 for ai, x in zip(intra, uw)]
        mn = [lax.dot_general(kd, jnp.concatenate([x[:, ln:], x[:, :ln]], axis=1), tn_dims,
                              preferred_element_type=F32) for kd, x in zip(k_dec, uw)]
        for d in dirs:
            for ch in chunks:
                st = d * DN_BLOCK + ch
                ci = blk * DN_BLOCK + ch
                p_c = q_dec[st] - aw_au[st][:, :ln]
                mp_sc[d, ci] = jnp.concatenate([fold(mn[st][:, :ln]), p_c], axis=0).astype(BF16)
                n_sc[d, ci] = fold(mn[st][:, ln:])
                r_sc[d, ci] = aw_au[st][:, ln:].astype(BF16)
                gam_sc[d, ci] = gamma[st]
        return carry

    lax.fori_loop(0, n_chunks // DN_BLOCK, prepare, 0)

    o_ref[...] = jnp.zeros(o_ref.shape, F32)

    def scan(s, states):
        ci = (s, jnp.where(s < n_ctx_chunks, n_ctx_chunks - 1 - s, n_chunks - 1 + n_ctx_chunks - s))
        res = [jnp.dot(mp_sc[d, ci[d]], bdiag(states[d].astype(BF16)), preferred_element_type=F32)
               for d in dirs]
        for d in dirs:
            rows = pl.ds(pl.multiple_of(ci[d] * c, c), c)
            o_ref[0, rows, :] += res[d][c:] + r_sc[d, ci[d]].astype(F32)
        return tuple(gam_sc[d, ci[d]][0:1] * states[d] - res[d][:c] + n_sc[d, ci[d]] for d in dirs)

    zero = jnp.zeros((c, ln), F32)
    lax.fori_loop(0, n_chunks, scan, (zero, zero), unroll=DN_BLOCK)


def _deltanet(dqkv, conv_w, bg, *, n_ctx):
    b, t, _ = dqkv.shape
    n_chunks = t // CHUNK
    block_rows = DN_BLOCK * CHUNK
    assert t % block_rows == 0 and n_ctx % block_rows == 0 and block_rows == DN_LANES == DN_GROUP * DN_DIM
    ng = DN_HEADS // DN_GROUP
    once = dict(pipeline_mode=pl.Buffered(1))
    part = lambda s: pl.BlockSpec((1, t, DN_LANES), lambda bi, gi, s=s: (bi, 0, s * ng + gi))
    wpart = lambda s: pl.BlockSpec((CONV_K, DN_LANES), lambda bi, gi, s=s: (0, s * ng + gi))
    kern = functools.partial(_dn_kernel, n_chunks=n_chunks, n_ctx_chunks=n_ctx // CHUNK)
    return pl.pallas_call(
        kern, grid=(b, ng),
        in_specs=[part(0), part(1), part(2), wpart(0), wpart(1), wpart(2),
                  pl.BlockSpec((1, 1, t, SCALAR_LANES), lambda bi, gi: (bi, gi, 0, 0), **once)],
        out_specs=pl.BlockSpec((1, t, DN_LANES), lambda bi, gi: (bi, 0, gi)),
        out_shape=jax.ShapeDtypeStruct((b, t, DN_WIDTH), F32),
        scratch_shapes=[pltpu.VMEM((2, n_chunks, 2 * CHUNK, DN_LANES), BF16),
                        pltpu.VMEM((2, n_chunks, CHUNK, DN_LANES), F32),
                        pltpu.VMEM((2, n_chunks, CHUNK, DN_LANES), BF16),
                        pltpu.VMEM((2, n_chunks, SUBLANES, DN_LANES), F32)],
        compiler_params=_cparams("parallel", "parallel"),
        name="gated_deltanet",
    )(dqkv, dqkv, dqkv, conv_w, conv_w, conv_w, bg)


def _outproj_kernel(a_ref, d_ref, gate_ref, x_ref, mod_ref, dnw_ref, bd_ref, woa_ref, wod_ref, nfw_ref,
                    wrt_ref, br_ref, xo_ref, hf_ref, route_ref, grp_ref, cnt_ref):
    dd = d_ref[0]
    gate = gate_ref[0].astype(F32)
    dn = dd * lax.rsqrt(_group_mean_sq(dd, bd_ref[...], DN_DIM) + EPS) * dnw_ref[...] * _silu(gate)
    y = (jnp.dot(a_ref[0], woa_ref[...], preferred_element_type=F32)
         + jnp.dot(dn.astype(BF16), wod_ref[...], preferred_element_type=F32))
    x = x_ref[0] + mod_ref[0, 2:3, :] * y
    xo_ref[0] = x
    h = x * lax.rsqrt(jnp.mean(x * x, axis=-1, keepdims=True) + EPS) * nfw_ref[...]
    h = h * (1.0 + mod_ref[0, 4:5, :]) + mod_ref[0, 3:4, :]
    hb = h.astype(BF16)
    hf_ref[0] = hb
    logits = lax.dot_general(wrt_ref[...], hb, (((1,), (1,)), ((), ())), preferred_element_type=F32) + br_ref[...]
    row = lax.broadcasted_iota(jnp.int32, logits.shape, 0)
    neg = jnp.float32(-jnp.inf)
    big = jnp.int32(ROUTE_LANES)

    def first_argmax(vals, vmax):
        return jnp.min(jnp.where(vals == vmax, row, big), axis=0, keepdims=True)

    is_g = jnp.logical_and(row >= N_EXPERTS, row < N_EXPERTS + N_GROUPS)
    gl = jnp.where(is_g, logits, neg)
    g_max = jnp.max(gl, axis=0, keepdims=True)
    g_sel = first_argmax(gl, g_max) - N_EXPERTS
    p_g = 1.0 / jnp.sum(jnp.exp(gl - g_max), axis=0, keepdims=True)
    e_lo = g_sel * EXPERTS_PER_GROUP
    in_grp = jnp.logical_and(row >= e_lo, row < e_lo + EXPERTS_PER_GROUP)
    el = jnp.where(in_grp, logits, neg)
    e_max = jnp.max(el, axis=0, keepdims=True)
    i1 = first_argmax(el, e_max)
    el2 = jnp.where(row == i1, neg, el)
    e_max2 = jnp.max(el2, axis=0, keepdims=True)
    i2 = first_argmax(el2, e_max2)
    p2 = jnp.exp(e_max2 - e_max)
    w1 = p_g / (1.0 + p2)
    w2 = p_g * p2 / (1.0 + p2)
    route_ref[...] = jnp.where(row == i1, w1, jnp.where(row == i2, w2, 0.0)).astype(BF16)
    grp_ref[...] = jnp.broadcast_to(g_sel.astype(F32), grp_ref.shape)
    in_g = jnp.where(lax.broadcasted_iota(jnp.int32, grp_ref.shape, 0) == g_sel, 1.0, 0.0)
    cnt_ref[0, 0] = jnp.broadcast_to(jnp.sum(in_g, axis=1, keepdims=True), (SUBLANES, ROUTE_LANES))


def _outproj(a, dsum, gate, xs, mods, dnw, bd, woa, wod, nfw, wrt, br, *, n_ctx_tiles):
    b, t, d = xs.shape
    tm = TOKEN_TILE
    tok = lambda i, j: (j, i, 0)
    const = lambda i, j: (0, 0)
    flat = lambda i, j: (0, j * (t // tm) + i)
    modi = lambda i, j: (jnp.where(i < n_ctx_tiles, b, j), 0, 0)
    return pl.pallas_call(
        _outproj_kernel, grid=(t // tm, b),
        in_specs=[pl.BlockSpec((1, tm, ATTN_WIDTH), tok), pl.BlockSpec((1, tm, DN_WIDTH), tok),
                  pl.BlockSpec((1, tm, DN_WIDTH), tok), pl.BlockSpec((1, tm, d), tok),
                  pl.BlockSpec((1, 6, d), modi), pl.BlockSpec((1, DN_WIDTH), const),
                  pl.BlockSpec((DN_WIDTH, DN_WIDTH), const), pl.BlockSpec((ATTN_WIDTH, d), const),
                  pl.BlockSpec((DN_WIDTH, d), const), pl.BlockSpec((1, d), const),
                  pl.BlockSpec((ROUTE_LANES, d), const), pl.BlockSpec((ROUTE_LANES, 1), const)],
        out_specs=[pl.BlockSpec((1, tm, d), tok), pl.BlockSpec((1, tm, d), tok),
                   pl.BlockSpec((ROUTE_LANES, tm), flat), pl.BlockSpec((SUBLANES, tm), flat),
                   pl.BlockSpec((1, 1, SUBLANES, ROUTE_LANES), lambda i, j: (j, i, 0, 0))],
        out_shape=[jax.ShapeDtypeStruct((b, t, d), F32), jax.ShapeDtypeStruct((b, t, d), BF16),
                   jax.ShapeDtypeStruct((ROUTE_LANES, b * t), BF16),
                   jax.ShapeDtypeStruct((SUBLANES, b * t), F32),
                   jax.ShapeDtypeStruct((b, t // tm, SUBLANES, ROUTE_LANES), F32)],
        input_output_aliases={3: 0},
        compiler_params=_cparams("parallel", "parallel"),
        name="outproj_router",
    )(a, dsum, gate, xs, mods, dnw, bd, woa, wod, nfw, wrt, br)


def _moe_kernel(cnt_ref, h_ref, r_ref, grp_ref, tri_ref, w1_ref, w3_ref, w2_ref, f_ref, acc_sc, pos_sc,
                route_sc):
    m, g = pl.program_id(0), pl.program_id(1)
    tm = h_ref.shape[0]
    ff = w1_ref.shape[2]

    @pl.when(g == 0)
    def _():
        acc_sc[...] = jnp.zeros(acc_sc.shape, F32)
        route_sc[...] = r_ref[...].astype(F32).T.astype(BF16)
        grp = jnp.broadcast_to(grp_ref[0:1, :], (POS_ROWS * N_GROUPS, tm))
        row_g = lax.broadcasted_iota(jnp.int32, grp.shape, 0) // POS_ROWS
        member = grp == row_g.astype(F32)
        prefix = jnp.dot(member.astype(BF16), tri_ref[...], preferred_element_type=F32)
        pos_sc[...] = jnp.where(member, prefix - 1.0, -1.0)

    pos = pos_sc[pl.ds(pl.multiple_of(g * POS_ROWS, POS_ROWS), 1), :]
    e_row = lax.broadcasted_iota(jnp.int32, (ROUTE_LANES, EXPERTS_PER_GROUP * ff), 0)
    e_col = lax.broadcasted_iota(jnp.int32, (ROUTE_LANES, EXPERTS_PER_GROUP * ff), 1) // ff
    expand = (e_row == g * EXPERTS_PER_GROUP + e_col).astype(BF16)

    def sub_block(first_slot, rows):
        slot = lax.broadcasted_iota(jnp.int32, (rows, tm), 0) + first_slot
        sel = (pos == slot.astype(F32)).astype(BF16)
        xs = jnp.dot(sel, h_ref[...], preferred_element_type=F32).astype(BF16)
        wt = jnp.dot(sel, route_sc[...], preferred_element_type=F32).astype(BF16)
        wexp = jnp.dot(wt, expand, preferred_element_type=F32)
        y = None
        for e in range(EXPERTS_PER_GROUP):
            a = jnp.dot(xs, w1_ref[e], preferred_element_type=F32)
            gate = jnp.dot(xs, w3_ref[e], preferred_element_type=F32)
            mid = (_silu(a) * gate * wexp[:, e * ff:(e + 1) * ff]).astype(BF16)
            ye = jnp.dot(mid, w2_ref[e], preferred_element_type=F32)
            y = ye if y is None else y + ye
        acc_sc[...] += lax.dot_general(sel, y.astype(BF16), (((0,), (0,)), ((), ())),
                                       preferred_element_type=F32)

    count = cnt_ref[m * N_GROUPS + g]
    wide = jnp.logical_and(count > MOE_SUB, count <= MOE_WIDE[-1])
    n_full = jnp.where(wide, 0, count // MOE_SUB)
    rem = jnp.where(wide, 0, count - n_full * MOE_SUB)

    def full_block(sb, carry):
        sub_block(sb * MOE_SUB, MOE_SUB)
        return carry

    lax.fori_loop(0, n_full, full_block, 0)
    lo = 0
    for rows in MOE_TAILS:
        @pl.when(jnp.logical_and(rem > lo, rem <= rows))
        def _(rows=rows):
            sub_block(n_full * MOE_SUB, rows)
        lo = rows
    lo = MOE_SUB
    for rows in MOE_WIDE:
        @pl.when(jnp.logical_and(count > lo, count <= rows))
        def _(rows=rows):
            sub_block(jnp.int32(0), rows)
        lo = rows

    @pl.when(g == N_GROUPS - 1)
    def _():
        f_ref[...] = acc_sc[...].astype(f_ref.dtype)


def _moe(hf, route, grp, counts, tri, w1, w3, w2):
    n, d = hf.shape
    _, _, ff = w1.shape
    tm = tri.shape[0]
    epg = EXPERTS_PER_GROUP
    grid_spec = pltpu.PrefetchScalarGridSpec(
        num_scalar_prefetch=1, grid=(n // tm, N_GROUPS),
        in_specs=[pl.BlockSpec((tm, d), lambda i, g, c: (i, 0)),
                  pl.BlockSpec((ROUTE_LANES, tm), lambda i, g, c: (0, i)),
                  pl.BlockSpec((SUBLANES, tm), lambda i, g, c: (0, i)),
                  pl.BlockSpec((tm, tm), lambda i, g, c: (0, 0)),
                  pl.BlockSpec((epg, d, ff), lambda i, g, c: (g, 0, 0)),
                  pl.BlockSpec((epg, d, ff), lambda i, g, c: (g, 0, 0)),
                  pl.BlockSpec((epg, ff, d), lambda i, g, c: (g, 0, 0))],
        out_specs=pl.BlockSpec((tm, d), lambda i, g, c: (i, 0)),
        scratch_shapes=[pltpu.VMEM((tm, d), F32), pltpu.VMEM((POS_ROWS * N_GROUPS, tm), F32),
                        pltpu.VMEM((tm, ROUTE_LANES), BF16)])
    return pl.pallas_call(
        _moe_kernel, grid_spec=grid_spec,
        out_shape=jax.ShapeDtypeStruct((n, d), BF16),
        compiler_params=_cparams("parallel", "arbitrary"),
        name="moe_experts",
    )(counts, hf, route, grp, tri, w1, w3, w2)


def _final_kernel(x_ref, f_ref, mod_ref, w_ref, o_ref):
    x = x_ref[...] + mod_ref[:, 5:6, :] * f_ref[...].astype(F32)
    o_ref[...] = x * lax.rsqrt(jnp.mean(x * x, axis=-1, keepdims=True) + EPS) * w_ref[...]


def _final_norm(xs, f_prev, mods, w, *, n_ctx_tiles):
    b, t, d = xs.shape
    tm = TOKEN_TILE
    nb = next(c for c in (4, 2, 1) if b % c == 0)
    n_lat = t // tm - n_ctx_tiles
    lat = lambda i, j: (j, i + n_ctx_tiles, 0)
    return pl.pallas_call(
        _final_kernel, grid=(n_lat, b // nb),
        in_specs=[pl.BlockSpec((nb, tm, d), lat), pl.BlockSpec((nb, tm, d), lat),
                  pl.BlockSpec((nb, 6, d), lambda i, j: (j, 0, 0)), pl.BlockSpec((1, d), lambda i, j: (0, 0))],
        out_specs=pl.BlockSpec((nb, tm, d), lambda i, j: (j, i, 0)),
        out_shape=jax.ShapeDtypeStruct((b, n_lat * tm, d), F32),
        compiler_params=_cparams("parallel", "parallel"),
        name="final_norm",
    )(xs, f_prev, mods, w)


def _rope_tables(n_ctx, n_lat):
    pos = jnp.arange(n_lat, dtype=jnp.int32)
    inv = ROPE_THETA ** (-jnp.arange(ROPE_NF, dtype=F32) / ROPE_NF)
    ang_r = (pos // GRID_W).astype(F32)[:, None] * inv
    ang_c = (pos % GRID_W).astype(F32)[:, None] * inv
    cos = jnp.concatenate([jnp.cos(ang_r)] * 2 + [jnp.cos(ang_c)] * 2, axis=-1)
    sin = jnp.concatenate([-jnp.sin(ang_r), jnp.sin(ang_r), -jnp.sin(ang_c), jnp.sin(ang_c)], axis=-1)
    cos = jnp.concatenate([jnp.ones((n_ctx, HEAD_DIM), F32), cos], axis=0)
    sin = jnp.concatenate([jnp.zeros((n_ctx, HEAD_DIM), F32), sin], axis=0)
    reps = QK_WIDTH // HEAD_DIM
    return jnp.tile(cos, (1, reps)), jnp.tile(sin, (1, reps))


def kernel(x, c, ctx, c_ctx, ada_w, ada_b, norm_mix_w, norm_ffn_w, w_in, q_norm_w, k_norm_w, conv_w,
           dn_A_log, dn_dt_bias, dn_norm_w, w_out, rg_w, rg_b, re_w, re_b, w1, w3, w2, final_norm_w):
    b, s, d = x.shape
    n_ctx = ctx.shape[1]
    depth = w_in.shape[0]
    t = n_ctx + s
    assert n_ctx % TOKEN_TILE == 0 and s % TOKEN_TILE == 0 and s % GRID_W == 0
    n_ctx_tiles = n_ctx // TOKEN_TILE

    xs = jnp.concatenate([ctx, x], axis=1)
    mod_rows = -(-(b + BATCH_PER_STEP) // SUBLANES) * SUBLANES
    cs = jnp.zeros((mod_rows, d), F32).at[:b].set(c).at[b:b + BATCH_PER_STEP].set(c_ctx)
    mods = _ada_mods(cs, ada_w, ada_b).reshape(depth, mod_rows, 6, d)

    cos_t, sin_t = _rope_tables(n_ctx, s)
    bd_qk = _block_ones(QK_WIDTH, HEAD_DIM)
    bd_dn = _block_ones(DN_WIDTH, DN_DIM)
    zeros16 = jnp.zeros((2 * DN_HEADS,), F32)
    n_main = w_in.shape[2] - 4 * DN_HEADS
    lane_i = jnp.arange((DN_HEADS // DN_GROUP) * SCALAR_LANES)
    gi, li = lane_i // SCALAR_LANES, lane_i % SCALAR_LANES
    di, ki, hi = li // (2 * DN_GROUP), (li // DN_GROUP) % 2, li % DN_GROUP
    bg_used = li < 4 * DN_GROUP
    bg_src = jnp.where(bg_used, ki * 2 * DN_HEADS + di * DN_HEADS + gi * DN_GROUP + hi, 0)
    moe_tile = next(c for c in MOE_TILES if (b * t) % c == 0)
    tri = (lax.broadcasted_iota(jnp.int32, (moe_tile, moe_tile), 0)
           <= lax.broadcasted_iota(jnp.int32, (moe_tile, moe_tile), 1)).astype(BF16)

    f_prev = None
    for l in range(depth):
        qkw = jnp.concatenate([jnp.tile(q_norm_w[l], ATTN_HEADS), jnp.tile(k_norm_w[l], KV_HEADS)])[None]
        dnp = jnp.stack([jnp.concatenate([zeros16, dn_A_log[l].reshape(-1)]),
                         jnp.concatenate([zeros16, dn_dt_bias[l].reshape(-1)])])
        dnp = jnp.where(bg_used, jnp.take(dnp, bg_src, axis=1), 0.0)
        w_tail = jnp.where(bg_used, jnp.take(w_in[l][:, n_main:], bg_src, axis=1), 0.0)
        w_in_l = jnp.concatenate([w_in[l][:, :n_main], w_tail], axis=1).astype(BF16)
        outs = _inproj(xs, f_prev, mods[l - 1] if l else None, mods[l], norm_mix_w[l][None],
                       w_in_l, qkw, cos_t, sin_t, bd_qk, dnp, n_ctx_tiles=n_ctx_tiles)
        qt, k, vt, dqkv, gate, bg = outs[:6]
        if l:
            xs = outs[6]
        a = _attention(qt, k, vt, n_ctx=n_ctx)
        dsum = _deltanet(dqkv, conv_w[l], bg, n_ctx=n_ctx)
        wo = w_out[l].astype(BF16)
        wrt = jnp.zeros((ROUTE_LANES, d), F32).at[:N_EXPERTS].set(re_w[l].T).at[
            N_EXPERTS:N_EXPERTS + N_GROUPS].set(rg_w[l].T).astype(BF16)
        br = jnp.zeros((ROUTE_LANES, 1), F32).at[:N_EXPERTS, 0].set(re_b[l]).at[
            N_EXPERTS:N_EXPERTS + N_GROUPS, 0].set(rg_b[l])
        xs, hf, route, grp, cnt = _outproj(
            a, dsum, gate, xs, mods[l], jnp.tile(dn_norm_w[l], DN_HEADS)[None], bd_dn,
            wo[:ATTN_WIDTH], wo[ATTN_WIDTH:], norm_ffn_w[l][None], wrt, br, n_ctx_tiles=n_ctx_tiles)
        counts = cnt[:, :, :N_GROUPS, 0].reshape(-1, moe_tile // TOKEN_TILE, N_GROUPS).sum(axis=1)
        f_prev = _moe(hf.reshape(b * t, d), route, grp,
                      counts.astype(jnp.int32).reshape(-1), tri,
                      w1[l].astype(BF16), w3[l].astype(BF16), w2[l].astype(BF16)).reshape(b, t, d)
    return _final_norm(xs, f_prev, mods[depth - 1], final_norm_w[None], n_ctx_tiles=n_ctx_tiles)
```

```python
import functools
import math

import jax
import jax.numpy as jnp
from jax import lax
from jax.experimental import pallas as pl
from jax.experimental.pallas import tpu as pltpu

F32 = jnp.float32
BF16 = jnp.bfloat16
HIGHEST = lax.Precision.HIGHEST

GRID_W = 64
HEAD_DIM = 64
ATTN_HEADS = 8
KV_HEADS = 2
GQA_GROUP = ATTN_HEADS // KV_HEADS
ATTN_WIDTH = ATTN_HEADS * HEAD_DIM
KV_WIDTH = KV_HEADS * HEAD_DIM
ATTN_SCALE = HEAD_DIM ** -0.5
LOG2_E = math.log2(math.e)
ROPE_THETA = 10000.0
ROPE_NF = HEAD_DIM // 4
DN_HEADS = 8
DN_DIM = 64
DN_WIDTH = DN_HEADS * DN_DIM
DN_SCALE = DN_DIM ** -0.5
DN_GROUP = 4
CONV_K = 5
CHUNK = 64
N_GROUPS = 4
EXPERTS_PER_GROUP = 8
N_EXPERTS = N_GROUPS * EXPERTS_PER_GROUP
EPS = 1e-6
QK_WIDTH = ATTN_WIDTH + KV_WIDTH
SUBLANES = 8
TOKEN_TILE = 256
BATCH_PER_STEP = 2
ATTN_KEY_TILE = 256
ONES_ROWS = 16
HEADS_PER_PASS = 8
HALO = 16
ROUTE_LANES = 128
MOE_TILES = (1024, 512, 256)
MOE_SUB = 256
MOE_TAILS = (64, 128, 256)
MOE_WIDE = (320, 384)
POS_ROWS = 16
VMEM_LIMIT = 56 * 1024 * 1024


def _cparams(*sem):
    return pltpu.CompilerParams(dimension_semantics=sem, vmem_limit_bytes=VMEM_LIMIT)


def _silu(x):
    return x * jax.nn.sigmoid(x)


def _block_ones(n, blk):
    i = lax.broadcasted_iota(jnp.int32, (n, n), 0) // blk
    j = lax.broadcasted_iota(jnp.int32, (n, n), 1) // blk
    return (i == j).astype(BF16)


def _group_mean_sq(x, ones_bd, width):
    return jnp.dot((x * x).astype(BF16), ones_bd, preferred_element_type=F32) * (1.0 / width)


def _ada_kernel(cs_ref, w_ref, b_ref, o_ref):
    o_ref[0] = jnp.dot(_silu(cs_ref[...]), w_ref[0], preferred_element_type=F32,
                       precision=HIGHEST) + b_ref[0]


def _ada_mods(cs, ada_w, ada_b):
    depth, d, n6 = ada_w.shape
    rows = cs.shape[0]
    tn = 1536
    return pl.pallas_call(
        _ada_kernel,
        grid=(depth, n6 // tn),
        in_specs=[pl.BlockSpec((rows, d), lambda l, j: (0, 0)),
                  pl.BlockSpec((1, d, tn), lambda l, j: (l, 0, j)),
                  pl.BlockSpec((1, 1, tn), lambda l, j: (l, 0, j))],
        out_specs=pl.BlockSpec((1, rows, tn), lambda l, j: (l, 0, j)),
        out_shape=jax.ShapeDtypeStruct((depth, rows, n6), F32),
        compiler_params=_cparams("parallel", "parallel"),
        name="ada_mods",
    )(cs, ada_w, ada_b.reshape(depth, 1, n6))


def _inproj_kernel(*refs, has_prev):
    if has_prev:
        (x_ref, f_ref, modp_ref, mod_ref, nw_ref, win_ref, qkw_ref, cos_ref, sin_ref, bd_ref, dnp_ref,
         qt_ref, k_ref, vt_ref, dqkv_ref, gate_ref, bg_ref, xo_ref) = refs
    else:
        (x_ref, mod_ref, nw_ref, win_ref, qkw_ref, cos_ref, sin_ref, bd_ref, dnp_ref,
         qt_ref, k_ref, vt_ref, dqkv_ref, gate_ref, bg_ref) = refs
    nb, tm = x_ref.shape[0], x_ref.shape[1]
    hs = []
    for r in range(nb):
        x = x_ref[r]
        if has_prev:
            x = x + modp_ref[r, 5:6, :] * f_ref[r].astype(F32)
            xo_ref[r] = x
        h = x * lax.rsqrt(jnp.mean(x * x, axis=-1, keepdims=True) + EPS) * nw_ref[...]
        hs.append((h * (1.0 + mod_ref[r, 1:2, :]) + mod_ref[r, 0:1, :]).astype(BF16))
    acc = jnp.dot(jnp.concatenate(hs, axis=0), win_ref[...], preferred_element_type=F32)
    qk = acc[:, :QK_WIDTH]
    qn = qk * lax.rsqrt(_group_mean_sq(qk, bd_ref[...], HEAD_DIM) + EPS) * qkw_ref[...]
    lane = lax.broadcasted_iota(jnp.int32, qn.shape, 1)
    partner = jnp.where(lane % (2 * ROPE_NF) < ROPE_NF,
                        pltpu.roll(qn, QK_WIDTH - ROPE_NF, 1), pltpu.roll(qn, ROPE_NF, 1))
    qr = (qn * jnp.concatenate([cos_ref[...]] * nb, axis=0)
          + partner * jnp.concatenate([sin_ref[...]] * nb, axis=0))
    c_v = QK_WIDTH
    c_dn = c_v + KV_WIDTH
    c_gate = c_dn + 3 * DN_WIDTH
    c_bg = c_gate + DN_WIDTH
    z = acc[:, c_bg:]
    zb = z + dnp_ref[1:2, :]
    softplus = jnp.maximum(zb, 0.0) + jnp.log1p(jnp.exp(-jnp.abs(zb)))
    lane_z = lax.broadcasted_iota(jnp.int32, z.shape, 1)
    bg = jnp.where(lane_z % (2 * DN_GROUP) < DN_GROUP, jax.nn.sigmoid(z), -jnp.exp(dnp_ref[0:1, :]) * softplus)
    for r in range(nb):
        rs = slice(r * tm, (r + 1) * tm)
        qt_ref[r] = (qr[rs, :ATTN_WIDTH] * (ATTN_SCALE * LOG2_E)).T.astype(BF16)
        k_ref[r] = qr[rs, ATTN_WIDTH:].astype(BF16)
        vt_ref[r] = acc[rs, c_v:c_dn].T.astype(BF16)
        dqkv_ref[r] = acc[rs, c_dn:c_gate].astype(BF16)
        gate_ref[r] = acc[rs, c_gate:c_bg].astype(BF16)
        for gi in range(DN_HEADS // DN_GROUP):
            bg_ref[r, gi] = bg[rs, gi * SCALAR_LANES:(gi + 1) * SCALAR_LANES]


def _inproj(xs, f_prev, mods_prev, mods, nw, win, qkw, cos_t, sin_t, bd, dnp, *, n_ctx_tiles):
    b, t, d = xs.shape
    tm = TOKEN_TILE
    nb = BATCH_PER_STEP
    assert b % nb == 0
    nt = t // tm
    proj = win.shape[1]
    has_prev = f_prev is not None
    tok = lambda i, j: (j, i, 0)
    modi = lambda i, j: (jnp.where(i < n_ctx_tiles, b // nb, j), 0, 0)
    const = lambda i, j: (0, 0)
    in_specs = [pl.BlockSpec((nb, tm, d), tok)]
    args = [xs]
    if has_prev:
        in_specs += [pl.BlockSpec((nb, tm, d), tok), pl.BlockSpec((nb, 6, d), modi)]
        args += [f_prev, mods_prev]
    in_specs += [pl.BlockSpec((nb, 6, d), modi), pl.BlockSpec((1, d), const),
                 pl.BlockSpec((d, proj), const), pl.BlockSpec((1, QK_WIDTH), const),
                 pl.BlockSpec((tm, QK_WIDTH), lambda i, j: (i, 0)),
                 pl.BlockSpec((tm, QK_WIDTH), lambda i, j: (i, 0)),
                 pl.BlockSpec((QK_WIDTH, QK_WIDTH), const), pl.BlockSpec((2, dnp.shape[1]), const)]
    args += [mods, nw, win, qkw, cos_t, sin_t, bd, dnp]
    seq_out = lambda w, dt: (pl.BlockSpec((nb, tm, w), tok), jax.ShapeDtypeStruct((b, t, w), dt))
    tr_out = lambda w: (pl.BlockSpec((nb, w, tm), lambda i, j: (j, 0, i)), jax.ShapeDtypeStruct((b, w, t), BF16))
    ng = DN_HEADS // DN_GROUP
    outs = [tr_out(ATTN_WIDTH), seq_out(KV_WIDTH, BF16), tr_out(KV_WIDTH),
            seq_out(3 * DN_WIDTH, BF16), seq_out(DN_WIDTH, BF16),
            (pl.BlockSpec((nb, ng, tm, SCALAR_LANES), lambda i, j: (j, 0, i, 0)),
             jax.ShapeDtypeStruct((b, ng, t, SCALAR_LANES), F32))]
    out_specs = [o[0] for o in outs]
    out_shape = [o[1] for o in outs]
    aliases = {}
    if has_prev:
        out_specs.append(pl.BlockSpec((nb, tm, d), tok))
        out_shape.append(jax.ShapeDtypeStruct((b, t, d), F32))
        aliases = {0: len(out_shape) - 1}
    return pl.pallas_call(
        functools.partial(_inproj_kernel, has_prev=has_prev),
        grid=(nt, b // nb), in_specs=in_specs, out_specs=out_specs, out_shape=out_shape,
        input_output_aliases=aliases,
        compiler_params=_cparams("parallel", "parallel"),
        name="inproj",
    )(*args)


def _attn_kernel(qt_ref, k_ref, vt_ref, o_ref, acc_sc, sa_sc, sb_sc, *, tk, n_ctx_q, n_ctx_k, n_all_k):
    tq = qt_ref.shape[2]
    n_kv = jnp.where(pl.program_id(1) < n_ctx_q, n_ctx_k, n_all_k)
    nh = HEADS_PER_PASS
    local = range(nh)
    last = n_kv - 1

    def tile_rows(i):
        return pl.ds(pl.multiple_of(i * tk, tk), tk)

    for h0 in range(0, ATTN_HEADS, nh):
        kv = [(h0 + j) // GQA_GROUP for j in local]
        acc_sc[...] = jnp.zeros(acc_sc.shape, F32)

        def scores(i, s_sc, h0=h0, kv=kv):
            rows = tile_rows(i)
            for j in local:
                s_sc[j] = jnp.dot(k_ref[0, rows, kv[j] * HEAD_DIM:(kv[j] + 1) * HEAD_DIM],
                                  qt_ref[0, (h0 + j) * HEAD_DIM:(h0 + j + 1) * HEAD_DIM, :],
                                  preferred_element_type=F32)

        def softmax_pv(i, s_sc, carry, kv=kv):
            rows = tile_rows(i)
            m_prev, l_prev = carry[:nh], carry[nh:]
            s = [s_sc[j] for j in local]
            m_new = [jnp.maximum(m_prev[j], jnp.max(s[j], axis=0, keepdims=True)) for j in local]
            p = [jnp.exp2(s[j] - m_new[j]) for j in local]
            alpha = [jnp.exp2(m_prev[j] - m_new[j]) for j in local]
            ones = jnp.ones((ONES_ROWS, tk), BF16)
            vt1 = {g: jnp.concatenate([vt_ref[0, g * HEAD_DIM:(g + 1) * HEAD_DIM, rows], ones], axis=0)
                   for g in sorted(set(kv))}
            pv = [jnp.dot(vt1[kv[j]], p[j].astype(BF16), preferred_element_type=F32) for j in local]
            for j in local:
                acc_sc[j] = alpha[j] * acc_sc[j] + pv[j][:HEAD_DIM]
            l_new = [alpha[j] * l_prev[j] + pv[j][HEAD_DIM:HEAD_DIM + 1] for j in local]
            return tuple(m_new) + tuple(l_new)

        init = (jnp.full((1, tq), -jnp.inf, F32),) * nh + (jnp.zeros((1, tq), F32),) * nh
        scores(0, sa_sc)
        scores(jnp.minimum(1, last), sb_sc)
        stats = softmax_pv(0, sa_sc, init)

        def pair(j, carry, scores=scores, softmax_pv=softmax_pv):
            a = 2 * j + 1
            scores(a + 1, sa_sc)
            carry = softmax_pv(a, sb_sc, carry)
            scores(jnp.minimum(a + 2, last), sb_sc)
            return softmax_pv(a + 1, sa_sc, carry)

        stats = lax.fori_loop(0, last // 2, pair, stats)
        og = jnp.concatenate([acc_sc[j] / stats[nh + j] for j in local], axis=0)
        o_ref[0, :, h0 * HEAD_DIM:(h0 + nh) * HEAD_DIM] = og.T.astype(BF16)


def _attention(qt, k, vt, *, n_ctx):
    b, _, t = qt.shape
    tq, tk = TOKEN_TILE, ATTN_KEY_TILE
    assert (n_ctx // tk) % 2 == 1 and (t // tk) % 2 == 1
    kern = functools.partial(_attn_kernel, tk=tk, n_ctx_q=n_ctx // tq, n_ctx_k=n_ctx // tk, n_all_k=t // tk)
    s_buf = pltpu.VMEM((HEADS_PER_PASS, tk, tq), F32)
    return pl.pallas_call(
        kern, grid=(b, t // tq),
        in_specs=[pl.BlockSpec((1, ATTN_WIDTH, tq), lambda bi, i: (bi, 0, i)),
                  pl.BlockSpec((1, t, KV_WIDTH), lambda bi, i: (bi, 0, 0)),
                  pl.BlockSpec((1, KV_WIDTH, t), lambda bi, i: (bi, 0, 0))],
        out_specs=pl.BlockSpec((1, tq, ATTN_WIDTH), lambda bi, i: (bi, i, 0)),
        out_shape=jax.ShapeDtypeStruct((b, t, ATTN_WIDTH), BF16),
        scratch_shapes=[pltpu.VMEM((HEADS_PER_PASS, HEAD_DIM, tq), F32), s_buf, s_buf],
        compiler_params=_cparams("parallel", "parallel"),
        name="gqa_attention",
    )(qt, k, vt)


DN_LANES = DN_GROUP * DN_DIM
DN_BLOCK = 4
INV_BLOCK = 16
SCALAR_LANES = 128


def _split3(x):
    x1 = x.astype(BF16)
    r1 = x - x1.astype(F32)
    x2 = r1.astype(BF16)
    x3 = (r1 - x2.astype(F32)).astype(BF16)
    return jnp.concatenate([x1, x2, x3], axis=1)


def _sum3(y):
    w = y.shape[1] // 3
    return y[:, :w] + y[:, w:2 * w] + y[:, 2 * w:]


def _dn_kernel(dq_ref, dk_ref, dv_ref, wq_ref, wk_ref, wv_ref, bg_ref, o_ref, mp_sc, n_sc, r_sc, gam_sc, *,
               n_chunks, n_ctx_chunks):
    c = CHUNK
    rows_b = DN_BLOCK * c
    ln = DN_LANES
    ri = lax.broadcasted_iota(jnp.int32, (rows_b, ln), 0)
    li = lax.broadcasted_iota(jnp.int32, (rows_b, ln), 1)
    same = (ri // c) == (li // c)
    i_in, j_in = ri % c, li % c
    eye_rc = i_in == j_in
    near = (i_in // INV_BLOCK) == (j_in // INV_BLOCK)
    e_row = lax.broadcasted_iota(jnp.int32, (3 * SCALAR_LANES, 2 * ln), 0) % SCALAR_LANES
    e_blk = lax.broadcasted_iota(jnp.int32, (3 * SCALAR_LANES, 2 * ln), 1) // c
    zl = lax.broadcasted_iota(jnp.int32, (rows_b, SCALAR_LANES), 1)
    nt_dims = (((1,), (1,)), ((), ()))
    tn_dims = (((0,), (0,)), ((), ()))

    def bdiag(x):
        return jnp.where(same, jnp.concatenate([x] * DN_GROUP, axis=0), jnp.zeros((), x.dtype))

    def fold(x):
        x = jnp.where(same, x, 0.0)
        return x[0:c] + x[c:2 * c] + x[2 * c:3 * c] + x[3 * c:4 * c]

    dirs = range(2)
    incl = [i_in >= j_in, i_in <= j_in]
    strict = [i_in > j_in, i_in < j_in]
    tri_bd = [jnp.logical_and(same, incl[d]).astype(BF16) for d in dirs]
    sel_last = [jnp.logical_and(same, j_in == (c - 1, 0)[d]).astype(BF16) for d in dirs]
    same_b = same.astype(BF16)
    col0 = [d * 2 * DN_GROUP for d in dirs]
    expand = [(e_row == e_blk + col0[d]).astype(BF16) for d in dirs]
    is_beta = [jnp.logical_and(zl >= col0[d], zl < col0[d] + DN_GROUP) for d in dirs]
    eye_f = eye_rc.astype(F32)
    chunks = range(DN_BLOCK)
    sls = [slice(ch * c, (ch + 1) * c) for ch in chunks]
    mm = lambda a, bmat: jnp.dot(a.astype(BF16), bmat, preferred_element_type=F32)

    def prepare(blk, carry):
        rs = pl.ds(pl.multiple_of(blk * rows_b, rows_b), rows_b)
        z = bg_ref[0, 0, rs, :]
        r0 = blk * rows_b
        keep_prev = jnp.where(jnp.logical_or(blk == 0, blk == n_ctx_chunks // DN_BLOCK), 0.0, 1.0)
        keep_next = jnp.where(jnp.logical_or(blk == n_ctx_chunks // DN_BLOCK - 1,
                                             blk == n_chunks // DN_BLOCK - 1), 0.0, 1.0)
        prev_rows = pl.ds(pl.multiple_of(jnp.maximum(r0 - HALO, 0), HALO), HALO)
        next_rows = pl.ds(pl.multiple_of(jnp.minimum(r0 + rows_b, n_chunks * c - HALO), HALO), HALO)

        def conv_silu(x_ref, w_ref):
            ext = jnp.concatenate([x_ref[0, prev_rows, :].astype(F32)[HALO - SUBLANES:] * keep_prev,
                                   x_ref[0, rs, :].astype(F32),
                                   x_ref[0, next_rows, :].astype(F32)[:SUBLANES] * keep_next], axis=0)
            y = None
            for j in range(CONV_K):
                shift = (CONV_K // 2 - j) % (rows_b + 2 * SUBLANES)
                tap = ext if shift == 0 else pltpu.roll(ext, shift, 0)
                term = tap[SUBLANES:SUBLANES + rows_b] * w_ref[j:j + 1, :]
                y = term if y is None else y + term
            return _silu(y)

        def l2n(x):
            return x * lax.rsqrt(jnp.dot((x * x).astype(BF16), same_b, preferred_element_type=F32) + EPS)

        q = l2n(conv_silu(dq_ref, wq_ref)) * DN_SCALE
        k = l2n(conv_silu(dk_ref, wk_ref))
        v = conv_silu(dv_ref, wv_ref)
        qbf = q.astype(BF16)
        kbf = k.astype(BF16)
        sc = [lax.dot_general(jnp.concatenate([kbf[sl], qbf[sl]], axis=0), bdiag(kbf[sl]), nt_dims,
                              preferred_element_type=F32) for sl in sls]
        lfull, intra, vb, kbe, q_dec, k_dec, gamma = [], [], [], [], [], [], []
        for d in dirs:
            cum = _sum3(jnp.dot(tri_bd[d], _split3(z), preferred_element_type=F32))
            ex = jnp.dot(_split3(jnp.where(is_beta[d], z, cum)), expand[d], preferred_element_type=F32)
            beta, gce = ex[:, :ln], ex[:, ln:]
            g_row = _sum3(jnp.dot(same_b, _split3(jnp.where(eye_rc, gce, 0.0)), preferred_element_type=F32))
            g_last = _sum3(jnp.dot(sel_last[d], _split3(gce), preferred_element_type=F32))
            decay = jnp.where(incl[d], jnp.exp(jnp.where(incl[d], gce - g_row, 0.0)), 0.0)
            bdecay = jnp.where(strict[d], beta * decay, 0.0)
            e_g = jnp.exp(gce)
            kb = k * beta
            kbe_d = (kb * e_g).astype(BF16)
            vb_d = (v * beta).astype(BF16)
            q_dec_d = q * e_g
            k_dec_d = (k * jnp.exp(g_last - gce)).astype(BF16)
            gamma_d = jnp.exp(g_last)
            lfull += [s[:c] * bdecay[sl] for s, sl in zip(sc, sls)]
            intra += [(s[c:] * decay[sl]).astype(BF16) for s, sl in zip(sc, sls)]
            vb += [vb_d[sl] for sl in sls]
            kbe += [kbe_d[sl] for sl in sls]
            q_dec += [q_dec_d[sl] for sl in sls]
            k_dec += [k_dec_d[sl] for sl in sls]
            gamma += [gamma_d[ch * c:ch * c + SUBLANES] for ch in chunks]
        near_c = near[sls[0]]
        p = [jnp.where(near_c, -lf, 0.0) for lf in lfull]
        l_off = [jnp.where(near_c, 0.0, lf).astype(BF16) for lf in lfull]
        tmat = [eye_f[sls[0]] + pi for pi in p]
        p = [mm(pi, bdiag(pi.astype(BF16))) for pi in p]
        for _ in range(int(math.log2(INV_BLOCK)) - 2):
            res = [mm(jnp.concatenate([ti, pi], axis=0), bdiag(pi.astype(BF16))) for ti, pi in zip(tmat, p)]
            tmat = [ti + ri[:c] for ti, ri in zip(tmat, res)]
            p = [ri[c:] for ri in res]
        tmat = [ti + mm(ti, bdiag(pi.astype(BF16))) for ti, pi in zip(tmat, p)]
        nmat = [mm(ti, bdiag(lo)) for ti, lo in zip(tmat, l_off)]
        n2 = [mm(ni, bdiag(ni.astype(BF16))) for ni in nmat]
        tmat = [ti + mm(qi, bdiag(ti.astype(BF16))) for ti, qi in zip(tmat, n2)]
        tmat = [ti - mm(ni, bdiag(ti.astype(BF16))) for ti, ni in zip(tmat, nmat)]
        uw = [mm(ti, jnp.concatenate([bdiag(vi), bdiag(ki)], axis=1)).astype(BF16)
              for ti, vi, ki in zip(tmat, vb, kbe)]
        aw_au = [mm(ai, jnp.concatenate([bdiag(x[:, ln:]), bdiag(x[:, :ln])], axis=1))
                 for ai, x in zip(intra, uw)]
        mn = [lax.dot_general(kd, jnp.concatenate([x[:, ln:], x[:, :ln]], axis=1), tn_dims,
                              preferred_element_type=F32) for kd, x in zip(k_dec, uw)]
        for d in dirs:
            for ch in chunks:
                st = d * DN_BLOCK + ch
                ci = blk * DN_BLOCK + ch
                p_c = q_dec[st] - aw_au[st][:, :ln]
                mp_sc[d, ci] = jnp.concatenate([fold(mn[st][:, :ln]), p_c], axis=0).astype(BF16)
                n_sc[d, ci] = fold(mn[st][:, ln:])
                r_sc[d, ci] = aw_au[st][:, ln:].astype(BF16)
                gam_sc[d, ci] = gamma[st]
        return carry

    lax.fori_loop(0, n_chunks // DN_BLOCK, prepare, 0)

    o_ref[...] = jnp.zeros(o_ref.shape, F32)

    def scan(s, states):
        ci = (s, jnp.where(s < n_ctx_chunks, n_ctx_chunks - 1 - s, n_chunks - 1 + n_ctx_chunks - s))
        res = [jnp.dot(mp_sc[d, ci[d]], bdiag(states[d].astype(BF16)), preferred_element_type=F32)
               for d in dirs]
        for d in dirs:
            rows = pl.ds(pl.multiple_of(ci[d] * c, c), c)
            o_ref[0, rows, :] += res[d][c:] + r_sc[d, ci[d]].astype(F32)
        return tuple(gam_sc[d, ci[d]][0:1] * states[d] - res[d][:c] + n_sc[d, ci[d]] for d in dirs)

    zero = jnp.zeros((c, ln), F32)
    lax.fori_loop(0, n_chunks, scan, (zero, zero), unroll=DN_BLOCK)


def _deltanet(dqkv, conv_w, bg, *, n_ctx):
    b, t, _ = dqkv.shape
    n_chunks = t // CHUNK
    block_rows = DN_BLOCK * CHUNK
    assert t % block_rows == 0 and n_ctx % block_rows == 0 and block_rows == DN_LANES == DN_GROUP * DN_DIM
    ng = DN_HEADS // DN_GROUP
    once = dict(pipeline_mode=pl.Buffered(1))
    part = lambda s: pl.BlockSpec((1, t, DN_LANES), lambda bi, gi, s=s: (bi, 0, s * ng + gi))
    wpart = lambda s: pl.BlockSpec((CONV_K, DN_LANES), lambda bi, gi, s=s: (0, s * ng + gi))
    kern = functools.partial(_dn_kernel, n_chunks=n_chunks, n_ctx_chunks=n_ctx // CHUNK)
    return pl.pallas_call(
        kern, grid=(b, ng),
        in_specs=[part(0), part(1), part(2), wpart(0), wpart(1), wpart(2),
                  pl.BlockSpec((1, 1, t, SCALAR_LANES), lambda bi, gi: (bi, gi, 0, 0), **once)],
        out_specs=pl.BlockSpec((1, t, DN_LANES), lambda bi, gi: (bi, 0, gi)),
        out_shape=jax.ShapeDtypeStruct((b, t, DN_WIDTH), F32),
        scratch_shapes=[pltpu.VMEM((2, n_chunks, 2 * CHUNK, DN_LANES), BF16),
                        pltpu.VMEM((2, n_chunks, CHUNK, DN_LANES), F32),
                        pltpu.VMEM((2, n_chunks, CHUNK, DN_LANES), BF16),
                        pltpu.VMEM((2, n_chunks, SUBLANES, DN_LANES), F32)],
        compiler_params=_cparams("parallel", "parallel"),
        name="gated_deltanet",
    )(dqkv, dqkv, dqkv, conv_w, conv_w, conv_w, bg)


def _outproj_kernel(a_ref, d_ref, gate_ref, x_ref, mod_ref, dnw_ref, bd_ref, woa_ref, wod_ref, nfw_ref,
                    wrt_ref, br_ref, xo_ref, hf_ref, route_ref, grp_ref, cnt_ref):
    dd = d_ref[0]
    gate = gate_ref[0].astype(F32)
    dn = dd * lax.rsqrt(_group_mean_sq(dd, bd_ref[...], DN_DIM) + EPS) * dnw_ref[...] * _silu(gate)
    y = (jnp.dot(a_ref[0], woa_ref[...], preferred_element_type=F32)
         + jnp.dot(dn.astype(BF16), wod_ref[...], preferred_element_type=F32))
    x = x_ref[0] + mod_ref[0, 2:3, :] * y
    xo_ref[0] = x
    h = x * lax.rsqrt(jnp.mean(x * x, axis=-1, keepdims=True) + EPS) * nfw_ref[...]
    h = h * (1.0 + mod_ref[0, 4:5, :]) + mod_ref[0, 3:4, :]
    hb = h.astype(BF16)
    hf_ref[0] = hb
    logits = lax.dot_general(wrt_ref[...], hb, (((1,), (1,)), ((), ())), preferred_element_type=F32) + br_ref[...]
    row = lax.broadcasted_iota(jnp.int32, logits.shape, 0)
    neg = jnp.float32(-jnp.inf)
    big = jnp.int32(ROUTE_LANES)

    def first_argmax(vals, vmax):
        return jnp.min(jnp.where(vals == vmax, row, big), axis=0, keepdims=True)

    is_g = jnp.logical_and(row >= N_EXPERTS, row < N_EXPERTS + N_GROUPS)
    gl = jnp.where(is_g, logits, neg)
    g_max = jnp.max(gl, axis=0, keepdims=True)
    g_sel = first_argmax(gl, g_max) - N_EXPERTS
    p_g = 1.0 / jnp.sum(jnp.exp(gl - g_max), axis=0, keepdims=True)
    e_lo = g_sel * EXPERTS_PER_GROUP
    in_grp = jnp.logical_and(row >= e_lo, row < e_lo + EXPERTS_PER_GROUP)
    el = jnp.where(in_grp, logits, neg)
    e_max = jnp.max(el, axis=0, keepdims=True)
    i1 = first_argmax(el, e_max)
    el2 = jnp.where(row == i1, neg, el)
    e_max2 = jnp.max(el2, axis=0, keepdims=True)
    i2 = first_argmax(el2, e_max2)
    p2 = jnp.exp(e_max2 - e_max)
    w1 = p_g / (1.0 + p2)
    w2 = p_g * p2 / (1.0 + p2)
    route_ref[...] = jnp.where(row == i1, w1, jnp.where(row == i2, w2, 0.0)).astype(BF16)
    grp_ref[...] = jnp.broadcast_to(g_sel.astype(F32), grp_ref.shape)
    in_g = jnp.where(lax.broadcasted_iota(jnp.int32, grp_ref.shape, 0) == g_sel, 1.0, 0.0)
    cnt_ref[0, 0] = jnp.broadcast_to(jnp.sum(in_g, axis=1, keepdims=True), (SUBLANES, ROUTE_LANES))


def _outproj(a, dsum, gate, xs, mods, dnw, bd, woa, wod, nfw, wrt, br, *, n_ctx_tiles):
    b, t, d = xs.shape
    tm = TOKEN_TILE
    tok = lambda i, j: (j, i, 0)
    const = lambda i, j: (0, 0)
    flat = lambda i, j: (0, j * (t // tm) + i)
    modi = lambda i, j: (jnp.where(i < n_ctx_tiles, b, j), 0, 0)
    return pl.pallas_call(
        _outproj_kernel, grid=(t // tm, b),
        in_specs=[pl.BlockSpec((1, tm, ATTN_WIDTH), tok), pl.BlockSpec((1, tm, DN_WIDTH), tok),
                  pl.BlockSpec((1, tm, DN_WIDTH), tok), pl.BlockSpec((1, tm, d), tok),
                  pl.BlockSpec((1, 6, d), modi), pl.BlockSpec((1, DN_WIDTH), const),
                  pl.BlockSpec((DN_WIDTH, DN_WIDTH), const), pl.BlockSpec((ATTN_WIDTH, d), const),
                  pl.BlockSpec((DN_WIDTH, d), const), pl.BlockSpec((1, d), const),
                  pl.BlockSpec((ROUTE_LANES, d), const), pl.BlockSpec((ROUTE_LANES, 1), const)],
        out_specs=[pl.BlockSpec((1, tm, d), tok), pl.BlockSpec((1, tm, d), tok),
                   pl.BlockSpec((ROUTE_LANES, tm), flat), pl.BlockSpec((SUBLANES, tm), flat),
                   pl.BlockSpec((1, 1, SUBLANES, ROUTE_LANES), lambda i, j: (j, i, 0, 0))],
        out_shape=[jax.ShapeDtypeStruct((b, t, d), F32), jax.ShapeDtypeStruct((b, t, d), BF16),
                   jax.ShapeDtypeStruct((ROUTE_LANES, b * t), BF16),
                   jax.ShapeDtypeStruct((SUBLANES, b * t), F32),
                   jax.ShapeDtypeStruct((b, t // tm, SUBLANES, ROUTE_LANES), F32)],
        input_output_aliases={3: 0},
        compiler_params=_cparams("parallel", "parallel"),
        name="outproj_router",
    )(a, dsum, gate, xs, mods, dnw, bd, woa, wod, nfw, wrt, br)


def _moe_kernel(cnt_ref, h_ref, r_ref, grp_ref, tri_ref, w1_ref, w3_ref, w2_ref, f_ref, acc_sc, pos_sc,
                route_sc):
    m, g = pl.program_id(0), pl.program_id(1)
    tm = h_ref.shape[0]
    ff = w1_ref.shape[2]

    @pl.when(g == 0)
    def _():
        acc_sc[...] = jnp.zeros(acc_sc.shape, F32)
        route_sc[...] = r_ref[...].astype(F32).T.astype(BF16)
        grp = jnp.broadcast_to(grp_ref[0:1, :], (POS_ROWS * N_GROUPS, tm))
        row_g = lax.broadcasted_iota(jnp.int32, grp.shape, 0) // POS_ROWS
        member = grp == row_g.astype(F32)
        prefix = jnp.dot(member.astype(BF16), tri_ref[...], preferred_element_type=F32)
        pos_sc[...] = jnp.where(member, prefix - 1.0, -1.0)

    pos = pos_sc[pl.ds(pl.multiple_of(g * POS_ROWS, POS_ROWS), 1), :]
    e_row = lax.broadcasted_iota(jnp.int32, (ROUTE_LANES, EXPERTS_PER_GROUP * ff), 0)
    e_col = lax.broadcasted_iota(jnp.int32, (ROUTE_LANES, EXPERTS_PER_GROUP * ff), 1) // ff
    expand = (e_row == g * EXPERTS_PER_GROUP + e_col).astype(BF16)

    def sub_block(first_slot, rows):
        slot = lax.broadcasted_iota(jnp.int32, (rows, tm), 0) + first_slot
        sel = (pos == slot.astype(F32)).astype(BF16)
        xs = jnp.dot(sel, h_ref[...], preferred_element_type=F32).astype(BF16)
        wt = jnp.dot(sel, route_sc[...], preferred_element_type=F32).astype(BF16)
        wexp = jnp.dot(wt, expand, preferred_element_type=F32)
        experts = range(EXPERTS_PER_GROUP)
        a = [jnp.dot(xs, w1_ref[e], preferred_element_type=F32) for e in experts]
        gate = [jnp.dot(xs, w3_ref[e], preferred_element_type=F32) for e in experts]
        mid = [(_silu(a[e]) * gate[e] * wexp[:, e * ff:(e + 1) * ff]).astype(BF16) for e in experts]
        y = sum(jnp.dot(mid[e], w2_ref[e], preferred_element_type=F32) for e in experts)
        acc_sc[...] += lax.dot_general(sel, y.astype(BF16), (((0,), (0,)), ((), ())),
                                       preferred_element_type=F32)

    count = cnt_ref[m * N_GROUPS + g]
    wide = jnp.logical_and(count > MOE_SUB, count <= MOE_WIDE[-1])
    n_full = jnp.where(wide, 0, count // MOE_SUB)
    rem = jnp.where(wide, 0, count - n_full * MOE_SUB)

    def full_block(sb, carry):
        sub_block(sb * MOE_SUB, MOE_SUB)
        return carry

    lax.fori_loop(0, n_full, full_block, 0)
    lo = 0
    for rows in MOE_TAILS:
        @pl.when(jnp.logical_and(rem > lo, rem <= rows))
        def _(rows=rows):
            sub_block(n_full * MOE_SUB, rows)
        lo = rows
    lo = MOE_SUB
    for rows in MOE_WIDE:
        @pl.when(jnp.logical_and(count > lo, count <= rows))
        def _(rows=rows):
            sub_block(jnp.int32(0), rows)
        lo = rows

    @pl.when(g == N_GROUPS - 1)
    def _():
        f_ref[...] = acc_sc[...].astype(f_ref.dtype)


def _moe(hf, route, grp, counts, tri, w1, w3, w2):
    n, d = hf.shape
    _, _, ff = w1.shape
    tm = tri.shape[0]
    epg = EXPERTS_PER_GROUP
    grid_spec = pltpu.PrefetchScalarGridSpec(
        num_scalar_prefetch=1, grid=(n // tm, N_GROUPS),
        in_specs=[pl.BlockSpec((tm, d), lambda i, g, c: (i, 0)),
                  pl.BlockSpec((ROUTE_LANES, tm), lambda i, g, c: (0, i)),
                  pl.BlockSpec((SUBLANES, tm), lambda i, g, c: (0, i)),
                  pl.BlockSpec((tm, tm), lambda i, g, c: (0, 0)),
                  pl.BlockSpec((epg, d, ff), lambda i, g, c: (g, 0, 0)),
                  pl.BlockSpec((epg, d, ff), lambda i, g, c: (g, 0, 0)),
                  pl.BlockSpec((epg, ff, d), lambda i, g, c: (g, 0, 0))],
        out_specs=pl.BlockSpec((tm, d), lambda i, g, c: (i, 0)),
        scratch_shapes=[pltpu.VMEM((tm, d), F32), pltpu.VMEM((POS_ROWS * N_GROUPS, tm), F32),
                        pltpu.VMEM((tm, ROUTE_LANES), BF16)])
    return pl.pallas_call(
        _moe_kernel, grid_spec=grid_spec,
        out_shape=jax.ShapeDtypeStruct((n, d), BF16),
        compiler_params=_cparams("parallel", "arbitrary"),
        name="moe_experts",
    )(counts, hf, route, grp, tri, w1, w3, w2)


def _final_kernel(x_ref, f_ref, mod_ref, w_ref, o_ref):
    x = x_ref[...] + mod_ref[:, 5:6, :] * f_ref[...].astype(F32)
    o_ref[...] = x * lax.rsqrt(jnp.mean(x * x, axis=-1, keepdims=True) + EPS) * w_ref[...]


def _final_norm(xs, f_prev, mods, w, *, n_ctx_tiles):
    b, t, d = xs.shape
    tm = TOKEN_TILE
    nb = next(c for c in (4, 2, 1) if b % c == 0)
    n_lat = t // tm - n_ctx_tiles
    lat = lambda i, j: (j, i + n_ctx_tiles, 0)
    return pl.pallas_call(
        _final_kernel, grid=(n_lat, b // nb),
        in_specs=[pl.BlockSpec((nb, tm, d), lat), pl.BlockSpec((nb, tm, d), lat),
                  pl.BlockSpec((nb, 6, d), lambda i, j: (j, 0, 0)), pl.BlockSpec((1, d), lambda i, j: (0, 0))],
        out_specs=pl.BlockSpec((nb, tm, d), lambda i, j: (j, i, 0)),
        out_shape=jax.ShapeDtypeStruct((b, n_lat * tm, d), F32),
        compiler_params=_cparams("parallel", "parallel"),
        name="final_norm",
    )(xs, f_prev, mods, w)


def _rope_tables(n_ctx, n_lat):
    pos = jnp.arange(n_lat, dtype=jnp.int32)
    inv = ROPE_THETA ** (-jnp.arange(ROPE_NF, dtype=F32) / ROPE_NF)
    ang_r = (pos // GRID_W).astype(F32)[:, None] * inv
    ang_c = (pos % GRID_W).astype(F32)[:, None] * inv
    cos = jnp.concatenate([jnp.cos(ang_r)] * 2 + [jnp.cos(ang_c)] * 2, axis=-1)
    sin = jnp.concatenate([-jnp.sin(ang_r), jnp.sin(ang_r), -jnp.sin(ang_c), jnp.sin(ang_c)], axis=-1)
    cos = jnp.concatenate([jnp.ones((n_ctx, HEAD_DIM), F32), cos], axis=0)
    sin = jnp.concatenate([jnp.zeros((n_ctx, HEAD_DIM), F32), sin], axis=0)
    reps = QK_WIDTH // HEAD_DIM
    return jnp.tile(cos, (1, reps)), jnp.tile(sin, (1, reps))


def kernel(x, c, ctx, c_ctx, ada_w, ada_b, norm_mix_w, norm_ffn_w, w_in, q_norm_w, k_norm_w, conv_w,
           dn_A_log, dn_dt_bias, dn_norm_w, w_out, rg_w, rg_b, re_w, re_b, w1, w3, w2, final_norm_w):
    b, s, d = x.shape
    n_ctx = ctx.shape[1]
    depth = w_in.shape[0]
    t = n_ctx + s
    assert n_ctx % TOKEN_TILE == 0 and s % TOKEN_TILE == 0 and s % GRID_W == 0
    n_ctx_tiles = n_ctx // TOKEN_TILE

    xs = jnp.concatenate([ctx, x], axis=1)
    mod_rows = -(-(b + BATCH_PER_STEP) // SUBLANES) * SUBLANES
    cs = jnp.zeros((mod_rows, d), F32).at[:b].set(c).at[b:b + BATCH_PER_STEP].set(c_ctx)
    mods = _ada_mods(cs, ada_w, ada_b).reshape(depth, mod_rows, 6, d)

    cos_t, sin_t = _rope_tables(n_ctx, s)
    bd_qk = _block_ones(QK_WIDTH, HEAD_DIM)
    bd_dn = _block_ones(DN_WIDTH, DN_DIM)
    zeros16 = jnp.zeros((2 * DN_HEADS,), F32)
    n_main = w_in.shape[2] - 4 * DN_HEADS
    lane_i = jnp.arange((DN_HEADS // DN_GROUP) * SCALAR_LANES)
    gi, li = lane_i // SCALAR_LANES, lane_i % SCALAR_LANES
    di, ki, hi = li // (2 * DN_GROUP), (li // DN_GROUP) % 2, li % DN_GROUP
    bg_used = li < 4 * DN_GROUP
    bg_src = jnp.where(bg_used, ki * 2 * DN_HEADS + di * DN_HEADS + gi * DN_GROUP + hi, 0)
    moe_tile = next(c for c in MOE_TILES if (b * t) % c == 0)
    tri = (lax.broadcasted_iota(jnp.int32, (moe_tile, moe_tile), 0)
           <= lax.broadcasted_iota(jnp.int32, (moe_tile, moe_tile), 1)).astype(BF16)

    f_prev = None
    for l in range(depth):
        qkw = jnp.concatenate([jnp.tile(q_norm_w[l], ATTN_HEADS), jnp.tile(k_norm_w[l], KV_HEADS)])[None]
        dnp = jnp.stack([jnp.concatenate([zeros16, dn_A_log[l].reshape(-1)]),
                         jnp.concatenate([zeros16, dn_dt_bias[l].reshape(-1)])])
        dnp = jnp.where(bg_used, jnp.take(dnp, bg_src, axis=1), 0.0)
        w_tail = jnp.where(bg_used, jnp.take(w_in[l][:, n_main:], bg_src, axis=1), 0.0)
        w_in_l = jnp.concatenate([w_in[l][:, :n_main], w_tail], axis=1).astype(BF16)
        outs = _inproj(xs, f_prev, mods[l - 1] if l else None, mods[l], norm_mix_w[l][None],
                       w_in_l, qkw, cos_t, sin_t, bd_qk, dnp, n_ctx_tiles=n_ctx_tiles)
        qt, k, vt, dqkv, gate, bg = outs[:6]
        if l:
            xs = outs[6]
        a = _attention(qt, k, vt, n_ctx=n_ctx)
        dsum = _deltanet(dqkv, conv_w[l], bg, n_ctx=n_ctx)
        wo = w_out[l].astype(BF16)
        wrt = jnp.zeros((ROUTE_LANES, d), F32).at[:N_EXPERTS].set(re_w[l].T).at[
            N_EXPERTS:N_EXPERTS + N_GROUPS].set(rg_w[l].T).astype(BF16)
        br = jnp.zeros((ROUTE_LANES, 1), F32).at[:N_EXPERTS, 0].set(re_b[l]).at[
            N_EXPERTS:N_EXPERTS + N_GROUPS, 0].set(rg_b[l])
        xs, hf, route, grp, cnt = _outproj(
            a, dsum, gate, xs, mods[l], jnp.tile(dn_norm_w[l], DN_HEADS)[None], bd_dn,
            wo[:ATTN_WIDTH], wo[ATTN_WIDTH:], norm_ffn_w[l][None], wrt, br, n_ctx_tiles=n_ctx_tiles)
        counts = cnt[:, :, :N_GROUPS, 0].reshape(-1, moe_tile // TOKEN_TILE, N_GROUPS).sum(axis=1)
        f_prev = _moe(hf.reshape(b * t, d), route, grp,
                      counts.astype(jnp.int32).reshape(-1), tri,
                      w1[l].astype(BF16), w3[l].astype(BF16), w2[l].astype(BF16)).reshape(b, t, d)
    return _final_norm(xs, f_prev, mods[depth - 1], final_norm_w[None], n_ctx_tiles=n_ctx_tiles)
```

```python
import functools
import math

import jax
import jax.numpy as jnp
from jax import lax
from jax.experimental import pallas as pl
from jax.experimental.pallas import tpu as pltpu

F32 = jnp.float32
BF16 = jnp.bfloat16
HIGHEST = lax.Precision.HIGHEST

GRID_W = 64
HEAD_DIM = 64
ATTN_HEADS = 8
KV_HEADS = 2
GQA_GROUP = ATTN_HEADS // KV_HEADS
ATTN_WIDTH = ATTN_HEADS * HEAD_DIM
KV_WIDTH = KV_HEADS * HEAD_DIM
ATTN_SCALE = HEAD_DIM ** -0.5
LOG2_E = math.log2(math.e)
ROPE_THETA = 10000.0
ROPE_NF = HEAD_DIM // 4
DN_HEADS = 8
DN_DIM = 64
DN_WIDTH = DN_HEADS * DN_DIM
DN_SCALE = DN_DIM ** -0.5
DN_GROUP = 4
CONV_K = 5
CHUNK = 64
N_GROUPS = 4
EXPERTS_PER_GROUP = 8
N_EXPERTS = N_GROUPS * EXPERTS_PER_GROUP
EPS = 1e-6
QK_WIDTH = ATTN_WIDTH + KV_WIDTH
SUBLANES = 8
TOKEN_TILE = 256
BATCH_PER_STEP = 2
ATTN_KEY_TILE = 256
ONES_ROWS = 16
HEADS_PER_PASS = 8
HALO = 16
ROUTE_LANES = 128
MOE_TILES = (1024, 512, 256)
MOE_SUB = 256
MOE_TAILS = (64, 128, 256)
MOE_WIDE = (320, 384)
POS_ROWS = 16
VMEM_LIMIT = 56 * 1024 * 1024


def _cparams(*sem):
    return pltpu.CompilerParams(dimension_semantics=sem, vmem_limit_bytes=VMEM_LIMIT)


def _silu(x):
    return x * jax.nn.sigmoid(x)


def _block_ones(n, blk):
    i = lax.broadcasted_iota(jnp.int32, (n, n), 0) // blk
    j = lax.broadcasted_iota(jnp.int32, (n, n), 1) // blk
    return (i == j).astype(BF16)


def _group_mean_sq(x, ones_bd, width):
    return jnp.dot((x * x).astype(BF16), ones_bd, preferred_element_type=F32) * (1.0 / width)


def _ada_kernel(cs_ref, w_ref, b_ref, o_ref):
    o_ref[0] = jnp.dot(_silu(cs_ref[...]), w_ref[0], preferred_element_type=F32,
                       precision=HIGHEST) + b_ref[0]


def _ada_mods(cs, ada_w, ada_b):
    depth, d, n6 = ada_w.shape
    rows = cs.shape[0]
    tn = 1536
    return pl.pallas_call(
        _ada_kernel,
        grid=(depth, n6 // tn),
        in_specs=[pl.BlockSpec((rows, d), lambda l, j: (0, 0)),
                  pl.BlockSpec((1, d, tn), lambda l, j: (l, 0, j)),
                  pl.BlockSpec((1, 1, tn), lambda l, j: (l, 0, j))],
        out_specs=pl.BlockSpec((1, rows, tn), lambda l, j: (l, 0, j)),
        out_shape=jax.ShapeDtypeStruct((depth, rows, n6), F32),
        compiler_params=_cparams("parallel", "parallel"),
        name="ada_mods",
    )(cs, ada_w, ada_b.reshape(depth, 1, n6))


def _inproj_kernel(*refs, has_prev):
    if has_prev:
        (x_ref, f_ref, modp_ref, mod_ref, nw_ref, win_ref, qkw_ref, cos_ref, sin_ref, bd_ref, dnp_ref,
         qt_ref, k_ref, vt_ref, dqkv_ref, gate_ref, bg_ref, xo_ref) = refs
    else:
        (x_ref, mod_ref, nw_ref, win_ref, qkw_ref, cos_ref, sin_ref, bd_ref, dnp_ref,
         qt_ref, k_ref, vt_ref, dqkv_ref, gate_ref, bg_ref) = refs
    nb, tm = x_ref.shape[0], x_ref.shape[1]
    hs = []
    for r in range(nb):
        x = x_ref[r]
        if has_prev:
            x = x + modp_ref[r, 5:6, :] * f_ref[r].astype(F32)
            xo_ref[r] = x
        h = x * lax.rsqrt(jnp.mean(x * x, axis=-1, keepdims=True) + EPS) * nw_ref[...]
        hs.append((h * (1.0 + mod_ref[r, 1:2, :]) + mod_ref[r, 0:1, :]).astype(BF16))
    acc = jnp.dot(jnp.concatenate(hs, axis=0), win_ref[...], preferred_element_type=F32)
    qk = acc[:, :QK_WIDTH]
    qn = qk * lax.rsqrt(_group_mean_sq(qk, bd_ref[...], HEAD_DIM) + EPS) * qkw_ref[...]
    lane = lax.broadcasted_iota(jnp.int32, qn.shape, 1)
    partner = jnp.where(lane % (2 * ROPE_NF) < ROPE_NF,
                        pltpu.roll(qn, QK_WIDTH - ROPE_NF, 1), pltpu.roll(qn, ROPE_NF, 1))
    qr = (qn * jnp.concatenate([cos_ref[...]] * nb, axis=0)
          + partner * jnp.concatenate([sin_ref[...]] * nb, axis=0))
    c_v = QK_WIDTH
    c_dn = c_v + KV_WIDTH
    c_gate = c_dn + 3 * DN_WIDTH
    c_bg = c_gate + DN_WIDTH
    z = acc[:, c_bg:]
    zb = z + dnp_ref[1:2, :]
    softplus = jnp.maximum(zb, 0.0) + jnp.log1p(jnp.exp(-jnp.abs(zb)))
    lane_z = lax.broadcasted_iota(jnp.int32, z.shape, 1)
    bg = jnp.where(lane_z % (2 * DN_GROUP) < DN_GROUP, jax.nn.sigmoid(z), -jnp.exp(dnp_ref[0:1, :]) * softplus)
    for r in range(nb):
        rs = slice(r * tm, (r + 1) * tm)
        qt_ref[r] = (qr[rs, :ATTN_WIDTH] * (ATTN_SCALE * LOG2_E)).T.astype(BF16)
        k_ref[r] = qr[rs, ATTN_WIDTH:].astype(BF16)
        vt_ref[r] = acc[rs, c_v:c_dn].T.astype(BF16)
        dqkv_ref[r] = acc[rs, c_dn:c_gate].astype(BF16)
        gate_ref[r] = acc[rs, c_gate:c_bg].astype(BF16)
        for gi in range(DN_HEADS // DN_GROUP):
            bg_ref[r, gi] = bg[rs, gi * SCALAR_LANES:(gi + 1) * SCALAR_LANES]


def _inproj(xs, f_prev, mods_prev, mods, nw, win, qkw, cos_t, sin_t, bd, dnp, *, n_ctx_tiles):
    b, t, d = xs.shape
    tm = TOKEN_TILE
    nb = BATCH_PER_STEP
    assert b % nb == 0
    nt = t // tm
    proj = win.shape[1]
    has_prev = f_prev is not None
    tok = lambda i, j: (j, i, 0)
    modi = lambda i, j: (jnp.where(i < n_ctx_tiles, b // nb, j), 0, 0)
    const = lambda i, j: (0, 0)
    in_specs = [pl.BlockSpec((nb, tm, d), tok)]
    args = [xs]
    if has_prev:
        in_specs += [pl.BlockSpec((nb, tm, d), tok), pl.BlockSpec((nb, 6, d), modi)]
        args += [f_prev, mods_prev]
    in_specs += [pl.BlockSpec((nb, 6, d), modi), pl.BlockSpec((1, d), const),
                 pl.BlockSpec((d, proj), const), pl.BlockSpec((1, QK_WIDTH), const),
                 pl.BlockSpec((tm, QK_WIDTH), lambda i, j: (i, 0)),
                 pl.BlockSpec((tm, QK_WIDTH), lambda i, j: (i, 0)),
                 pl.BlockSpec((QK_WIDTH, QK_WIDTH), const), pl.BlockSpec((2, dnp.shape[1]), const)]
    args += [mods, nw, win, qkw, cos_t, sin_t, bd, dnp]
    seq_out = lambda w, dt: (pl.BlockSpec((nb, tm, w), tok), jax.ShapeDtypeStruct((b, t, w), dt))
    tr_out = lambda w: (pl.BlockSpec((nb, w, tm), lambda i, j: (j, 0, i)), jax.ShapeDtypeStruct((b, w, t), BF16))
    ng = DN_HEADS // DN_GROUP
    outs = [tr_out(ATTN_WIDTH), seq_out(KV_WIDTH, BF16), tr_out(KV_WIDTH),
            seq_out(3 * DN_WIDTH, BF16), seq_out(DN_WIDTH, BF16),
            (pl.BlockSpec((nb, ng, tm, SCALAR_LANES), lambda i, j: (j, 0, i, 0)),
             jax.ShapeDtypeStruct((b, ng, t, SCALAR_LANES), F32))]
    out_specs = [o[0] for o in outs]
    out_shape = [o[1] for o in outs]
    aliases = {}
    if has_prev:
        out_specs.append(pl.BlockSpec((nb, tm, d), tok))
        out_shape.append(jax.ShapeDtypeStruct((b, t, d), F32))
        aliases = {0: len(out_shape) - 1}
    return pl.pallas_call(
        functools.partial(_inproj_kernel, has_prev=has_prev),
        grid=(nt, b // nb), in_specs=in_specs, out_specs=out_specs, out_shape=out_shape,
        input_output_aliases=aliases,
        compiler_params=_cparams("parallel", "parallel"),
        name="inproj",
    )(*args)


def _attn_kernel(qt_ref, k_ref, vt_ref, o_ref, acc_sc, sa_sc, sb_sc, *, tk, n_ctx_q, n_ctx_k, n_all_k):
    tq = qt_ref.shape[2]
    n_kv = jnp.where(pl.program_id(1) < n_ctx_q, n_ctx_k, n_all_k)
    nh = HEADS_PER_PASS
    local = range(nh)
    last = n_kv - 1

    def tile_rows(i):
        return pl.ds(pl.multiple_of(i * tk, tk), tk)

    for h0 in range(0, ATTN_HEADS, nh):
        kv = [(h0 + j) // GQA_GROUP for j in local]
        acc_sc[...] = jnp.zeros(acc_sc.shape, F32)

        def scores(i, s_sc, h0=h0, kv=kv):
            rows = tile_rows(i)
            for j in local:
                s_sc[j] = jnp.dot(k_ref[0, rows, kv[j] * HEAD_DIM:(kv[j] + 1) * HEAD_DIM],
                                  qt_ref[0, (h0 + j) * HEAD_DIM:(h0 + j + 1) * HEAD_DIM, :],
                                  preferred_element_type=F32)

        def softmax_pv(i, s_sc, carry, kv=kv):
            rows = tile_rows(i)
            m_prev, l_prev = carry[:nh], carry[nh:]
            s = [s_sc[j] for j in local]
            m_new = [jnp.maximum(m_prev[j], jnp.max(s[j], axis=0, keepdims=True)) for j in local]
            p = [jnp.exp2(s[j] - m_new[j]) for j in local]
            alpha = [jnp.exp2(m_prev[j] - m_new[j]) for j in local]
            ones = jnp.ones((ONES_ROWS, tk), BF16)
            vt1 = {g: jnp.concatenate([vt_ref[0, g * HEAD_DIM:(g + 1) * HEAD_DIM, rows], ones], axis=0)
                   for g in sorted(set(kv))}
            pv = [jnp.dot(vt1[kv[j]], p[j].astype(BF16), preferred_element_type=F32) for j in local]
            for j in local:
                acc_sc[j] = alpha[j] * acc_sc[j] + pv[j][:HEAD_DIM]
            l_new = [alpha[j] * l_prev[j] + pv[j][HEAD_DIM:HEAD_DIM + 1] for j in local]
            return tuple(m_new) + tuple(l_new)

        init = (jnp.full((1, tq), -jnp.inf, F32),) * nh + (jnp.zeros((1, tq), F32),) * nh
        scores(0, sa_sc)
        scores(jnp.minimum(1, last), sb_sc)
        stats = softmax_pv(0, sa_sc, init)

        def pair(j, carry, scores=scores, softmax_pv=softmax_pv):
            a = 2 * j + 1
            scores(a + 1, sa_sc)
            carry = softmax_pv(a, sb_sc, carry)
            scores(jnp.minimum(a + 2, last), sb_sc)
            return softmax_pv(a + 1, sa_sc, carry)

        stats = lax.fori_loop(0, last // 2, pair, stats)
        og = jnp.concatenate([acc_sc[j] / stats[nh + j] for j in local], axis=0)
        o_ref[0, :, h0 * HEAD_DIM:(h0 + nh) * HEAD_DIM] = og.T.astype(BF16)


def _attention(qt, k, vt, *, n_ctx):
    b, _, t = qt.shape
    tq, tk = TOKEN_TILE, ATTN_KEY_TILE
    assert (n_ctx // tk) % 2 == 1 and (t // tk) % 2 == 1
    kern = functools.partial(_attn_kernel, tk=tk, n_ctx_q=n_ctx // tq, n_ctx_k=n_ctx // tk, n_all_k=t // tk)
    s_buf = pltpu.VMEM((HEADS_PER_PASS, tk, tq), F32)
    return pl.pallas_call(
        kern, grid=(b, t // tq),
        in_specs=[pl.BlockSpec((1, ATTN_WIDTH, tq), lambda bi, i: (bi, 0, i)),
                  pl.BlockSpec((1, t, KV_WIDTH), lambda bi, i: (bi, 0, 0)),
                  pl.BlockSpec((1, KV_WIDTH, t), lambda bi, i: (bi, 0, 0))],
        out_specs=pl.BlockSpec((1, tq, ATTN_WIDTH), lambda bi, i: (bi, i, 0)),
        out_shape=jax.ShapeDtypeStruct((b, t, ATTN_WIDTH), BF16),
        scratch_shapes=[pltpu.VMEM((HEADS_PER_PASS, HEAD_DIM, tq), F32), s_buf, s_buf],
        compiler_params=_cparams("parallel", "parallel"),
        name="gqa_attention",
    )(qt, k, vt)


DN_LANES = DN_GROUP * DN_DIM
DN_BLOCK = 4
INV_BLOCK = 16
SCALAR_LANES = 128


def _split3(x):
    x1 = x.astype(BF16)
    r1 = x - x1.astype(F32)
    x2 = r1.astype(BF16)
    x3 = (r1 - x2.astype(F32)).astype(BF16)
    return jnp.concatenate([x1, x2, x3], axis=1)


def _sum3(y):
    w = y.shape[1] // 3
    return y[:, :w] + y[:, w:2 * w] + y[:, 2 * w:]


def _dn_kernel(dq_ref, dk_ref, dv_ref, wq_ref, wk_ref, wv_ref, bg_ref, o_ref, mp_sc, n_sc, r_sc, gam_sc, *,
               n_chunks, n_ctx_chunks):
    c = CHUNK
    rows_b = DN_BLOCK * c
    ln = DN_LANES
    ri = lax.broadcasted_iota(jnp.int32, (rows_b, ln), 0)
    li = lax.broadcasted_iota(jnp.int32, (rows_b, ln), 1)
    same = (ri // c) == (li // c)
    i_in, j_in = ri % c, li % c
    eye_rc = i_in == j_in
    near = (i_in // INV_BLOCK) == (j_in // INV_BLOCK)
    e_row = lax.broadcasted_iota(jnp.int32, (3 * SCALAR_LANES, 2 * ln), 0) % SCALAR_LANES
    e_blk = lax.broadcasted_iota(jnp.int32, (3 * SCALAR_LANES, 2 * ln), 1) // c
    zl = lax.broadcasted_iota(jnp.int32, (rows_b, SCALAR_LANES), 1)
    nt_dims = (((1,), (1,)), ((), ()))
    tn_dims = (((0,), (0,)), ((), ()))

    def bdiag(x):
        return jnp.where(same, jnp.concatenate([x] * DN_GROUP, axis=0), jnp.zeros((), x.dtype))

    def fold(x):
        x = jnp.where(same, x, 0.0)
        return x[0:c] + x[c:2 * c] + x[2 * c:3 * c] + x[3 * c:4 * c]

    dirs = range(2)
    incl = [i_in >= j_in, i_in <= j_in]
    strict = [i_in > j_in, i_in < j_in]
    tri_bd = [jnp.logical_and(same, incl[d]).astype(BF16) for d in dirs]
    sel_last = [jnp.logical_and(same, j_in == (c - 1, 0)[d]).astype(BF16) for d in dirs]
    same_b = same.astype(BF16)
    col0 = [d * 2 * DN_GROUP for d in dirs]
    expand = [(e_row == e_blk + col0[d]).astype(BF16) for d in dirs]
    is_beta = [jnp.logical_and(zl >= col0[d], zl < col0[d] + DN_GROUP) for d in dirs]
    eye_f = eye_rc.astype(F32)
    chunks = range(DN_BLOCK)
    sls = [slice(ch * c, (ch + 1) * c) for ch in chunks]
    mm = lambda a, bmat: jnp.dot(a.astype(BF16), bmat, preferred_element_type=F32)

    def prepare(blk, carry):
        rs = pl.ds(pl.multiple_of(blk * rows_b, rows_b), rows_b)
        z = bg_ref[0, 0, rs, :]
        r0 = blk * rows_b
        keep_prev = jnp.where(jnp.logical_or(blk == 0, blk == n_ctx_chunks // DN_BLOCK), 0.0, 1.0)
        keep_next = jnp.where(jnp.logical_or(blk == n_ctx_chunks // DN_BLOCK - 1,
                                             blk == n_chunks // DN_BLOCK - 1), 0.0, 1.0)
        prev_rows = pl.ds(pl.multiple_of(jnp.maximum(r0 - HALO, 0), HALO), HALO)
        next_rows = pl.ds(pl.multiple_of(jnp.minimum(r0 + rows_b, n_chunks * c - HALO), HALO), HALO)

        def conv_silu(x_ref, w_ref):
            ext = jnp.concatenate([x_ref[0, prev_rows, :].astype(F32)[HALO - SUBLANES:] * keep_prev,
                                   x_ref[0, rs, :].astype(F32),
                                   x_ref[0, next_rows, :].astype(F32)[:SUBLANES] * keep_next], axis=0)
            y = None
            for j in range(CONV_K):
                shift = (CONV_K // 2 - j) % (rows_b + 2 * SUBLANES)
                tap = ext if shift == 0 else pltpu.roll(ext, shift, 0)
                term = tap[SUBLANES:SUBLANES + rows_b] * w_ref[j:j + 1, :]
                y = term if y is None else y + term
            return _silu(y)

        def l2n(x):
            return x * lax.rsqrt(jnp.dot((x * x).astype(BF16), same_b, preferred_element_type=F32) + EPS)

        q, k, v = conv_silu(dq_ref, wq_ref), conv_silu(dk_ref, wk_ref), conv_silu(dv_ref, wv_ref)
        q, k = l2n(q) * DN_SCALE, l2n(k)
        qbf = q.astype(BF16)
        kbf = k.astype(BF16)
        sc = [lax.dot_general(jnp.concatenate([kbf[sl], qbf[sl]], axis=0), bdiag(kbf[sl]), nt_dims,
                              preferred_element_type=F32) for sl in sls]
        cum = [_sum3(jnp.dot(tri_bd[d], _split3(z), preferred_element_type=F32)) for d in dirs]
        ex = [jnp.dot(_split3(jnp.where(is_beta[d], z, cum[d])), expand[d], preferred_element_type=F32)
              for d in dirs]
        beta = [ex[d][:, :ln] for d in dirs]
        gce = [ex[d][:, ln:] for d in dirs]
        g_row = [_sum3(jnp.dot(same_b, _split3(jnp.where(eye_rc, gce[d], 0.0)), preferred_element_type=F32))
                 for d in dirs]
        g_last = [_sum3(jnp.dot(sel_last[d], _split3(gce[d]), preferred_element_type=F32)) for d in dirs]
        decay = [jnp.where(incl[d], jnp.exp(jnp.where(incl[d], gce[d] - g_row[d], 0.0)), 0.0) for d in dirs]
        bdecay = [jnp.where(strict[d], beta[d] * decay[d], 0.0) for d in dirs]
        e_g = [jnp.exp(gce[d]) for d in dirs]
        kbe_d = [(k * beta[d] * e_g[d]).astype(BF16) for d in dirs]
        vb_d = [(v * beta[d]).astype(BF16) for d in dirs]
        q_dec_d = [q * e_g[d] for d in dirs]
        k_dec_d = [(k * jnp.exp(g_last[d] - gce[d])).astype(BF16) for d in dirs]
        gamma_d = [jnp.exp(g_last[d]) for d in dirs]
        lfull = [s[:c] * bdecay[d][sl] for d in dirs for s, sl in zip(sc, sls)]
        intra = [(s[c:] * decay[d][sl]).astype(BF16) for d in dirs for s, sl in zip(sc, sls)]
        vb = [vb_d[d][sl] for d in dirs for sl in sls]
        kbe = [kbe_d[d][sl] for d in dirs for sl in sls]
        q_dec = [q_dec_d[d][sl] for d in dirs for sl in sls]
        k_dec = [k_dec_d[d][sl] for d in dirs for sl in sls]
        gamma = [gamma_d[d][ch * c:ch * c + SUBLANES] for d in dirs for ch in chunks]
        near_c = near[sls[0]]
        p = [jnp.where(near_c, -lf, 0.0) for lf in lfull]
        l_off = [jnp.where(near_c, 0.0, lf).astype(BF16) for lf in lfull]
        tmat = [eye_f[sls[0]] + pi for pi in p]
        p = [mm(pi, bdiag(pi.astype(BF16))) for pi in p]
        for _ in range(int(math.log2(INV_BLOCK)) - 2):
            res = [mm(jnp.concatenate([ti, pi], axis=0), bdiag(pi.astype(BF16))) for ti, pi in zip(tmat, p)]
            tmat = [ti + ri[:c] for ti, ri in zip(tmat, res)]
            p = [ri[c:] for ri in res]
        tmat = [ti + mm(ti, bdiag(pi.astype(BF16))) for ti, pi in zip(tmat, p)]
        nmat = [mm(ti, bdiag(lo)) for ti, lo in zip(tmat, l_off)]
        n2 = [mm(ni, bdiag(ni.astype(BF16))) for ni in nmat]
        tmat = [ti + mm(qi, bdiag(ti.astype(BF16))) for ti, qi in zip(tmat, n2)]
        tmat = [ti - mm(ni, bdiag(ti.astype(BF16))) for ti, ni in zip(tmat, nmat)]
        uw = [mm(ti, jnp.concatenate([bdiag(vi), bdiag(ki)], axis=1)).astype(BF16)
              for ti, vi, ki in zip(tmat, vb, kbe)]
        aw_au = [mm(ai, jnp.concatenate([bdiag(x[:, ln:]), bdiag(x[:, :ln])], axis=1))
                 for ai, x in zip(intra, uw)]
        mn = [lax.dot_general(kd, jnp.concatenate([x[:, ln:], x[:, :ln]], axis=1), tn_dims,
                              preferred_element_type=F32) for kd, x in zip(k_dec, uw)]
        streams = range(len(mn))
        m_rc = [fold(mn[st][:, :ln]) for st in streams]
        n_rc = [fold(mn[st][:, ln:]) for st in streams]
        p_c = [q_dec[st] - aw_au[st][:, :ln] for st in streams]
        for st in streams:
            d, ci = st // DN_BLOCK, blk * DN_BLOCK + st % DN_BLOCK
            mp_sc[d, ci] = jnp.concatenate([m_rc[st], p_c[st]], axis=0).astype(BF16)
            n_sc[d, ci] = n_rc[st]
            r_sc[d, ci] = aw_au[st][:, ln:].astype(BF16)
            gam_sc[d, ci] = gamma[st]
        return carry

    lax.fori_loop(0, n_chunks // DN_BLOCK, prepare, 0)

    o_ref[...] = jnp.zeros(o_ref.shape, F32)

    def scan(s, states):
        ci = (s, jnp.where(s < n_ctx_chunks, n_ctx_chunks - 1 - s, n_chunks - 1 + n_ctx_chunks - s))
        res = [jnp.dot(mp_sc[d, ci[d]], bdiag(states[d].astype(BF16)), preferred_element_type=F32)
               for d in dirs]
        for d in dirs:
            rows = pl.ds(pl.multiple_of(ci[d] * c, c), c)
            o_ref[0, rows, :] += res[d][c:] + r_sc[d, ci[d]].astype(F32)
        return tuple(gam_sc[d, ci[d]][0:1] * states[d] - res[d][:c] + n_sc[d, ci[d]] for d in dirs)

    zero = jnp.zeros((c, ln), F32)
    lax.fori_loop(0, n_chunks, scan, (zero, zero), unroll=DN_BLOCK)


def _deltanet(dqkv, conv_w, bg, *, n_ctx):
    b, t, _ = dqkv.shape
    n_chunks = t // CHUNK
    block_rows = DN_BLOCK * CHUNK
    assert t % block_rows == 0 and n_ctx % block_rows == 0 and block_rows == DN_LANES == DN_GROUP * DN_DIM
    ng = DN_HEADS // DN_GROUP
    once = dict(pipeline_mode=pl.Buffered(1))
    part = lambda s: pl.BlockSpec((1, t, DN_LANES), lambda bi, gi, s=s: (bi, 0, s * ng + gi))
    wpart = lambda s: pl.BlockSpec((CONV_K, DN_LANES), lambda bi, gi, s=s: (0, s * ng + gi))
    kern = functools.partial(_dn_kernel, n_chunks=n_chunks, n_ctx_chunks=n_ctx // CHUNK)
    return pl.pallas_call(
        kern, grid=(b, ng),
        in_specs=[part(0), part(1), part(2), wpart(0), wpart(1), wpart(2),
                  pl.BlockSpec((1, 1, t, SCALAR_LANES), lambda bi, gi: (bi, gi, 0, 0), **once)],
        out_specs=pl.BlockSpec((1, t, DN_LANES), lambda bi, gi: (bi, 0, gi)),
        out_shape=jax.ShapeDtypeStruct((b, t, DN_WIDTH), F32),
        scratch_shapes=[pltpu.VMEM((2, n_chunks, 2 * CHUNK, DN_LANES), BF16),
                        pltpu.VMEM((2, n_chunks, CHUNK, DN_LANES), F32),
                        pltpu.VMEM((2, n_chunks, CHUNK, DN_LANES), BF16),
                        pltpu.VMEM((2, n_chunks, SUBLANES, DN_LANES), F32)],
        compiler_params=_cparams("parallel", "parallel"),
        name="gated_deltanet",
    )(dqkv, dqkv, dqkv, conv_w, conv_w, conv_w, bg)


def _outproj_kernel(a_ref, d_ref, gate_ref, x_ref, mod_ref, dnw_ref, bd_ref, woa_ref, wod_ref, nfw_ref,
                    wrt_ref, br_ref, xo_ref, hf_ref, route_ref, grp_ref, cnt_ref):
    dd = d_ref[0]
    gate = gate_ref[0].astype(F32)
    dn = dd * lax.rsqrt(_group_mean_sq(dd, bd_ref[...], DN_DIM) + EPS) * dnw_ref[...] * _silu(gate)
    y = (jnp.dot(a_ref[0], woa_ref[...], preferred_element_type=F32)
         + jnp.dot(dn.astype(BF16), wod_ref[...], preferred_element_type=F32))
    x = x_ref[0] + mod_ref[0, 2:3, :] * y
    xo_ref[0] = x
    h = x * lax.rsqrt(jnp.mean(x * x, axis=-1, keepdims=True) + EPS) * nfw_ref[...]
    h = h * (1.0 + mod_ref[0, 4:5, :]) + mod_ref[0, 3:4, :]
    hb = h.astype(BF16)
    hf_ref[0] = hb
    logits = lax.dot_general(wrt_ref[...], hb, (((1,), (1,)), ((), ())), preferred_element_type=F32) + br_ref[...]
    row = lax.broadcasted_iota(jnp.int32, logits.shape, 0)
    neg = jnp.float32(-jnp.inf)
    big = jnp.int32(ROUTE_LANES)

    def first_argmax(vals, vmax):
        return jnp.min(jnp.where(vals == vmax, row, big), axis=0, keepdims=True)

    is_g = jnp.logical_and(row >= N_EXPERTS, row < N_EXPERTS + N_GROUPS)
    gl = jnp.where(is_g, logits, neg)
    g_max = jnp.max(gl, axis=0, keepdims=True)
    g_sel = first_argmax(gl, g_max) - N_EXPERTS
    p_g = 1.0 / jnp.sum(jnp.exp(gl - g_max), axis=0, keepdims=True)
    e_lo = g_sel * EXPERTS_PER_GROUP
    in_grp = jnp.logical_and(row >= e_lo, row < e_lo + EXPERTS_PER_GROUP)
    el = jnp.where(in_grp, logits, neg)
    e_max = jnp.max(el, axis=0, keepdims=True)
    i1 = first_argmax(el, e_max)
    el2 = jnp.where(row == i1, neg, el)
    e_max2 = jnp.max(el2, axis=0, keepdims=True)
    i2 = first_argmax(el2, e_max2)
    p2 = jnp.exp(e_max2 - e_max)
    w1 = p_g / (1.0 + p2)
    w2 = p_g * p2 / (1.0 + p2)
    route_ref[...] = jnp.where(row == i1, w1, jnp.where(row == i2, w2, 0.0)).astype(BF16)
    grp_ref[...] = jnp.broadcast_to(g_sel.astype(F32), grp_ref.shape)
    in_g = jnp.where(lax.broadcasted_iota(jnp.int32, grp_ref.shape, 0) == g_sel, 1.0, 0.0)
    cnt_ref[0, 0] = jnp.broadcast_to(jnp.sum(in_g, axis=1, keepdims=True), (SUBLANES, ROUTE_LANES))


def _outproj(a, dsum, gate, xs, mods, dnw, bd, woa, wod, nfw, wrt, br, *, n_ctx_tiles):
    b, t, d = xs.shape
    tm = TOKEN_TILE
    tok = lambda i, j: (j, i, 0)
    const = lambda i, j: (0, 0)
    flat = lambda i, j: (0, j * (t // tm) + i)
    modi = lambda i, j: (jnp.where(i < n_ctx_tiles, b, j), 0, 0)
    return pl.pallas_call(
        _outproj_kernel, grid=(t // tm, b),
        in_specs=[pl.BlockSpec((1, tm, ATTN_WIDTH), tok), pl.BlockSpec((1, tm, DN_WIDTH), tok),
                  pl.BlockSpec((1, tm, DN_WIDTH), tok), pl.BlockSpec((1, tm, d), tok),
                  pl.BlockSpec((1, 6, d), modi), pl.BlockSpec((1, DN_WIDTH), const),
                  pl.BlockSpec((DN_WIDTH, DN_WIDTH), const), pl.BlockSpec((ATTN_WIDTH, d), const),
                  pl.BlockSpec((DN_WIDTH, d), const), pl.BlockSpec((1, d), const),
                  pl.BlockSpec((ROUTE_LANES, d), const), pl.BlockSpec((ROUTE_LANES, 1), const)],
        out_specs=[pl.BlockSpec((1, tm, d), tok), pl.BlockSpec((1, tm, d), tok),
                   pl.BlockSpec((ROUTE_LANES, tm), flat), pl.BlockSpec((SUBLANES, tm), flat),
                   pl.BlockSpec((1, 1, SUBLANES, ROUTE_LANES), lambda i, j: (j, i, 0, 0))],
        out_shape=[jax.ShapeDtypeStruct((b, t, d), F32), jax.ShapeDtypeStruct((b, t, d), BF16),
                   jax.ShapeDtypeStruct((ROUTE_LANES, b * t), BF16),
                   jax.ShapeDtypeStruct((SUBLANES, b * t), F32),
                   jax.ShapeDtypeStruct((b, t // tm, SUBLANES, ROUTE_LANES), F32)],
        input_output_aliases={3: 0},
        compiler_params=_cparams("parallel", "parallel"),
        name="outproj_router",
    )(a, dsum, gate, xs, mods, dnw, bd, woa, wod, nfw, wrt, br)


def _moe_kernel(cnt_ref, h_ref, r_ref, grp_ref, tri_ref, ex_ref, w1_ref, w3_ref, w2_ref, f_ref, acc_sc, pos_sc,
                route_sc):
    m, g = pl.program_id(0), pl.program_id(1)
    tm = h_ref.shape[0]
    ff = w1_ref.shape[2]

    @pl.when(g == 0)
    def _():
        acc_sc[...] = jnp.zeros(acc_sc.shape, F32)
        route_sc[...] = r_ref[...].astype(F32).T.astype(BF16)
        grp = jnp.broadcast_to(grp_ref[0:1, :], (POS_ROWS * N_GROUPS, tm))
        row_g = lax.broadcasted_iota(jnp.int32, grp.shape, 0) // POS_ROWS
        member = grp == row_g.astype(F32)
        prefix = jnp.dot(member.astype(BF16), tri_ref[...], preferred_element_type=F32)
        pos_sc[...] = jnp.where(member, prefix - 1.0, -1.0)

    pos = pos_sc[pl.ds(pl.multiple_of(g * POS_ROWS, POS_ROWS), 1), :]
    expand = ex_ref[...]

    def sub_block(first_slot, rows):
        slot = lax.broadcasted_iota(jnp.int32, (rows, tm), 0) + first_slot
        sel = (pos == slot.astype(F32)).astype(BF16)
        xs = jnp.dot(sel, h_ref[...], preferred_element_type=F32).astype(BF16)
        wt = jnp.dot(sel, route_sc[...], preferred_element_type=F32).astype(BF16)
        wexp = jnp.dot(wt, expand, preferred_element_type=F32)
        experts = range(EXPERTS_PER_GROUP)
        a = [jnp.dot(xs, w1_ref[e], preferred_element_type=F32) for e in experts]
        gate = [jnp.dot(xs, w3_ref[e], preferred_element_type=F32) for e in experts]
        mid = [(_silu(a[e]) * gate[e] * wexp[:, e * ff:(e + 1) * ff]).astype(BF16) for e in experts]
        y = sum(jnp.dot(mid[e], w2_ref[e], preferred_element_type=F32) for e in experts)
        acc_sc[...] += lax.dot_general(sel, y.astype(BF16), (((0,), (0,)), ((), ())),
                                       preferred_element_type=F32)

    count = cnt_ref[m * N_GROUPS + g]
    wide = jnp.logical_and(count > MOE_SUB, count <= MOE_WIDE[-1])
    n_full = jnp.where(wide, 0, count // MOE_SUB)
    rem = jnp.where(wide, 0, count - n_full * MOE_SUB)

    def full_block(sb, carry):
        sub_block(sb * MOE_SUB, MOE_SUB)
        return carry

    lax.fori_loop(0, n_full, full_block, 0)
    lo = 0
    for rows in MOE_TAILS:
        @pl.when(jnp.logical_and(rem > lo, rem <= rows))
        def _(rows=rows):
            sub_block(n_full * MOE_SUB, rows)
        lo = rows
    lo = MOE_SUB
    for rows in MOE_WIDE:
        @pl.when(jnp.logical_and(count > lo, count <= rows))
        def _(rows=rows):
            sub_block(jnp.int32(0), rows)
        lo = rows

    @pl.when(g == N_GROUPS - 1)
    def _():
        f_ref[...] = acc_sc[...].astype(f_ref.dtype)


def _moe(hf, route, grp, counts, tri, w1, w3, w2):
    n, d = hf.shape
    _, _, ff = w1.shape
    tm = tri.shape[0]
    epg = EXPERTS_PER_GROUP
    ex_row = lax.broadcasted_iota(jnp.int32, (N_GROUPS, ROUTE_LANES, epg * ff), 1)
    ex_col = lax.broadcasted_iota(jnp.int32, (N_GROUPS, ROUTE_LANES, epg * ff), 2) // ff
    ex_grp = lax.broadcasted_iota(jnp.int32, (N_GROUPS, ROUTE_LANES, epg * ff), 0)
    expand = (ex_row == ex_grp * epg + ex_col).astype(BF16).reshape(N_GROUPS * ROUTE_LANES, epg * ff)
    grid_spec = pltpu.PrefetchScalarGridSpec(
        num_scalar_prefetch=1, grid=(n // tm, N_GROUPS),
        in_specs=[pl.BlockSpec((tm, d), lambda i, g, c: (i, 0)),
                  pl.BlockSpec((ROUTE_LANES, tm), lambda i, g, c: (0, i)),
                  pl.BlockSpec((SUBLANES, tm), lambda i, g, c: (0, i)),
                  pl.BlockSpec((tm, tm), lambda i, g, c: (0, 0)),
                  pl.BlockSpec((ROUTE_LANES, epg * ff), lambda i, g, c: (g, 0)),
                  pl.BlockSpec((epg, d, ff), lambda i, g, c: (g, 0, 0)),
                  pl.BlockSpec((epg, d, ff), lambda i, g, c: (g, 0, 0)),
                  pl.BlockSpec((epg, ff, d), lambda i, g, c: (g, 0, 0))],
        out_specs=pl.BlockSpec((tm, d), lambda i, g, c: (i, 0)),
        scratch_shapes=[pltpu.VMEM((tm, d), F32), pltpu.VMEM((POS_ROWS * N_GROUPS, tm), F32),
                        pltpu.VMEM((tm, ROUTE_LANES), BF16)])
    return pl.pallas_call(
        _moe_kernel, grid_spec=grid_spec,
        out_shape=jax.ShapeDtypeStruct((n, d), BF16),
        compiler_params=_cparams("parallel", "arbitrary"),
        name="moe_experts",
    )(counts, hf, route, grp, tri, expand, w1, w3, w2)


def _final_kernel(x_ref, f_ref, mod_ref, w_ref, o_ref):
    x = x_ref[...] + mod_ref[:, 5:6, :] * f_ref[...].astype(F32)
    o_ref[...] = x * lax.rsqrt(jnp.mean(x * x, axis=-1, keepdims=True) + EPS) * w_ref[...]


def _final_norm(xs, f_prev, mods, w, *, n_ctx_tiles):
    b, t, d = xs.shape
    tm = TOKEN_TILE
    nb = next(c for c in (4, 2, 1) if b % c == 0)
    n_lat = t // tm - n_ctx_tiles
    lat = lambda i, j: (j, i + n_ctx_tiles, 0)
    return pl.pallas_call(
        _final_kernel, grid=(n_lat, b // nb),
        in_specs=[pl.BlockSpec((nb, tm, d), lat), pl.BlockSpec((nb, tm, d), lat),
                  pl.BlockSpec((nb, 6, d), lambda i, j: (j, 0, 0)), pl.BlockSpec((1, d), lambda i, j: (0, 0))],
        out_specs=pl.BlockSpec((nb, tm, d), lambda i, j: (j, i, 0)),
        out_shape=jax.ShapeDtypeStruct((b, n_lat * tm, d), F32),
        compiler_params=_cparams("parallel", "parallel"),
        name="final_norm",
    )(xs, f_prev, mods, w)


def _rope_tables(n_ctx, n_lat):
    pos = jnp.arange(n_lat, dtype=jnp.int32)
    inv = ROPE_THETA ** (-jnp.arange(ROPE_NF, dtype=F32) / ROPE_NF)
    ang_r = (pos // GRID_W).astype(F32)[:, None] * inv
    ang_c = (pos % GRID_W).astype(F32)[:, None] * inv
    cos = jnp.concatenate([jnp.cos(ang_r)] * 2 + [jnp.cos(ang_c)] * 2, axis=-1)
    sin = jnp.concatenate([-jnp.sin(ang_r), jnp.sin(ang_r), -jnp.sin(ang_c), jnp.sin(ang_c)], axis=-1)
    cos = jnp.concatenate([jnp.ones((n_ctx, HEAD_DIM), F32), cos], axis=0)
    sin = jnp.concatenate([jnp.zeros((n_ctx, HEAD_DIM), F32), sin], axis=0)
    reps = QK_WIDTH // HEAD_DIM
    return jnp.tile(cos, (1, reps)), jnp.tile(sin, (1, reps))


def kernel(x, c, ctx, c_ctx, ada_w, ada_b, norm_mix_w, norm_ffn_w, w_in, q_norm_w, k_norm_w, conv_w,
           dn_A_log, dn_dt_bias, dn_norm_w, w_out, rg_w, rg_b, re_w, re_b, w1, w3, w2, final_norm_w):
    b, s, d = x.shape
    n_ctx = ctx.shape[1]
    depth = w_in.shape[0]
    t = n_ctx + s
    assert n_ctx % TOKEN_TILE == 0 and s % TOKEN_TILE == 0 and s % GRID_W == 0
    n_ctx_tiles = n_ctx // TOKEN_TILE

    xs = jnp.concatenate([ctx, x], axis=1)
    mod_rows = -(-(b + BATCH_PER_STEP) // SUBLANES) * SUBLANES
    cs = jnp.zeros((mod_rows, d), F32).at[:b].set(c).at[b:b + BATCH_PER_STEP].set(c_ctx)
    mods = _ada_mods(cs, ada_w, ada_b).reshape(depth, mod_rows, 6, d)

    cos_t, sin_t = _rope_tables(n_ctx, s)
    bd_qk = _block_ones(QK_WIDTH, HEAD_DIM)
    bd_dn = _block_ones(DN_WIDTH, DN_DIM)
    zeros16 = jnp.zeros((2 * DN_HEADS,), F32)
    n_main = w_in.shape[2] - 4 * DN_HEADS
    lane_i = jnp.arange((DN_HEADS // DN_GROUP) * SCALAR_LANES)
    gi, li = lane_i // SCALAR_LANES, lane_i % SCALAR_LANES
    di, ki, hi = li // (2 * DN_GROUP), (li // DN_GROUP) % 2, li % DN_GROUP
    bg_used = li < 4 * DN_GROUP
    bg_src = jnp.where(bg_used, ki * 2 * DN_HEADS + di * DN_HEADS + gi * DN_GROUP + hi, 0)
    moe_tile = next(c for c in MOE_TILES if (b * t) % c == 0)
    tri = (lax.broadcasted_iota(jnp.int32, (moe_tile, moe_tile), 0)
           <= lax.broadcasted_iota(jnp.int32, (moe_tile, moe_tile), 1)).astype(BF16)

    f_prev = None
    for l in range(depth):
        qkw = jnp.concatenate([jnp.tile(q_norm_w[l], ATTN_HEADS), jnp.tile(k_norm_w[l], KV_HEADS)])[None]
        dnp = jnp.stack([jnp.concatenate([zeros16, dn_A_log[l].reshape(-1)]),
                         jnp.concatenate([zeros16, dn_dt_bias[l].reshape(-1)])])
        dnp = jnp.where(bg_used, jnp.take(dnp, bg_src, axis=1), 0.0)
        w_tail = jnp.where(bg_used, jnp.take(w_in[l][:, n_main:], bg_src, axis=1), 0.0)
        w_in_l = jnp.concatenate([w_in[l][:, :n_main], w_tail], axis=1).astype(BF16)
        outs = _inproj(xs, f_prev, mods[l - 1] if l else None, mods[l], norm_mix_w[l][None],
                       w_in_l, qkw, cos_t, sin_t, bd_qk, dnp, n_ctx_tiles=n_ctx_tiles)
        qt, k, vt, dqkv, gate, bg = outs[:6]
        if l:
            xs = outs[6]
        a = _attention(qt, k, vt, n_ctx=n_ctx)
        dsum = _deltanet(dqkv, conv_w[l], bg, n_ctx=n_ctx)
        wo = w_out[l].astype(BF16)
        wrt = jnp.zeros((ROUTE_LANES, d), F32).at[:N_EXPERTS].set(re_w[l].T).at[
            N_EXPERTS:N_EXPERTS + N_GROUPS].set(rg_w[l].T).astype(BF16)
        br = jnp.zeros((ROUTE_LANES, 1), F32).at[:N_EXPERTS, 0].set(re_b[l]).at[
            N_EXPERTS:N_EXPERTS + N_GROUPS, 0].set(rg_b[l])
        xs, hf, route, grp, cnt = _outproj(
            a, dsum, gate, xs, mods[l], jnp.tile(dn_norm_w[l], DN_HEADS)[None], bd_dn,
            wo[:ATTN_WIDTH], wo[ATTN_WIDTH:], norm_ffn_w[l][None], wrt, br, n_ctx_tiles=n_ctx_tiles)
        counts = cnt[:, :, :N_GROUPS, 0].reshape(-1, moe_tile // TOKEN_TILE, N_GROUPS).sum(axis=1)
        f_prev = _moe(hf.reshape(b * t, d), route, grp,
                      counts.astype(jnp.int32).reshape(-1), tri,
                      w1[l].astype(BF16), w3[l].astype(BF16), w2[l].astype(BF16)).reshape(b, t, d)
    return _final_norm(xs, f_prev, mods[depth - 1], final_norm_w[None], n_ctx_tiles=n_ctx_tiles)
```
